```python
import math
import jax
import jax.numpy as jnp
from jax import lax
import numpy as np

D_MODEL = 4096
BATCH = 2
SEQ = 4096
DEPTH = 2
DEC_BATCH = 32
DEC_SEQ = 32
PAST_LEN = 1024

CHUNK = 64
QUERY_BLOCK = 128
EPS = 1e-6
H_A = 8
DK_A = 128
DV_A = 256
W_A = H_A * DV_A
ROPE_BASE = 10000.0
H_B = 16
DK_B = 128
DV_B = 128
W_B = H_B * DV_B
CONV_W = 4
CONV_CH = 2 * H_B * DK_B + H_B * DV_B
H_C = 16
H_KV = 4
DH_C = 128
W_C = H_C * DH_C
H_IDX = 32
D_IDX = 64
TOPK_MAX = 256
NUM_BUCKETS = 32
MAX_DISTANCE = 1024
D_FF = ((8 * D_MODEL + 3 * 256 - 1) // (3 * 256)) * 256
SPLIT_SIZES = [H_A * DK_A, H_A * DK_A, H_A * DV_A, H_A * DV_A,
               CONV_CH, H_B * DV_B, H_B, H_B,
               H_C * DH_C, H_KV * DH_C, H_KV * DH_C, H_IDX * D_IDX, D_IDX, H_IDX,
               D_MODEL, D_MODEL, D_MODEL]
D_IN = (2 * H_A * DK_A + 2 * H_A * DV_A + CONV_CH + H_B * DV_B + 2 * H_B
        + H_C * DH_C + 2 * H_KV * DH_C + H_IDX * D_IDX + D_IDX + H_IDX + 3 * D_MODEL)

kernel_name = 'hybrid_retention_gdn_dsa_stream_step'


def rms_norm(x, g):
    xf = x.astype(jnp.float32)
    y = xf * lax.rsqrt(jnp.mean(xf * xf, axis=-1, keepdims=True) + EPS)
    return (y * g.astype(jnp.float32)).astype(x.dtype)


def l2norm(x):
    return x * lax.rsqrt(jnp.sum(x * x, axis=-1, keepdims=True) + EPS)


def rotary(x, pos):
    half = x.shape[-1] // 2
    inv = ROPE_BASE ** (-jnp.arange(half, dtype=jnp.float32) / half)
    ang = pos.astype(jnp.float32)[:, None] * inv[None, :]
    cos = jnp.cos(ang)[None, :, None, :]
    sin = jnp.sin(ang)[None, :, None, :]
    xf = x.astype(jnp.float32)
    x1, x2 = xf[..., :half], xf[..., half:]
    return jnp.concatenate([x1 * cos - x2 * sin, x2 * cos + x1 * sin], axis=-1)


def retention_chunked(q, k, v, s0, log_gamma, chunk):
    b, t, h, _ = q.shape
    n = t // chunk

    def to_chunks(a):
        return jnp.moveaxis(a.reshape(b, n, chunk, *a.shape[2:]), 1, 0)

    i = jnp.arange(chunk, dtype=jnp.float32)
    rel = i[:, None] - i[None, :]
    inner_decay = jnp.exp(jnp.where(rel[None] >= 0, rel[None] * log_gamma[:, None, None], -jnp.inf))
    q_decay = jnp.exp((i[:, None] + 1.0) * log_gamma[None, :])
    k_decay = jnp.exp((chunk - 1.0 - i[:, None]) * log_gamma[None, :])
    chunk_decay = jnp.exp(chunk * log_gamma)

    def step(s, xs):
        qc, kc, vc = xs
        att = jnp.einsum('bihd,bjhd->bhij', qc, kc) * inner_decay
        inner = jnp.einsum('bhij,bjhe->bihe', att, vc)
        cross = jnp.einsum('bihd,bhde->bihe', qc * q_decay[None, :, :, None], s)
        s_new = s * chunk_decay[None, :, None, None] + jnp.einsum(
            'bjhd,bjhe->bhde', kc * k_decay[None, :, :, None], vc)
        return s_new, inner + cross

    s_fin, out = lax.scan(step, s0, (to_chunks(q), to_chunks(k), to_chunks(v)))
    return jnp.moveaxis(out, 0, 1).reshape(b, t, h, -1), s_fin


def gated_delta_chunked(q, k, v, g, beta, s0, chunk):
    b, t, h, _ = q.shape
    n = t // chunk

    def to_chunks(a):
        a = a.reshape(b, n, chunk, h, *a.shape[3:])
        return jnp.moveaxis(jnp.moveaxis(a, 1, 0), 3, 2)

    incl = jnp.tril(jnp.ones((chunk, chunk), dtype=bool))
    strict = jnp.tril(jnp.ones((chunk, chunk), dtype=bool), -1)
    eye = jnp.eye(chunk, dtype=jnp.float32)

    def step(s, xs):
        qc, kc, vc, gc, bc = xs
        gcum = jnp.cumsum(gc, axis=-1)
        gam = jnp.exp(jnp.where(incl, gcum[..., :, None] - gcum[..., None, :], -jnp.inf))
        kb = kc * bc[..., None]
        a = jnp.where(strict, jnp.einsum('bhid,bhjd->bhij', kb, kc) * gam, 0.0)
        tinv = lax.linalg.triangular_solve(eye + a, jnp.broadcast_to(eye, a.shape),
                                           left_side=True, lower=True)
        u = tinv @ (vc * bc[..., None])
        w = tinv @ (kb * jnp.exp(gcum)[..., None])
        v_new = u - w @ s
        qk = jnp.einsum('bhid,bhjd->bhij', qc, kc) * gam
        o = (qc * jnp.exp(gcum)[..., None]) @ s + qk @ v_new
        g_last = gcum[..., -1:]
        s_new = s * jnp.exp(g_last)[..., None] + jnp.einsum(
            'bhjd,bhje->bhde', kc * jnp.exp(g_last - gcum)[..., None], v_new)
        return s_new, o

    s_fin, out = lax.scan(step, s0, (to_chunks(q), to_chunks(k), to_chunks(v), to_chunks(g), to_chunks(beta)))
    out = jnp.moveaxis(jnp.moveaxis(out, 2, 3), 0, 1).reshape(b, t, h, -1)
    return out, s_fin


def t5_bucket(rel):
    half = NUM_BUCKETS // 2
    exact = half // 2
    n = jnp.abs(rel)
    large = exact + (jnp.log(jnp.maximum(n, 1).astype(jnp.float32) / exact)
                     / math.log(MAX_DISTANCE / exact) * (half - exact)).astype(jnp.int32)
    large = jnp.minimum(large, half - 1)
    return jnp.where(rel > 0, half, 0) + jnp.where(n < exact, n, large)


def dsa_block(q, q_idx, w_idx, qpos, k_all, v_all, k_idx_all, rel_bias, topk):
    b, nq = q.shape[:2]
    kpos = jnp.arange(k_all.shape[1])
    score = jnp.einsum('bqhd,bsd->bqhs', q_idx, k_idx_all)
    index_score = jnp.einsum('bqh,bqhs->bqs', w_idx, jax.nn.relu(score)).astype(jnp.float32)
    visible = (kpos[None, :] // CHUNK) <= (qpos[:, None] // CHUNK)
    index_score = jnp.where(visible[None], index_score, -jnp.inf)
    top_val, sel = lax.top_k(index_score, topk)
    valid = jnp.isfinite(top_val)
    gather = jax.vmap(lambda rows, idx: rows[idx])
    k_sel = gather(k_all, sel)
    v_sel = gather(v_all, sel)
    logits = jnp.einsum('bqhgd,bqnhd->bqhgn', q, k_sel).astype(jnp.float32)
    bias = rel_bias[t5_bucket(sel - qpos[None, :, None])]
    bias = jnp.transpose(bias.reshape(b, nq, topk, H_KV, H_C // H_KV), (0, 1, 3, 4, 2))
    logits = jnp.where(valid[:, :, None, None, :], logits + bias.astype(jnp.float32), -jnp.inf)
    p = jax.nn.softmax(logits, axis=-1).astype(v_all.dtype)
    return jnp.einsum('bqhgn,bqnhd->bqhgd', p, v_sel)


def dsa_attention(q, q_idx, w_idx, qpos, k_all, v_all, k_idx_all, rel_bias, topk):
    b, t = q.shape[:2]
    qb = QUERY_BLOCK if t % QUERY_BLOCK == 0 else t
    nb = t // qb

    def blocks(a):
        return jnp.moveaxis(a.reshape(b, nb, qb, *a.shape[2:]), 1, 0)

    out = lax.map(lambda xs: dsa_block(xs[0], xs[1], xs[2], xs[3], k_all, v_all, k_idx_all, rel_bias, topk),
                  (blocks(q), blocks(q_idx), blocks(w_idx), qpos.reshape(nb, qb)))
    return jnp.moveaxis(out, 0, 1).reshape(b, t, W_C)


def trunk_layer(h, past_k, past_v, past_kidx, s_ret, s_delta, s_conv,
                norm_mix, w_in, ret_gn, conv_w, a_log, dt_bias, d_norm, rel_bias,
                w_br_a, w_br_b, w_br_c, w_out, norm_ffn, w_gate, w_up, w_down):
    dt = h.dtype
    f32 = jnp.float32
    b, t, _ = h.shape
    p_len = past_k.shape[1]
    pos = p_len + jnp.arange(t)
    chunk = CHUNK if t % CHUNK == 0 else t
    topk = min(TOPK_MAX, (p_len + t) // 4)

    u = rms_norm(h, norm_mix)
    proj = u @ w_in
    offsets = np.cumsum(SPLIT_SIZES)[:-1].tolist()
    (q_a, k_a, v_a, g_a, qkv_b, z_b, a_b, b_b, q_c, k_c, v_c, q_i, k_i, w_i,
     gate_a, gate_b, gate_c) = jnp.split(proj, offsets, axis=-1)

    qa = rotary(q_a.reshape(b, t, H_A, DK_A), pos)
    ka = rotary(k_a.reshape(b, t, H_A, DK_A), pos) * DK_A ** -0.5
    va = v_a.reshape(b, t, H_A, DV_A).astype(f32)
    log_gamma = jnp.log(1.0 - 2.0 ** (-5.0 - jnp.arange(H_A, dtype=f32)))
    o_a, ret_new = retention_chunked(qa, ka, va, s_ret.astype(f32), log_gamma, chunk)
    mu = jnp.mean(o_a, axis=-1, keepdims=True)
    var = jnp.mean(jnp.square(o_a - mu), axis=-1, keepdims=True)
    o_a = ((o_a - mu) * lax.rsqrt(var + EPS)).reshape(b, t, W_A) * ret_gn.astype(f32)
    o_a = (jax.nn.silu(g_a.astype(f32)) * o_a).astype(dt)

    conv_in = jnp.concatenate([s_conv, qkv_b], axis=1)
    conv_new = conv_in[:, -(CONV_W - 1):]
    qkv = jax.nn.silu(lax.conv_general_dilated(
        conv_in, conv_w[:, None, :], window_strides=(1,), padding='VALID',
        dimension_numbers=('NWC', 'WIO', 'NWC'), feature_group_count=CONV_CH))
    q_b, k_b, v_b = jnp.split(qkv, [H_B * DK_B, 2 * H_B * DK_B], axis=-1)
    qb_ = l2norm(q_b.reshape(b, t, H_B, DK_B).astype(f32)) * DK_B ** -0.5
    kb_ = l2norm(k_b.reshape(b, t, H_B, DK_B).astype(f32))
    vb_ = v_b.reshape(b, t, H_B, DV_B).astype(f32)
    beta = jax.nn.sigmoid(b_b.astype(f32))
    g_log = -jnp.exp(a_log.astype(f32)) * jax.nn.softplus(a_b.astype(f32) + dt_bias.astype(f32))
    o_b, delta_new = gated_delta_chunked(qb_, kb_, vb_, g_log, beta, s_delta.astype(f32), chunk)
    o_b = o_b * lax.rsqrt(jnp.mean(o_b * o_b, axis=-1, keepdims=True) + EPS) * d_norm.astype(f32)
    o_b = (o_b.reshape(b, t, W_B) * jax.nn.silu(z_b.astype(f32))).astype(dt)

    qc = q_c.reshape(b, t, H_KV, H_C // H_KV, DH_C) * DH_C ** -0.5
    kc = k_c.reshape(b, t, H_KV, DH_C)
    vc = v_c.reshape(b, t, H_KV, DH_C)
    qi = q_i.reshape(b, t, H_IDX, D_IDX) * D_IDX ** -0.5
    wi = w_i * H_IDX ** -0.5
    k_all = jnp.concatenate([past_k, kc], axis=1)
    v_all = jnp.concatenate([past_v, vc], axis=1)
    kidx_all = jnp.concatenate([past_kidx, k_i], axis=1)
    o_c = dsa_attention(qc, qi, wi, pos, k_all, v_all, kidx_all, rel_bias, topk)

    merged = (jax.nn.sigmoid(gate_a) * (o_a @ w_br_a)
              + jax.nn.sigmoid(gate_b) * (o_b @ w_br_b)
              + jax.nn.sigmoid(gate_c) * (o_c @ w_br_c))
    h = h + merged @ w_out

    u2 = rms_norm(h, norm_ffn)
    h = h + (jax.nn.silu(u2 @ w_gate) * (u2 @ w_up)) @ w_down
    return h, (kc, vc, k_i, ret_new.astype(s_ret.dtype), delta_new.astype(s_delta.dtype), conv_new)


def setup_inputs(seed: int = 0) -> dict:
    key = jax.random.key(seed)
    ks = jax.random.split(key, 32)
    f32 = jnp.float32

    def nrm(k, shape, scale):
        return jax.random.normal(k, shape, f32) * scale

    dt_init = jnp.exp(jax.random.uniform(ks[13], (DEPTH, H_B), f32, math.log(1e-3), math.log(1e-1)))
    return {
        'x_prompt': nrm(ks[0], (BATCH, SEQ, D_MODEL), 1.0),
        'x_sample': nrm(ks[1], (DEC_BATCH, DEC_SEQ, D_MODEL), 1.0),
        'cache_k': nrm(ks[2], (DEPTH, DEC_BATCH, PAST_LEN, H_KV, DH_C), 1.0),
        'cache_v': nrm(ks[3], (DEPTH, DEC_BATCH, PAST_LEN, H_KV, DH_C), 1.0),
        'cache_kidx': nrm(ks[4], (DEPTH, DEC_BATCH, PAST_LEN, D_IDX), 1.0),
        'state_ret': nrm(ks[5], (DEPTH, DEC_BATCH, H_A, DK_A, DV_A), 0.5),
        'state_delta': nrm(ks[6], (DEPTH, DEC_BATCH, H_B, DK_B, DV_B), 0.1),
        'state_conv': nrm(ks[7], (DEPTH, DEC_BATCH, CONV_W - 1, CONV_CH), 1.0),
        'norm_mix': 1.0 + nrm(ks[8], (DEPTH, D_MODEL), 0.02),
        'w_in': nrm(ks[9], (DEPTH, D_MODEL, D_IN), D_MODEL ** -0.5),
        'ret_gn': 1.0 + nrm(ks[10], (DEPTH, W_A), 0.02),
        'conv_w': nrm(ks[11], (DEPTH, CONV_W, CONV_CH), CONV_W ** -0.5),
        'delta_a_log': jnp.log(jax.random.uniform(ks[12], (DEPTH, H_B), f32, 1.0, 16.0)),
        'delta_dt_bias': dt_init + jnp.log(-jnp.expm1(-dt_init)),
        'delta_norm': 1.0 + nrm(ks[14], (DEPTH, DV_B), 0.02),
        'rel_bias': nrm(ks[15], (NUM_BUCKETS, H_C), 0.5),
        'w_branch_a': nrm(ks[16], (DEPTH, W_A, D_MODEL), W_A ** -0.5),
        'w_branch_b': nrm(ks[17], (DEPTH, W_B, D_MODEL), W_B ** -0.5),
        'w_branch_c': nrm(ks[18], (DEPTH, W_C, D_MODEL), W_C ** -0.5),
        'w_out': nrm(ks[19], (DEPTH, D_MODEL, D_MODEL), D_MODEL ** -0.5),
        'norm_ffn': 1.0 + nrm(ks[20], (DEPTH, D_MODEL), 0.02),
        'w_ffn_gate': nrm(ks[21], (DEPTH, D_MODEL, D_FF), D_MODEL ** -0.5),
        'w_ffn_up': nrm(ks[22], (DEPTH, D_MODEL, D_FF), D_MODEL ** -0.5),
        'w_ffn_down': nrm(ks[23], (DEPTH, D_FF, D_MODEL), D_FF ** -0.5),
        'norm_final': 1.0 + nrm(ks[24], (D_MODEL,), 0.02),
    }


def reference(x_prompt, x_sample, cache_k, cache_v, cache_kidx, state_ret, state_delta, state_conv,
              norm_mix, w_in, ret_gn, conv_w, delta_a_log, delta_dt_bias, delta_norm, rel_bias,
              w_branch_a, w_branch_b, w_branch_c, w_out, norm_ffn, w_ffn_gate, w_ffn_up, w_ffn_down,
              norm_final):
    dt = x_prompt.dtype
    bp = x_prompt.shape[0]
    zk = jnp.zeros((bp, 0, H_KV, DH_C), dt)
    zkidx = jnp.zeros((bp, 0, D_IDX), dt)
    zret = jnp.zeros((bp, H_A, DK_A, DV_A), dt)
    zdelta = jnp.zeros((bp, H_B, DK_B, DV_B), dt)
    zconv = jnp.zeros((bp, CONV_W - 1, CONV_CH), dt)

    hp, hs = x_prompt, x_sample
    new_p, new_s = [], []
    for l in range(DEPTH):
        wl = (norm_mix[l], w_in[l], ret_gn[l], conv_w[l], delta_a_log[l], delta_dt_bias[l], delta_norm[l],
              rel_bias, w_branch_a[l], w_branch_b[l], w_branch_c[l], w_out[l], norm_ffn[l],
              w_ffn_gate[l], w_ffn_up[l], w_ffn_down[l])
        hp, sp = trunk_layer(hp, zk, zk, zkidx, zret, zdelta, zconv, *wl)
        hs, ss = trunk_layer(hs, cache_k[l], cache_v[l], cache_kidx[l], state_ret[l], state_delta[l],
                             state_conv[l], *wl)
        new_p.append(sp)
        new_s.append(ss)

    y_prompt = rms_norm(hp, norm_final)
    y_sample = rms_norm(hs, norm_final)
    k_p, v_p, kidx_p, ret_p, delta_p, conv_p = [jnp.stack([s[i] for s in new_p]) for i in range(6)]
    k_s, v_s, kidx_s, ret_s, delta_s, conv_s = [jnp.stack([s[i] for s in new_s]) for i in range(6)]
    return (y_prompt, y_sample, k_p, v_p, kidx_p, ret_p, delta_p, conv_p,
            k_s, v_s, kidx_s, ret_s, delta_s, conv_s)
```

```python
import math
from functools import partial

import jax
import jax.numpy as jnp
import numpy as np
from jax import lax
from jax.experimental import pallas as pl
from jax.experimental.pallas import tpu as pltpu

F32 = jnp.float32
BF16 = jnp.bfloat16

D_MODEL = 4096
CHUNK = 64
QUERY_BLOCK = 128
EPS = 1e-6
H_A, DK_A, DV_A = 8, 128, 256
W_A = H_A * DV_A
ROPE_BASE = 10000.0
H_B, DK_B, DV_B = 16, 128, 128
W_B = H_B * DV_B
CONV_W = 4
CONV_CH = 2 * H_B * DK_B + H_B * DV_B
H_C, H_KV, DH_C = 16, 4, 128
W_C = H_C * DH_C
H_IDX, D_IDX = 32, 64
TOPK_MAX = 256
NUM_BUCKETS = 32
MAX_DISTANCE = 1024
D_FF = 11008

LANE = 128
VMEM_LIMIT_V7X = 56 * 1024 * 1024

COL = {}
_off = 0
for _name, _w in [("q_a", 1024), ("k_a", 1024), ("v_a", 2048), ("g_a", 2048), ("qkv_b", 6144), ("z_b", 2048),
                  ("q_c", 2048), ("k_c", 512), ("v_c", 512), ("q_i", 2048),
                  ("gate_a", 4096), ("gate_b", 4096), ("gate_c", 4096),
                  ("k_i", 64), ("w_i", 32), ("a_b", 16), ("b_b", 16)]:
    COL[_name] = (_off, _w)
    _off += _w
D_IN = _off
D_IN_PAD = 32768
D_FF_PAD = 11264
TM = 1024


def _cparams(sem):
    return pltpu.CompilerParams(dimension_semantics=sem, vmem_limit_bytes=VMEM_LIMIT_V7X)


def _rmsnorm_kernel(x_ref, g_ref, o_ref):
    x = x_ref[...]
    y = x * lax.rsqrt(jnp.mean(x * x, axis=-1, keepdims=True) + EPS)
    o_ref[...] = (y * g_ref[...]).astype(o_ref.dtype)


def rmsnorm(x, g, out_dtype, tr=512):
    m, d = x.shape
    return pl.pallas_call(
        _rmsnorm_kernel,
        grid=(m // tr,),
        in_specs=[pl.BlockSpec((tr, d), lambda i: (i, 0)), pl.BlockSpec((1, d), lambda i: (0, 0))],
        out_specs=pl.BlockSpec((tr, d), lambda i: (i, 0)),
        out_shape=jax.ShapeDtypeStruct((m, d), out_dtype),
        compiler_params=_cparams(("parallel",)),
        name="rmsnorm",
    )(x, g.reshape(1, d))


def _mm_kernel(a_ref, b_ref, o_ref):
    o_ref[...] = jnp.dot(a_ref[...], b_ref[...], preferred_element_type=F32).astype(o_ref.dtype)


def matmul(a, b, out_dtype, tn=1024):
    m, k = a.shape
    n = b.shape[1]
    return pl.pallas_call(
        _mm_kernel,
        grid=(m // TM, n // tn),
        in_specs=[pl.BlockSpec((TM, k), lambda i, j: (i, 0)), pl.BlockSpec((k, tn), lambda i, j: (0, j))],
        out_specs=pl.BlockSpec((TM, tn), lambda i, j: (i, j)),
        out_shape=jax.ShapeDtypeStruct((m, n), out_dtype),
        compiler_params=_cparams(("parallel", "parallel")),
        name="matmul_in",
    )(a, b)


def _merge_kernel(o_ref, w_ref, g_ref, out_ref, acc_ref):
    br = pl.program_id(2)
    y = jnp.dot(o_ref[0], w_ref[0], preferred_element_type=F32) * jax.nn.sigmoid(g_ref[...])

    @pl.when(br == 0)
    def _():
        acc_ref[...] = y

    @pl.when(br != 0)
    def _():
        acc_ref[...] += y

    @pl.when(br == pl.num_programs(2) - 1)
    def _():
        out_ref[...] = acc_ref[...].astype(out_ref.dtype)


def merge_branches(o_abc, w_br, proj, tn=1024):
    _, m, k = o_abc.shape
    n = w_br.shape[2]
    gate0 = COL["gate_a"][0] // tn
    nj = n // tn
    return pl.pallas_call(
        _merge_kernel,
        grid=(m // TM, nj, 3),
        in_specs=[pl.BlockSpec((1, TM, k), lambda i, j, r: (r, i, 0)),
                  pl.BlockSpec((1, k, tn), lambda i, j, r: (r, 0, j)),
                  pl.BlockSpec((TM, tn), lambda i, j, r: (i, gate0 + r * nj + j))],
        out_specs=pl.BlockSpec((TM, tn), lambda i, j, r: (i, j)),
        out_shape=jax.ShapeDtypeStruct((m, n), BF16),
        scratch_shapes=[pltpu.VMEM((TM, tn), F32)],
        compiler_params=_cparams(("parallel", "parallel", "arbitrary")),
        name="merge_branches",
    )(o_abc, w_br, proj)


def _mm_res_kernel(a_ref, b_ref, h_ref, o_ref):
    o_ref[...] = h_ref[...] + jnp.dot(a_ref[...], b_ref[...], preferred_element_type=F32)


def matmul_residual(a, b, h, tn=1024):
    m, k = a.shape
    n = b.shape[1]
    return pl.pallas_call(
        _mm_res_kernel,
        grid=(m // TM, n // tn),
        in_specs=[pl.BlockSpec((TM, k), lambda i, j: (i, 0)), pl.BlockSpec((k, tn), lambda i, j: (0, j)),
                  pl.BlockSpec((TM, tn), lambda i, j: (i, j))],
        out_specs=pl.BlockSpec((TM, tn), lambda i, j: (i, j)),
        out_shape=jax.ShapeDtypeStruct((m, n), F32),
        compiler_params=_cparams(("parallel", "parallel")),
        name="matmul_out",
    )(a, b, h)


def _gateup_kernel(u_ref, wg_ref, wu_ref, o_ref):
    u = u_ref[...]
    g = jnp.dot(u, wg_ref[...], preferred_element_type=F32)
    up = jnp.dot(u, wu_ref[...], preferred_element_type=F32)
    o_ref[...] = (jax.nn.silu(g) * up).astype(o_ref.dtype)


def ffn_gate_up(u, wg, wu, tn=512):
    m, k = u.shape
    n = wg.shape[1]
    return pl.pallas_call(
        _gateup_kernel,
        grid=(m // TM, n // tn),
        in_specs=[pl.BlockSpec((TM, k), lambda i, j: (i, 0)), pl.BlockSpec((k, tn), lambda i, j: (0, j)),
                  pl.BlockSpec((k, tn), lambda i, j: (0, j))],
        out_specs=pl.BlockSpec((TM, tn), lambda i, j: (i, j)),
        out_shape=jax.ShapeDtypeStruct((m, n), BF16),
        compiler_params=_cparams(("parallel", "parallel")),
        name="ffn_gate_up",
    )(u, wg, wu)


def _down_kernel(a_ref, b_ref, h_ref, o_ref, acc_ref):
    kk = pl.program_id(2)
    y = jnp.dot(a_ref[...], b_ref[...], preferred_element_type=F32)

    @pl.when(kk == 0)
    def _():
        acc_ref[...] = h_ref[...] + y

    @pl.when(kk != 0)
    def _():
        acc_ref[...] += y

    @pl.when(kk == pl.num_programs(2) - 1)
    def _():
        o_ref[...] = acc_ref[...]


def ffn_down(a, b, h, tn=1024, tk=2816):
    m, k = a.shape
    n = b.shape[1]
    return pl.pallas_call(
        _down_kernel,
        grid=(m // TM, n // tn, k // tk),
        in_specs=[pl.BlockSpec((TM, tk), lambda i, j, kk: (i, kk)), pl.BlockSpec((tk, tn), lambda i, j, kk: (kk, j)),
                  pl.BlockSpec((TM, tn), lambda i, j, kk: (i, j))],
        out_specs=pl.BlockSpec((TM, tn), lambda i, j, kk: (i, j)),
        out_shape=jax.ShapeDtypeStruct((m, n), F32),
        scratch_shapes=[pltpu.VMEM((TM, tn), F32)],
        compiler_params=_cparams(("parallel", "parallel", "arbitrary")),
        name="ffn_down",
    )(a, b, h)


def l2norm(x):
    return x * lax.rsqrt(jnp.sum(x * x, axis=-1, keepdims=True) + EPS)


def rotary(x, pos):
    half = x.shape[-1] // 2
    inv = ROPE_BASE ** (-jnp.arange(half, dtype=jnp.float32) / half)
    ang = pos.astype(jnp.float32)[:, None] * inv[None, :]
    cos = jnp.cos(ang)[None, :, None, :]
    sin = jnp.sin(ang)[None, :, None, :]
    xf = x.astype(jnp.float32)
    x1, x2 = xf[..., :half], xf[..., half:]
    return jnp.concatenate([x1 * cos - x2 * sin, x2 * cos + x1 * sin], axis=-1)


def retention_chunked(q, k, v, s0, log_gamma, chunk):
    b, t, h, _ = q.shape
    n = t // chunk

    def to_chunks(a):
        return jnp.moveaxis(a.reshape(b, n, chunk, *a.shape[2:]), 1, 0)

    i = jnp.arange(chunk, dtype=jnp.float32)
    rel = i[:, None] - i[None, :]
    inner_decay = jnp.exp(jnp.where(rel[None] >= 0, rel[None] * log_gamma[:, None, None], -jnp.inf))
    q_decay = jnp.exp((i[:, None] + 1.0) * log_gamma[None, :])
    k_decay = jnp.exp((chunk - 1.0 - i[:, None]) * log_gamma[None, :])
    chunk_decay = jnp.exp(chunk * log_gamma)

    def step(s, xs):
        qc, kc, vc = xs
        att = jnp.einsum('bihd,bjhd->bhij', qc, kc) * inner_decay
        inner = jnp.einsum('bhij,bjhe->bihe', att, vc)
        cross = jnp.einsum('bihd,bhde->bihe', qc * q_decay[None, :, :, None], s)
        s_new = s * chunk_decay[None, :, None, None] + jnp.einsum(
            'bjhd,bjhe->bhde', kc * k_decay[None, :, :, None], vc)
        return s_new, inner + cross

    s_fin, out = lax.scan(step, s0, (to_chunks(q), to_chunks(k), to_chunks(v)))
    return jnp.moveaxis(out, 0, 1).reshape(b, t, h, -1), s_fin


def gated_delta_chunked(q, k, v, g, beta, s0, chunk):
    b, t, h, _ = q.shape
    n = t // chunk

    def to_chunks(a):
        a = a.reshape(b, n, chunk, h, *a.shape[3:])
        return jnp.moveaxis(jnp.moveaxis(a, 1, 0), 3, 2)

    incl = jnp.tril(jnp.ones((chunk, chunk), dtype=bool))
    strict = jnp.tril(jnp.ones((chunk, chunk), dtype=bool), -1)
    eye = jnp.eye(chunk, dtype=jnp.float32)

    def step(s, xs):
        qc, kc, vc, gc, bc = xs
        gcum = jnp.cumsum(gc, axis=-1)
        gam = jnp.exp(jnp.where(incl, gcum[..., :, None] - gcum[..., None, :], -jnp.inf))
        kb = kc * bc[..., None]
        a = jnp.where(strict, jnp.einsum('bhid,bhjd->bhij', kb, kc) * gam, 0.0)
        tinv = lax.linalg.triangular_solve(eye + a, jnp.broadcast_to(eye, a.shape), left_side=True, lower=True)
        u = tinv @ (vc * bc[..., None])
        w = tinv @ (kb * jnp.exp(gcum)[..., None])
        v_new = u - w @ s
        qk = jnp.einsum('bhid,bhjd->bhij', qc, kc) * gam
        o = (qc * jnp.exp(gcum)[..., None]) @ s + qk @ v_new
        g_last = gcum[..., -1:]
        s_new = s * jnp.exp(g_last)[..., None] + jnp.einsum(
            'bhjd,bhje->bhde', kc * jnp.exp(g_last - gcum)[..., None], v_new)
        return s_new, o

    s_fin, out = lax.scan(step, s0, (to_chunks(q), to_chunks(k), to_chunks(v), to_chunks(g), to_chunks(beta)))
    out = jnp.moveaxis(jnp.moveaxis(out, 2, 3), 0, 1).reshape(b, t, h, -1)
    return out, s_fin


def t5_bucket(rel):
    half = NUM_BUCKETS // 2
    exact = half // 2
    n = jnp.abs(rel)
    large = exact + (jnp.log(jnp.maximum(n, 1).astype(jnp.float32) / exact)
                     / math.log(MAX_DISTANCE / exact) * (half - exact)).astype(jnp.int32)
    large = jnp.minimum(large, half - 1)
    return jnp.where(rel > 0, half, 0) + jnp.where(n < exact, n, large)


def dsa_block(q, q_idx, w_idx, qpos, k_all, v_all, k_idx_all, rel_bias, topk):
    b, nq = q.shape[:2]
    kpos = jnp.arange(k_all.shape[1])
    score = jnp.einsum('bqhd,bsd->bqhs', q_idx, k_idx_all)
    index_score = jnp.einsum('bqh,bqhs->bqs', w_idx, jax.nn.relu(score)).astype(jnp.float32)
    visible = (kpos[None, :] // CHUNK) <= (qpos[:, None] // CHUNK)
    index_score = jnp.where(visible[None], index_score, -jnp.inf)
    top_val, sel = lax.top_k(index_score, topk)
    valid = jnp.isfinite(top_val)
    gather = jax.vmap(lambda rows, idx: rows[idx])
    k_sel = gather(k_all, sel)
    v_sel = gather(v_all, sel)
    logits = jnp.einsum('bqhgd,bqnhd->bqhgn', q, k_sel).astype(jnp.float32)
    bias = rel_bias[t5_bucket(sel - qpos[None, :, None])]
    bias = jnp.transpose(bias.reshape(b, nq, topk, H_KV, H_C // H_KV), (0, 1, 3, 4, 2))
    logits = jnp.where(valid[:, :, None, None, :], logits + bias.astype(jnp.float32), -jnp.inf)
    p = jax.nn.softmax(logits, axis=-1).astype(v_all.dtype)
    return jnp.einsum('bqhgn,bqnhd->bqhgd', p, v_sel)


def dsa_attention(q, q_idx, w_idx, qpos, k_all, v_all, k_idx_all, rel_bias, topk):
    b, t = q.shape[:2]
    qb = QUERY_BLOCK if t % QUERY_BLOCK == 0 else t
    nb = t // qb

    def blocks(a):
        return jnp.moveaxis(a.reshape(b, nb, qb, *a.shape[2:]), 1, 0)

    out = lax.map(lambda xs: dsa_block(xs[0], xs[1], xs[2], xs[3], k_all, v_all, k_idx_all, rel_bias, topk),
                  (blocks(q), blocks(q_idx), blocks(w_idx), qpos.reshape(nb, qb)))
    return jnp.moveaxis(out, 0, 1).reshape(b, t, W_C)


def _col(proj, name):
    o, w = COL[name]
    return proj[..., o:o + w]


def branches(proj, past_k, past_v, past_kidx, s_ret, s_delta, s_conv,
             ret_gn, conv_w, a_log, dt_bias, d_norm, rel_bias):
    f32 = F32
    b, t, _ = proj.shape
    p_len = past_k.shape[1]
    pos = p_len + jnp.arange(t)
    chunk = CHUNK if t % CHUNK == 0 else t
    topk = min(TOPK_MAX, (p_len + t) // 4)

    q_a, k_a, v_a, g_a = (_col(proj, n) for n in ("q_a", "k_a", "v_a", "g_a"))
    qkv_b, z_b, a_b, b_b = (_col(proj, n) for n in ("qkv_b", "z_b", "a_b", "b_b"))
    q_c, k_c, v_c, q_i, k_i, w_i = (_col(proj, n) for n in ("q_c", "k_c", "v_c", "q_i", "k_i", "w_i"))

    qa = rotary(q_a.reshape(b, t, H_A, DK_A), pos)
    ka = rotary(k_a.reshape(b, t, H_A, DK_A), pos) * DK_A ** -0.5
    va = v_a.reshape(b, t, H_A, DV_A)
    log_gamma = jnp.log(1.0 - 2.0 ** (-5.0 - jnp.arange(H_A, dtype=f32)))
    o_a, ret_new = retention_chunked(qa, ka, va, s_ret, log_gamma, chunk)
    mu = jnp.mean(o_a, axis=-1, keepdims=True)
    var = jnp.mean(jnp.square(o_a - mu), axis=-1, keepdims=True)
    o_a = ((o_a - mu) * lax.rsqrt(var + EPS)).reshape(b, t, W_A) * ret_gn
    o_a = jax.nn.silu(g_a) * o_a

    conv_in = jnp.concatenate([s_conv, qkv_b], axis=1)
    conv_new = conv_in[:, -(CONV_W - 1):]
    qkv = jax.nn.silu(lax.conv_general_dilated(
        conv_in, conv_w[:, None, :], window_strides=(1,), padding='VALID',
        dimension_numbers=('NWC', 'WIO', 'NWC'), feature_group_count=CONV_CH))
    q_b, k_b, v_b = jnp.split(qkv, [H_B * DK_B, 2 * H_B * DK_B], axis=-1)
    qb_ = l2norm(q_b.reshape(b, t, H_B, DK_B)) * DK_B ** -0.5
    kb_ = l2norm(k_b.reshape(b, t, H_B, DK_B))
    vb_ = v_b.reshape(b, t, H_B, DV_B)
    beta = jax.nn.sigmoid(b_b)
    g_log = -jnp.exp(a_log) * jax.nn.softplus(a_b + dt_bias)
    o_b, delta_new = gated_delta_chunked(qb_, kb_, vb_, g_log, beta, s_delta, chunk)
    o_b = o_b * lax.rsqrt(jnp.mean(o_b * o_b, axis=-1, keepdims=True) + EPS) * d_norm
    o_b = o_b.reshape(b, t, W_B) * jax.nn.silu(z_b)

    qc = q_c.reshape(b, t, H_KV, H_C // H_KV, DH_C) * DH_C ** -0.5
    kc = k_c.reshape(b, t, H_KV, DH_C)
    vc = v_c.reshape(b, t, H_KV, DH_C)
    qi = q_i.reshape(b, t, H_IDX, D_IDX) * D_IDX ** -0.5
    wi = w_i * H_IDX ** -0.5
    k_all = jnp.concatenate([past_k, kc], axis=1)
    v_all = jnp.concatenate([past_v, vc], axis=1)
    kidx_all = jnp.concatenate([past_kidx, k_i], axis=1)
    o_c = dsa_attention(qc, qi, wi, pos, k_all, v_all, kidx_all, rel_bias, topk)

    o_abc = jnp.stack([o_a.reshape(b * t, W_A), o_b.reshape(b * t, W_B), o_c.reshape(b * t, W_C)]).astype(BF16)
    return o_abc, (kc, vc, k_i, ret_new, delta_new, conv_new)


def _prep_w_in(w):
    return jnp.concatenate([w[:, 0:14336], w[:, 14368:19488], w[:, 19584:31872], w[:, 19488:19584],
                            w[:, 14336:14368], jnp.zeros((w.shape[0], D_IN_PAD - D_IN), w.dtype)],
                           axis=1).astype(BF16)


def kernel(x_prompt, x_sample, cache_k, cache_v, cache_kidx, state_ret, state_delta, state_conv, norm_mix, w_in, ret_gn, conv_w, delta_a_log, delta_dt_bias, delta_norm, rel_bias, w_branch_a, w_branch_b, w_branch_c, w_out, norm_ffn, w_ffn_gate, w_ffn_up, w_ffn_down, norm_final):
    dt = x_prompt.dtype
    bp, tp, _ = x_prompt.shape
    bs, ts, _ = x_sample.shape
    depth = w_in.shape[0]
    n_p = bp * tp
    zk = jnp.zeros((bp, 0, H_KV, DH_C), dt)
    zkidx = jnp.zeros((bp, 0, D_IDX), dt)
    zret = jnp.zeros((bp, H_A, DK_A, DV_A), dt)
    zdelta = jnp.zeros((bp, H_B, DK_B, DV_B), dt)
    zconv = jnp.zeros((bp, CONV_W - 1, CONV_CH), dt)

    h = jnp.concatenate([x_prompt.reshape(n_p, D_MODEL), x_sample.reshape(bs * ts, D_MODEL)], axis=0)
    new_p, new_s = [], []
    for l in range(depth):
        u = rmsnorm(h, norm_mix[l], BF16)
        proj = matmul(u, _prep_w_in(w_in[l]), F32)
        bw = (ret_gn[l], conv_w[l], delta_a_log[l], delta_dt_bias[l], delta_norm[l], rel_bias)
        o_p, sp = branches(proj[:n_p].reshape(bp, tp, D_IN_PAD), zk, zk, zkidx, zret, zdelta, zconv, *bw)
        o_s, ss = branches(proj[n_p:].reshape(bs, ts, D_IN_PAD), cache_k[l], cache_v[l], cache_kidx[l],
                           state_ret[l], state_delta[l], state_conv[l], *bw)
        new_p.append(sp)
        new_s.append(ss)
        o_abc = jnp.concatenate([o_p, o_s], axis=1)
        w_br = jnp.stack([w_branch_a[l], w_branch_b[l], w_branch_c[l]]).astype(BF16)
        merged = merge_branches(o_abc, w_br, proj)
        h = matmul_residual(merged, w_out[l].astype(BF16), h)
        u2 = rmsnorm(h, norm_ffn[l], BF16)
        pad = D_FF_PAD - D_FF
        wg = jnp.pad(w_ffn_gate[l], ((0, 0), (0, pad))).astype(BF16)
        wu = jnp.pad(w_ffn_up[l], ((0, 0), (0, pad))).astype(BF16)
        wd = jnp.pad(w_ffn_down[l], ((0, pad), (0, 0))).astype(BF16)
        act = ffn_gate_up(u2, wg, wu)
        h = ffn_down(act, wd, h)

    y = rmsnorm(h, norm_final, F32)
    y_prompt = y[:n_p].reshape(bp, tp, D_MODEL)
    y_sample = y[n_p:].reshape(bs, ts, D_MODEL)
    k_p, v_p, kidx_p, ret_p, delta_p, conv_p = [jnp.stack([s[i] for s in new_p]) for i in range(6)]
    k_s, v_s, kidx_s, ret_s, delta_s, conv_s = [jnp.stack([s[i] for s in new_s]) for i in range(6)]
    return (y_prompt, y_sample, k_p, v_p, kidx_p, ret_p, delta_p, conv_p,
            k_s, v_s, kidx_s, ret_s, delta_s, conv_s)
```

```python
import math
from functools import partial

import jax
import jax.numpy as jnp
import numpy as np
from jax import lax
from jax.experimental import pallas as pl
from jax.experimental.pallas import tpu as pltpu

F32 = jnp.float32
BF16 = jnp.bfloat16

D_MODEL = 4096
CHUNK = 64
QUERY_BLOCK = 128
EPS = 1e-6
H_A, DK_A, DV_A = 8, 128, 256
W_A = H_A * DV_A
ROPE_BASE = 10000.0
H_B, DK_B, DV_B = 16, 128, 128
W_B = H_B * DV_B
CONV_W = 4
CONV_CH = 2 * H_B * DK_B + H_B * DV_B
H_C, H_KV, DH_C = 16, 4, 128
W_C = H_C * DH_C
H_IDX, D_IDX = 32, 64
TOPK_MAX = 256
NUM_BUCKETS = 32
MAX_DISTANCE = 1024
D_FF = 11008

LANE = 128
VMEM_LIMIT_V7X = 56 * 1024 * 1024

COL = {}
_off = 0
for _name, _w in [("q_a", 1024), ("k_a", 1024), ("v_a", 2048), ("g_a", 2048), ("qkv_b", 6144), ("z_b", 2048),
                  ("q_c", 2048), ("q_i", 2048), ("k_c", 512), ("v_c", 512),
                  ("gate_a", 4096), ("gate_b", 4096), ("gate_c", 4096),
                  ("k_i", 64), ("w_i", 32), ("a_b", 16), ("b_b", 16)]:
    COL[_name] = (_off, _w)
    _off += _w
D_IN = _off
D_IN_PAD = 32768
D_FF_PAD = 11264
TM = 1024


def _cparams(sem):
    return pltpu.CompilerParams(dimension_semantics=sem, vmem_limit_bytes=VMEM_LIMIT_V7X)


def _rmsnorm_kernel(x_ref, g_ref, o_ref):
    x = x_ref[...]
    y = x * lax.rsqrt(jnp.mean(x * x, axis=-1, keepdims=True) + EPS)
    o_ref[...] = (y * g_ref[...]).astype(o_ref.dtype)


def rmsnorm(x, g, out_dtype, tr=512):
    m, d = x.shape
    return pl.pallas_call(
        _rmsnorm_kernel,
        grid=(m // tr,),
        in_specs=[pl.BlockSpec((tr, d), lambda i: (i, 0)), pl.BlockSpec((1, d), lambda i: (0, 0))],
        out_specs=pl.BlockSpec((tr, d), lambda i: (i, 0)),
        out_shape=jax.ShapeDtypeStruct((m, d), out_dtype),
        compiler_params=_cparams(("parallel",)),
        name="rmsnorm",
    )(x, g.reshape(1, d))


def _mm_kernel(a_ref, b_ref, o_ref):
    o_ref[...] = jnp.dot(a_ref[...], b_ref[...], preferred_element_type=F32).astype(o_ref.dtype)


def matmul(a, b, out_dtype, tn=1024):
    m, k = a.shape
    n = b.shape[1]
    return pl.pallas_call(
        _mm_kernel,
        grid=(m // TM, n // tn),
        in_specs=[pl.BlockSpec((TM, k), lambda i, j: (i, 0)), pl.BlockSpec((k, tn), lambda i, j: (0, j))],
        out_specs=pl.BlockSpec((TM, tn), lambda i, j: (i, j)),
        out_shape=jax.ShapeDtypeStruct((m, n), out_dtype),
        compiler_params=_cparams(("parallel", "parallel")),
        name="matmul_in",
    )(a, b)


def _merge_kernel(o_ref, w_ref, g_ref, out_ref, acc_ref):
    br = pl.program_id(2)
    y = jnp.dot(o_ref[0], w_ref[0], preferred_element_type=F32) * jax.nn.sigmoid(g_ref[...])

    @pl.when(br == 0)
    def _():
        acc_ref[...] = y

    @pl.when(br != 0)
    def _():
        acc_ref[...] += y

    @pl.when(br == pl.num_programs(2) - 1)
    def _():
        out_ref[...] = acc_ref[...].astype(out_ref.dtype)


def merge_branches(o_abc, w_br, proj, tn=1024):
    _, m, k = o_abc.shape
    n = w_br.shape[2]
    gate0 = COL["gate_a"][0] // tn
    nj = n // tn
    return pl.pallas_call(
        _merge_kernel,
        grid=(m // TM, nj, 3),
        in_specs=[pl.BlockSpec((1, TM, k), lambda i, j, r: (r, i, 0)),
                  pl.BlockSpec((1, k, tn), lambda i, j, r: (r, 0, j)),
                  pl.BlockSpec((TM, tn), lambda i, j, r: (i, gate0 + r * nj + j))],
        out_specs=pl.BlockSpec((TM, tn), lambda i, j, r: (i, j)),
        out_shape=jax.ShapeDtypeStruct((m, n), BF16),
        scratch_shapes=[pltpu.VMEM((TM, tn), F32)],
        compiler_params=_cparams(("parallel", "parallel", "arbitrary")),
        name="merge_branches",
    )(o_abc, w_br, proj)


def _mm_res_kernel(a_ref, b_ref, h_ref, o_ref):
    o_ref[...] = h_ref[...] + jnp.dot(a_ref[...], b_ref[...], preferred_element_type=F32)


def matmul_residual(a, b, h, tn=1024):
    m, k = a.shape
    n = b.shape[1]
    return pl.pallas_call(
        _mm_res_kernel,
        grid=(m // TM, n // tn),
        in_specs=[pl.BlockSpec((TM, k), lambda i, j: (i, 0)), pl.BlockSpec((k, tn), lambda i, j: (0, j)),
                  pl.BlockSpec((TM, tn), lambda i, j: (i, j))],
        out_specs=pl.BlockSpec((TM, tn), lambda i, j: (i, j)),
        out_shape=jax.ShapeDtypeStruct((m, n), F32),
        compiler_params=_cparams(("parallel", "parallel")),
        name="matmul_out",
    )(a, b, h)


def _gateup_kernel(u_ref, wg_ref, wu_ref, o_ref):
    u = u_ref[...]
    g = jnp.dot(u, wg_ref[...], preferred_element_type=F32)
    up = jnp.dot(u, wu_ref[...], preferred_element_type=F32)
    o_ref[...] = (jax.nn.silu(g) * up).astype(o_ref.dtype)


def ffn_gate_up(u, wg, wu, tn=512):
    m, k = u.shape
    n = wg.shape[1]
    return pl.pallas_call(
        _gateup_kernel,
        grid=(m // TM, n // tn),
        in_specs=[pl.BlockSpec((TM, k), lambda i, j: (i, 0)), pl.BlockSpec((k, tn), lambda i, j: (0, j)),
                  pl.BlockSpec((k, tn), lambda i, j: (0, j))],
        out_specs=pl.BlockSpec((TM, tn), lambda i, j: (i, j)),
        out_shape=jax.ShapeDtypeStruct((m, n), BF16),
        compiler_params=_cparams(("parallel", "parallel")),
        name="ffn_gate_up",
    )(u, wg, wu)


def _down_kernel(a_ref, b_ref, h_ref, o_ref, acc_ref):
    kk = pl.program_id(2)
    y = jnp.dot(a_ref[...], b_ref[...], preferred_element_type=F32)

    @pl.when(kk == 0)
    def _():
        acc_ref[...] = h_ref[...] + y

    @pl.when(kk != 0)
    def _():
        acc_ref[...] += y

    @pl.when(kk == pl.num_programs(2) - 1)
    def _():
        o_ref[...] = acc_ref[...]


def ffn_down(a, b, h, tn=1024, tk=2816):
    m, k = a.shape
    n = b.shape[1]
    return pl.pallas_call(
        _down_kernel,
        grid=(m // TM, n // tn, k // tk),
        in_specs=[pl.BlockSpec((TM, tk), lambda i, j, kk: (i, kk)), pl.BlockSpec((tk, tn), lambda i, j, kk: (kk, j)),
                  pl.BlockSpec((TM, tn), lambda i, j, kk: (i, j))],
        out_specs=pl.BlockSpec((TM, tn), lambda i, j, kk: (i, j)),
        out_shape=jax.ShapeDtypeStruct((m, n), F32),
        scratch_shapes=[pltpu.VMEM((TM, tn), F32)],
        compiler_params=_cparams(("parallel", "parallel", "arbitrary")),
        name="ffn_down",
    )(a, b, h)


def l2norm(x):
    return x * lax.rsqrt(jnp.sum(x * x, axis=-1, keepdims=True) + EPS)


def rotary(x, pos):
    half = x.shape[-1] // 2
    inv = ROPE_BASE ** (-jnp.arange(half, dtype=jnp.float32) / half)
    ang = pos.astype(jnp.float32)[:, None] * inv[None, :]
    cos = jnp.cos(ang)[None, :, None, :]
    sin = jnp.sin(ang)[None, :, None, :]
    xf = x.astype(jnp.float32)
    x1, x2 = xf[..., :half], xf[..., half:]
    return jnp.concatenate([x1 * cos - x2 * sin, x2 * cos + x1 * sin], axis=-1)


def retention_chunked(q, k, v, s0, log_gamma, chunk):
    b, t, h, _ = q.shape
    n = t // chunk

    def to_chunks(a):
        return jnp.moveaxis(a.reshape(b, n, chunk, *a.shape[2:]), 1, 0)

    i = jnp.arange(chunk, dtype=jnp.float32)
    rel = i[:, None] - i[None, :]
    inner_decay = jnp.exp(jnp.where(rel[None] >= 0, rel[None] * log_gamma[:, None, None], -jnp.inf))
    q_decay = jnp.exp((i[:, None] + 1.0) * log_gamma[None, :])
    k_decay = jnp.exp((chunk - 1.0 - i[:, None]) * log_gamma[None, :])
    chunk_decay = jnp.exp(chunk * log_gamma)

    def step(s, xs):
        qc, kc, vc = xs
        att = jnp.einsum('bihd,bjhd->bhij', qc, kc) * inner_decay
        inner = jnp.einsum('bhij,bjhe->bihe', att, vc)
        cross = jnp.einsum('bihd,bhde->bihe', qc * q_decay[None, :, :, None], s)
        s_new = s * chunk_decay[None, :, None, None] + jnp.einsum(
            'bjhd,bjhe->bhde', kc * k_decay[None, :, :, None], vc)
        return s_new, inner + cross

    s_fin, out = lax.scan(step, s0, (to_chunks(q), to_chunks(k), to_chunks(v)))
    return jnp.moveaxis(out, 0, 1).reshape(b, t, h, -1), s_fin


def gated_delta_chunked(q, k, v, g, beta, s0, chunk):
    b, t, h, _ = q.shape
    n = t // chunk

    def to_chunks(a):
        a = a.reshape(b, n, chunk, h, *a.shape[3:])
        return jnp.moveaxis(jnp.moveaxis(a, 1, 0), 3, 2)

    incl = jnp.tril(jnp.ones((chunk, chunk), dtype=bool))
    strict = jnp.tril(jnp.ones((chunk, chunk), dtype=bool), -1)
    eye = jnp.eye(chunk, dtype=jnp.float32)

    def step(s, xs):
        qc, kc, vc, gc, bc = xs
        gcum = jnp.cumsum(gc, axis=-1)
        gam = jnp.exp(jnp.where(incl, gcum[..., :, None] - gcum[..., None, :], -jnp.inf))
        kb = kc * bc[..., None]
        a = jnp.where(strict, jnp.einsum('bhid,bhjd->bhij', kb, kc) * gam, 0.0)
        tinv = lax.linalg.triangular_solve(eye + a, jnp.broadcast_to(eye, a.shape), left_side=True, lower=True)
        u = tinv @ (vc * bc[..., None])
        w = tinv @ (kb * jnp.exp(gcum)[..., None])
        v_new = u - w @ s
        qk = jnp.einsum('bhid,bhjd->bhij', qc, kc) * gam
        o = (qc * jnp.exp(gcum)[..., None]) @ s + qk @ v_new
        g_last = gcum[..., -1:]
        s_new = s * jnp.exp(g_last)[..., None] + jnp.einsum(
            'bhjd,bhje->bhde', kc * jnp.exp(g_last - gcum)[..., None], v_new)
        return s_new, o

    s_fin, out = lax.scan(step, s0, (to_chunks(q), to_chunks(k), to_chunks(v), to_chunks(g), to_chunks(beta)))
    out = jnp.moveaxis(jnp.moveaxis(out, 2, 3), 0, 1).reshape(b, t, h, -1)
    return out, s_fin


def t5_bucket(rel):
    half = NUM_BUCKETS // 2
    exact = half // 2
    n = jnp.abs(rel)
    large = exact + (jnp.log(jnp.maximum(n, 1).astype(jnp.float32) / exact)
                     / math.log(MAX_DISTANCE / exact) * (half - exact)).astype(jnp.int32)
    large = jnp.minimum(large, half - 1)
    return jnp.where(rel > 0, half, 0) + jnp.where(n < exact, n, large)


KEY_TILE = LANE
N_BIAS_NEAR = 6
MASKED = -1e30
INT32_MIN = -2 ** 31
GROUP = H_C // H_KV


def _dot_nt(a, b):
    return lax.dot_general(a, b, (((1,), (1,)), ((), ())), preferred_element_type=F32)


def _dsa_kernel(qc_ref, qi_ref, small_ref, k_ref, v_ref, kidx_ref, bias_ref, o_ref,
                key_ref, qis_ref, wb_ref, m_ref, l_ref, acc_ref, *, tq, q0_tile, seq_len, topk):
    qt = q0_tile + pl.program_id(1)
    n_vis = qt + 1
    row = lax.broadcasted_iota(jnp.int32, (tq, KEY_TILE), 0)
    lane = lax.broadcasted_iota(jnp.int32, (tq, KEY_TILE), 1)
    q_chunk = (qt * KEY_TILE + row) // CHUNK

    qi = (qi_ref[...] * D_IDX ** -0.5).astype(BF16)
    w = small_ref[:, D_IDX:D_IDX + H_IDX] * H_IDX ** -0.5
    for h in range(H_IDX):
        qis_ref[h * tq:(h + 1) * tq, :] = qi[:, h * D_IDX:(h + 1) * D_IDX]
        wb_ref[h * tq:(h + 1) * tq, :] = jnp.broadcast_to(w[:, h:h + 1], (tq, KEY_TILE))

    def score_tile(t, carry):
        start = pl.multiple_of(t * KEY_TILE, KEY_TILE)
        kt = kidx_ref[0, pl.ds(start, KEY_TILE), :]
        s = _dot_nt(qis_ref[...], kt)
        r = jnp.maximum(s, 0.0) * wb_ref[...]
        score = jnp.sum(r.reshape(H_IDX, tq, KEY_TILE), axis=0) + 0.0
        bits = pltpu.bitcast(score, jnp.int32)
        key = jnp.where(bits < 0, bits ^ jnp.int32(0x7FFFFFFF), bits)
        kpos = start + lane
        visible = ((kpos // CHUNK) <= q_chunk) & (kpos < seq_len)
        key_ref[t] = jnp.where(visible, key, jnp.int32(INT32_MIN))
        return carry

    lax.fori_loop(0, n_vis, score_tile, 0)

    thr = jnp.full((tq, KEY_TILE), INT32_MIN, jnp.int32)
    for bit in range(31, -1, -1):
        cand = thr + jnp.int32(-2 ** 31 if bit == 31 else 2 ** bit)

        def count_tile(t, cnt, cand=cand):
            return cnt + jnp.where(key_ref[t] >= cand, 1.0, 0.0)

        cnt = lax.fori_loop(0, n_vis, count_tile, jnp.zeros((tq, KEY_TILE), F32))
        total = jnp.sum(cnt, axis=-1, keepdims=True)
        thr = jnp.where(total >= float(topk), cand, thr)
    thr = jnp.maximum(thr, jnp.int32(INT32_MIN + 1))

    for kv in range(H_KV):
        qg = jnp.concatenate(
            [(qc_ref[:, (kv * GROUP + g) * DH_C:(kv * GROUP + g + 1) * DH_C] * DH_C ** -0.5).astype(BF16)
             for g in range(GROUP)], axis=0)
        m_ref[...] = jnp.full(m_ref.shape, MASKED, F32)
        l_ref[...] = jnp.zeros(l_ref.shape, F32)
        acc_ref[...] = jnp.zeros(acc_ref.shape, F32)

        def attend_tile(t, carry, kv=kv, qg=qg):
            start = pl.multiple_of(t * KEY_TILE, KEY_TILE)
            kt = k_ref[0, pl.ds(start, KEY_TILE), kv * DH_C:(kv + 1) * DH_C]
            vt = v_ref[0, pl.ds(start, KEY_TILE), kv * DH_C:(kv + 1) * DH_C]
            s = _dot_nt(qg, kt)
            sel = key_ref[t] >= thr
            dd = jnp.minimum(qt - t, N_BIAS_NEAR)
            parts = []
            for g in range(GROUP):
                bias = bias_ref[dd, kv * GROUP + g, 0:tq, :]
                parts.append(jnp.where(sel, s[g * tq:(g + 1) * tq] + bias, MASKED))
            sm = jnp.concatenate(parts, axis=0)
            m_old = m_ref[...]
            m_new = jnp.maximum(m_old, jnp.max(sm, axis=-1, keepdims=True))
            alpha = jnp.exp(m_old - m_new)
            p = jnp.exp(sm - m_new)
            l_ref[...] = alpha * l_ref[...] + jnp.sum(p, axis=-1, keepdims=True)
            acc_ref[...] = alpha * acc_ref[...] + jnp.dot(p.astype(BF16), vt, preferred_element_type=F32)
            m_ref[...] = m_new
            return carry

        lax.fori_loop(0, n_vis, attend_tile, 0)
        out = acc_ref[...] / l_ref[...]
        for g in range(GROUP):
            c0 = (kv * GROUP + g) * DH_C
            o_ref[:, c0:c0 + DH_C] = out[g * tq:(g + 1) * tq].astype(o_ref.dtype)


def dsa_bias_tiles(rel_bias):
    i = jnp.arange(KEY_TILE)
    d = jnp.arange(N_BIAS_NEAR + 1)
    rel = (i[None, None, :] - i[None, :, None]) - KEY_TILE * d[:, None, None]
    return jnp.transpose(rel_bias[t5_bucket(rel)], (0, 3, 1, 2)).astype(F32)


def dsa_attention(proj, row0, b, t, k_all, v_all, kidx_all, bias_tiles, p_len, seq_len, topk):
    tq = KEY_TILE if t % KEY_TILE == 0 else t
    assert p_len % KEY_TILE == 0 and KEY_TILE % tq == 0 and (tq == KEY_TILE or t == tq) and row0 % tq == 0
    nb = t // tq
    lp = k_all.shape[1]
    assert lp % KEY_TILE == 0 and lp >= p_len + nb * KEY_TILE
    assert 8 * (MAX_DISTANCE / 8) ** (7 / 8) < KEY_TILE * N_BIAS_NEAR - (KEY_TILE - 1)
    rb = row0 // tq
    kern = partial(_dsa_kernel, tq=tq, q0_tile=p_len // KEY_TILE, seq_len=seq_len, topk=topk)
    return pl.pallas_call(
        kern,
        grid=(b, nb),
        in_specs=[pl.BlockSpec((tq, W_C), lambda bi, j: (rb + bi * nb + j, COL["q_c"][0] // W_C)),
                  pl.BlockSpec((tq, H_IDX * D_IDX), lambda bi, j: (rb + bi * nb + j, COL["q_i"][0] // (H_IDX * D_IDX))),
                  pl.BlockSpec((tq, LANE), lambda bi, j: (rb + bi * nb + j, COL["k_i"][0] // LANE)),
                  pl.BlockSpec((1, lp, H_KV * DH_C), lambda bi, j: (bi, 0, 0)),
                  pl.BlockSpec((1, lp, H_KV * DH_C), lambda bi, j: (bi, 0, 0)),
                  pl.BlockSpec((1, lp, D_IDX), lambda bi, j: (bi, 0, 0)),
                  pl.BlockSpec((N_BIAS_NEAR + 1, H_C, KEY_TILE, KEY_TILE), lambda bi, j: (0, 0, 0, 0))],
        out_specs=pl.BlockSpec((tq, W_C), lambda bi, j: (bi * nb + j, 0)),
        out_shape=jax.ShapeDtypeStruct((b * t, W_C), BF16),
        scratch_shapes=[pltpu.VMEM((lp // KEY_TILE, tq, KEY_TILE), jnp.int32),
                        pltpu.VMEM((H_IDX * tq, D_IDX), BF16),
                        pltpu.VMEM((H_IDX * tq, KEY_TILE), F32),
                        pltpu.VMEM((GROUP * tq, 1), F32),
                        pltpu.VMEM((GROUP * tq, 1), F32),
                        pltpu.VMEM((GROUP * tq, DH_C), F32)],
        compiler_params=_cparams(("parallel", "arbitrary")),
        name="dsa_attention",
    )(proj, proj, proj, k_all, v_all, kidx_all, bias_tiles)


def _col(proj, name):
    o, w = COL[name]
    return proj[..., o:o + w]


def branches(proj_all, row0, b, t, past_k, past_v, past_kidx, s_ret, s_delta, s_conv,
             ret_gn, conv_w, a_log, dt_bias, d_norm, bias_tiles):
    f32 = F32
    proj = proj_all[row0:row0 + b * t].reshape(b, t, D_IN_PAD)
    p_len = past_k.shape[1]
    pos = p_len + jnp.arange(t)
    chunk = CHUNK if t % CHUNK == 0 else t
    topk = min(TOPK_MAX, (p_len + t) // 4)

    q_a, k_a, v_a, g_a = (_col(proj, n) for n in ("q_a", "k_a", "v_a", "g_a"))
    qkv_b, z_b, a_b, b_b = (_col(proj, n) for n in ("qkv_b", "z_b", "a_b", "b_b"))
    q_c, k_c, v_c, q_i, k_i, w_i = (_col(proj, n) for n in ("q_c", "k_c", "v_c", "q_i", "k_i", "w_i"))

    qa = rotary(q_a.reshape(b, t, H_A, DK_A), pos)
    ka = rotary(k_a.reshape(b, t, H_A, DK_A), pos) * DK_A ** -0.5
    va = v_a.reshape(b, t, H_A, DV_A)
    log_gamma = jnp.log(1.0 - 2.0 ** (-5.0 - jnp.arange(H_A, dtype=f32)))
    o_a, ret_new = retention_chunked(qa, ka, va, s_ret, log_gamma, chunk)
    mu = jnp.mean(o_a, axis=-1, keepdims=True)
    var = jnp.mean(jnp.square(o_a - mu), axis=-1, keepdims=True)
    o_a = ((o_a - mu) * lax.rsqrt(var + EPS)).reshape(b, t, W_A) * ret_gn
    o_a = jax.nn.silu(g_a) * o_a

    conv_in = jnp.concatenate([s_conv, qkv_b], axis=1)
    conv_new = conv_in[:, -(CONV_W - 1):]
    qkv = jax.nn.silu(lax.conv_general_dilated(
        conv_in, conv_w[:, None, :], window_strides=(1,), padding='VALID',
        dimension_numbers=('NWC', 'WIO', 'NWC'), feature_group_count=CONV_CH))
    q_b, k_b, v_b = jnp.split(qkv, [H_B * DK_B, 2 * H_B * DK_B], axis=-1)
    qb_ = l2norm(q_b.reshape(b, t, H_B, DK_B)) * DK_B ** -0.5
    kb_ = l2norm(k_b.reshape(b, t, H_B, DK_B))
    vb_ = v_b.reshape(b, t, H_B, DV_B)
    beta = jax.nn.sigmoid(b_b)
    g_log = -jnp.exp(a_log) * jax.nn.softplus(a_b + dt_bias)
    o_b, delta_new = gated_delta_chunked(qb_, kb_, vb_, g_log, beta, s_delta, chunk)
    o_b = o_b * lax.rsqrt(jnp.mean(o_b * o_b, axis=-1, keepdims=True) + EPS) * d_norm
    o_b = o_b.reshape(b, t, W_B) * jax.nn.silu(z_b)

    kc = k_c.reshape(b, t, H_KV, DH_C)
    vc = v_c.reshape(b, t, H_KV, DH_C)
    seq_len = p_len + t
    lp = -(-seq_len // KEY_TILE) * KEY_TILE

    def all_keys(past, new):
        w = new.shape[-1]
        return jnp.concatenate([past.reshape(b, p_len, w).astype(BF16), new.astype(BF16),
                                jnp.zeros((b, lp - seq_len, w), BF16)], axis=1)

    o_c = dsa_attention(proj_all, row0, b, t, all_keys(past_k, k_c), all_keys(past_v, v_c),
                        all_keys(past_kidx, k_i), bias_tiles, p_len, seq_len, topk)

    o_abc = jnp.stack([o_a.reshape(b * t, W_A).astype(BF16), o_b.reshape(b * t, W_B).astype(BF16), o_c])
    return o_abc, (kc, vc, k_i, ret_new, delta_new, conv_new)


def _prep_w_in(w):
    return jnp.concatenate([w[:, 0:14336], w[:, 14368:16416], w[:, 17440:19488], w[:, 16416:17440],
                            w[:, 19584:31872], w[:, 19488:19584], w[:, 14336:14368],
                            jnp.zeros((w.shape[0], D_IN_PAD - D_IN), w.dtype)], axis=1).astype(BF16)


def kernel(x_prompt, x_sample, cache_k, cache_v, cache_kidx, state_ret, state_delta, state_conv, norm_mix, w_in, ret_gn, conv_w, delta_a_log, delta_dt_bias, delta_norm, rel_bias, w_branch_a, w_branch_b, w_branch_c, w_out, norm_ffn, w_ffn_gate, w_ffn_up, w_ffn_down, norm_final):
    dt = x_prompt.dtype
    bp, tp, _ = x_prompt.shape
    bs, ts, _ = x_sample.shape
    depth = w_in.shape[0]
    n_p = bp * tp
    zk = jnp.zeros((bp, 0, H_KV, DH_C), dt)
    zkidx = jnp.zeros((bp, 0, D_IDX), dt)
    zret = jnp.zeros((bp, H_A, DK_A, DV_A), dt)
    zdelta = jnp.zeros((bp, H_B, DK_B, DV_B), dt)
    zconv = jnp.zeros((bp, CONV_W - 1, CONV_CH), dt)

    h = jnp.concatenate([x_prompt.reshape(n_p, D_MODEL), x_sample.reshape(bs * ts, D_MODEL)], axis=0)
    new_p, new_s = [], []
    bias_tiles = dsa_bias_tiles(rel_bias)
    for l in range(depth):
        u = rmsnorm(h, norm_mix[l], BF16)
        proj = matmul(u, _prep_w_in(w_in[l]), F32)
        bw = (ret_gn[l], conv_w[l], delta_a_log[l], delta_dt_bias[l], delta_norm[l], bias_tiles)
        o_p, sp = branches(proj, 0, bp, tp, zk, zk, zkidx, zret, zdelta, zconv, *bw)
        o_s, ss = branches(proj, n_p, bs, ts, cache_k[l], cache_v[l], cache_kidx[l],
                           state_ret[l], state_delta[l], state_conv[l], *bw)
        new_p.append(sp)
        new_s.append(ss)
        o_abc = jnp.concatenate([o_p, o_s], axis=1)
        w_br = jnp.stack([w_branch_a[l], w_branch_b[l], w_branch_c[l]]).astype(BF16)
        merged = merge_branches(o_abc, w_br, proj)
        h = matmul_residual(merged, w_out[l].astype(BF16), h)
        u2 = rmsnorm(h, norm_ffn[l], BF16)
        pad = D_FF_PAD - D_FF
        wg = jnp.pad(w_ffn_gate[l], ((0, 0), (0, pad))).astype(BF16)
        wu = jnp.pad(w_ffn_up[l], ((0, 0), (0, pad))).astype(BF16)
        wd = jnp.pad(w_ffn_down[l], ((0, pad), (0, 0))).astype(BF16)
        act = ffn_gate_up(u2, wg, wu)
        h = ffn_down(act, wd, h)

    y = rmsnorm(h, norm_final, F32)
    y_prompt = y[:n_p].reshape(bp, tp, D_MODEL)
    y_sample = y[n_p:].reshape(bs, ts, D_MODEL)
    k_p, v_p, kidx_p, ret_p, delta_p, conv_p = [jnp.stack([s[i] for s in new_p]) for i in range(6)]
    k_s, v_s, kidx_s, ret_s, delta_s, conv_s = [jnp.stack([s[i] for s in new_s]) for i in range(6)]
    return (y_prompt, y_sample, k_p, v_p, kidx_p, ret_p, delta_p, conv_p,
            k_s, v_s, kidx_s, ret_s, delta_s, conv_s)
```

```python
import math
from functools import partial

import jax
import jax.numpy as jnp
import numpy as np
from jax import lax
from jax.experimental import pallas as pl
from jax.experimental.pallas import tpu as pltpu

F32 = jnp.float32
BF16 = jnp.bfloat16

D_MODEL = 4096
CHUNK = 64
QUERY_BLOCK = 128
EPS = 1e-6
H_A, DK_A, DV_A = 8, 128, 256
W_A = H_A * DV_A
ROPE_BASE = 10000.0
H_B, DK_B, DV_B = 16, 128, 128
W_B = H_B * DV_B
CONV_W = 4
CONV_CH = 2 * H_B * DK_B + H_B * DV_B
H_C, H_KV, DH_C = 16, 4, 128
W_C = H_C * DH_C
H_IDX, D_IDX = 32, 64
TOPK_MAX = 256
NUM_BUCKETS = 32
MAX_DISTANCE = 1024
D_FF = 11008

LANE = 128
VMEM_LIMIT_V7X = 56 * 1024 * 1024

COL = {}
_off = 0
for _name, _w in [("q_a", 1024), ("k_a", 1024), ("v_a", 2048), ("g_a", 2048), ("qkv_b", 6144), ("z_b", 2048),
                  ("q_c", 2048), ("q_i", 2048), ("k_c", 512), ("v_c", 512),
                  ("gate_a", 4096), ("gate_b", 4096), ("gate_c", 4096),
                  ("k_i", 64), ("w_i", 32), ("a_b", 16), ("b_b", 16)]:
    COL[_name] = (_off, _w)
    _off += _w
D_IN = _off
D_IN_PAD = 32768
D_FF_PAD = 11264
TM = 1024


def _cparams(sem):
    return pltpu.CompilerParams(dimension_semantics=sem, vmem_limit_bytes=VMEM_LIMIT_V7X)


def _rmsnorm_kernel(x_ref, g_ref, o_ref):
    x = x_ref[...]
    y = x * lax.rsqrt(jnp.mean(x * x, axis=-1, keepdims=True) + EPS)
    o_ref[...] = (y * g_ref[...]).astype(o_ref.dtype)


def rmsnorm(x, g, out_dtype, tr=512):
    m, d = x.shape
    return pl.pallas_call(
        _rmsnorm_kernel,
        grid=(m // tr,),
        in_specs=[pl.BlockSpec((tr, d), lambda i: (i, 0)), pl.BlockSpec((1, d), lambda i: (0, 0))],
        out_specs=pl.BlockSpec((tr, d), lambda i: (i, 0)),
        out_shape=jax.ShapeDtypeStruct((m, d), out_dtype),
        compiler_params=_cparams(("parallel",)),
        name="rmsnorm",
    )(x, g.reshape(1, d))


def _mm_kernel(a_ref, b_ref, o_ref):
    o_ref[...] = jnp.dot(a_ref[...], b_ref[...], preferred_element_type=F32).astype(o_ref.dtype)


def matmul(a, b, out_dtype, tn=1024):
    m, k = a.shape
    n = b.shape[1]
    return pl.pallas_call(
        _mm_kernel,
        grid=(m // TM, n // tn),
        in_specs=[pl.BlockSpec((TM, k), lambda i, j: (i, 0)), pl.BlockSpec((k, tn), lambda i, j: (0, j))],
        out_specs=pl.BlockSpec((TM, tn), lambda i, j: (i, j)),
        out_shape=jax.ShapeDtypeStruct((m, n), out_dtype),
        compiler_params=_cparams(("parallel", "parallel")),
        name="matmul_in",
    )(a, b)


def _merge_kernel(o_ref, w_ref, g_ref, out_ref, acc_ref):
    br = pl.program_id(2)
    y = jnp.dot(o_ref[0], w_ref[0], preferred_element_type=F32) * jax.nn.sigmoid(g_ref[...])

    @pl.when(br == 0)
    def _():
        acc_ref[...] = y

    @pl.when(br != 0)
    def _():
        acc_ref[...] += y

    @pl.when(br == pl.num_programs(2) - 1)
    def _():
        out_ref[...] = acc_ref[...].astype(out_ref.dtype)


def merge_branches(o_abc, w_br, proj, tn=1024):
    _, m, k = o_abc.shape
    n = w_br.shape[2]
    gate0 = COL["gate_a"][0] // tn
    nj = n // tn
    return pl.pallas_call(
        _merge_kernel,
        grid=(m // TM, nj, 3),
        in_specs=[pl.BlockSpec((1, TM, k), lambda i, j, r: (r, i, 0)),
                  pl.BlockSpec((1, k, tn), lambda i, j, r: (r, 0, j)),
                  pl.BlockSpec((TM, tn), lambda i, j, r: (i, gate0 + r * nj + j))],
        out_specs=pl.BlockSpec((TM, tn), lambda i, j, r: (i, j)),
        out_shape=jax.ShapeDtypeStruct((m, n), BF16),
        scratch_shapes=[pltpu.VMEM((TM, tn), F32)],
        compiler_params=_cparams(("parallel", "parallel", "arbitrary")),
        name="merge_branches",
    )(o_abc, w_br, proj)


def _mm_res_kernel(a_ref, b_ref, h_ref, o_ref):
    o_ref[...] = h_ref[...] + jnp.dot(a_ref[...], b_ref[...], preferred_element_type=F32)


def matmul_residual(a, b, h, tn=1024):
    m, k = a.shape
    n = b.shape[1]
    return pl.pallas_call(
        _mm_res_kernel,
        grid=(m // TM, n // tn),
        in_specs=[pl.BlockSpec((TM, k), lambda i, j: (i, 0)), pl.BlockSpec((k, tn), lambda i, j: (0, j)),
                  pl.BlockSpec((TM, tn), lambda i, j: (i, j))],
        out_specs=pl.BlockSpec((TM, tn), lambda i, j: (i, j)),
        out_shape=jax.ShapeDtypeStruct((m, n), F32),
        compiler_params=_cparams(("parallel", "parallel")),
        name="matmul_out",
    )(a, b, h)


def _gateup_kernel(u_ref, wg_ref, wu_ref, o_ref):
    u = u_ref[...]
    g = jnp.dot(u, wg_ref[...], preferred_element_type=F32)
    up = jnp.dot(u, wu_ref[...], preferred_element_type=F32)
    o_ref[...] = (jax.nn.silu(g) * up).astype(o_ref.dtype)


def ffn_gate_up(u, wg, wu, tn=512):
    m, k = u.shape
    n = wg.shape[1]
    return pl.pallas_call(
        _gateup_kernel,
        grid=(m // TM, n // tn),
        in_specs=[pl.BlockSpec((TM, k), lambda i, j: (i, 0)), pl.BlockSpec((k, tn), lambda i, j: (0, j)),
                  pl.BlockSpec((k, tn), lambda i, j: (0, j))],
        out_specs=pl.BlockSpec((TM, tn), lambda i, j: (i, j)),
        out_shape=jax.ShapeDtypeStruct((m, n), BF16),
        compiler_params=_cparams(("parallel", "parallel")),
        name="ffn_gate_up",
    )(u, wg, wu)


def _down_kernel(a_ref, b_ref, h_ref, o_ref, acc_ref):
    kk = pl.program_id(2)
    y = jnp.dot(a_ref[...], b_ref[...], preferred_element_type=F32)

    @pl.when(kk == 0)
    def _():
        acc_ref[...] = h_ref[...] + y

    @pl.when(kk != 0)
    def _():
        acc_ref[...] += y

    @pl.when(kk == pl.num_programs(2) - 1)
    def _():
        o_ref[...] = acc_ref[...]


def ffn_down(a, b, h, tn=1024, tk=2816):
    m, k = a.shape
    n = b.shape[1]
    return pl.pallas_call(
        _down_kernel,
        grid=(m // TM, n // tn, k // tk),
        in_specs=[pl.BlockSpec((TM, tk), lambda i, j, kk: (i, kk)), pl.BlockSpec((tk, tn), lambda i, j, kk: (kk, j)),
                  pl.BlockSpec((TM, tn), lambda i, j, kk: (i, j))],
        out_specs=pl.BlockSpec((TM, tn), lambda i, j, kk: (i, j)),
        out_shape=jax.ShapeDtypeStruct((m, n), F32),
        scratch_shapes=[pltpu.VMEM((TM, tn), F32)],
        compiler_params=_cparams(("parallel", "parallel", "arbitrary")),
        name="ffn_down",
    )(a, b, h)


def _retention_kernel(q_ref, k_ref, v_ref, g_ref, cos_ref, sin_ref, idec_ref, qdec_ref, kdec_ref, cdec_ref, gn_ref,
                      s0_ref, o_ref, sfin_ref, s_ref, *, tb, c):
    tblk = pl.program_id(2)

    @pl.when(tblk == 0)
    def _():
        s_ref[...] = s0_ref[0, 0]

    def rotate(x, cos, sin):
        return x * cos + pltpu.roll(x, DK_A // 2, 1) * sin

    def chunk_step(ci, carry):
        c0 = pl.multiple_of(ci * c, c)
        cos = cos_ref[pl.ds(c0, c), :]
        sin = sin_ref[pl.ds(c0, c), :]
        qr = rotate(q_ref[pl.ds(c0, c), :], cos, sin)
        kr = rotate(k_ref[pl.ds(c0, c), :], cos, sin) * DK_A ** -0.5
        v = v_ref[pl.ds(c0, c), :].astype(BF16)
        s = s_ref[...]
        att = _dot_nt(qr.astype(BF16), kr.astype(BF16)) * idec_ref[0]
        o = _bdot(att, v) + _bdot(qr * qdec_ref[0], s)
        s_ref[...] = s * cdec_ref[0, 0:1, :] + lax.dot_general(
            (kr * kdec_ref[0]).astype(BF16), v, (((0,), (0,)), ((), ())), preferred_element_type=F32)
        mu = jnp.mean(o, axis=-1, keepdims=True)
        d = o - mu
        var = jnp.mean(d * d, axis=-1, keepdims=True)
        g = g_ref[pl.ds(c0, c), :]
        o_ref[pl.ds(c0, c), :] = (g * jax.nn.sigmoid(g) * (d * lax.rsqrt(var + EPS) * gn_ref[...])).astype(o_ref.dtype)
        return carry

    lax.fori_loop(0, tb // c, chunk_step, 0)

    @pl.when(tblk == pl.num_programs(2) - 1)
    def _():
        sfin_ref[0, 0] = s_ref[...]


def retention(proj, row0, b, t, p_len, s_ret, ret_gn):
    c = 2 * CHUNK if t % (2 * CHUNK) == 0 else t
    tb = 512 if t % 512 == 0 else t
    assert tb % c == 0 and row0 % tb == 0
    nt = t // tb
    rb = row0 // tb
    half = DK_A // 2
    inv = ROPE_BASE ** (-jnp.arange(half, dtype=F32) / half)
    ang = (p_len + jnp.arange(t)).astype(F32)[:, None] * inv[None, :]
    cos2 = jnp.concatenate([jnp.cos(ang), jnp.cos(ang)], axis=1)
    sin2 = jnp.concatenate([-jnp.sin(ang), jnp.sin(ang)], axis=1)
    log_gamma = jnp.log(1.0 - 2.0 ** (-5.0 - jnp.arange(H_A, dtype=F32)))
    i = jnp.arange(c, dtype=F32)
    rel = i[:, None] - i[None, :]
    idec = jnp.exp(jnp.where(rel[None] >= 0, rel[None] * log_gamma[:, None, None], -jnp.inf))
    qdec = jnp.broadcast_to(jnp.exp((i[None, :] + 1.0) * log_gamma[:, None])[:, :, None], (H_A, c, DK_A))
    kdec = jnp.broadcast_to(jnp.exp((c - 1.0 - i[None, :]) * log_gamma[:, None])[:, :, None], (H_A, c, DK_A))
    cdec = jnp.broadcast_to(jnp.exp(c * log_gamma)[:, None, None], (H_A, 8, DV_A))
    qb0 = COL["q_a"][0] // DK_A
    kb0 = COL["k_a"][0] // DK_A
    vb0 = COL["v_a"][0] // DV_A
    gb0 = COL["g_a"][0] // DV_A
    sspec = pl.BlockSpec((1, 1, DK_A, DV_A), lambda bi, h, k: (bi, h, 0, 0))
    return pl.pallas_call(
        partial(_retention_kernel, tb=tb, c=c),
        grid=(b, H_A, nt),
        in_specs=[pl.BlockSpec((tb, DK_A), lambda bi, h, k: (rb + bi * nt + k, qb0 + h)),
                  pl.BlockSpec((tb, DK_A), lambda bi, h, k: (rb + bi * nt + k, kb0 + h)),
                  pl.BlockSpec((tb, DV_A), lambda bi, h, k: (rb + bi * nt + k, vb0 + h)),
                  pl.BlockSpec((tb, DV_A), lambda bi, h, k: (rb + bi * nt + k, gb0 + h)),
                  pl.BlockSpec((tb, DK_A), lambda bi, h, k: (k, 0)),
                  pl.BlockSpec((tb, DK_A), lambda bi, h, k: (k, 0)),
                  pl.BlockSpec((1, c, c), lambda bi, h, k: (h, 0, 0)),
                  pl.BlockSpec((1, c, DK_A), lambda bi, h, k: (h, 0, 0)),
                  pl.BlockSpec((1, c, DK_A), lambda bi, h, k: (h, 0, 0)),
                  pl.BlockSpec((1, 8, DV_A), lambda bi, h, k: (h, 0, 0)),
                  pl.BlockSpec((1, DV_A), lambda bi, h, k: (0, h)),
                  sspec],
        out_specs=[pl.BlockSpec((tb, DV_A), lambda bi, h, k: (bi * nt + k, h)), sspec],
        out_shape=[jax.ShapeDtypeStruct((b * t, W_A), BF16),
                   jax.ShapeDtypeStruct((b, H_A, DK_A, DV_A), F32)],
        scratch_shapes=[pltpu.VMEM((DK_A, DV_A), F32)],
        compiler_params=_cparams(("parallel", "parallel", "arbitrary")),
        name="retention",
    )(proj, proj, proj, proj, cos2, sin2, idec, qdec, kdec, cdec, ret_gn.reshape(1, W_A), s_ret)


DELTA_HEADS = 4
DELTA_W = DELTA_HEADS * DK_B
HIST_ROWS = 8
INV_BASE = 16


def _bdot(a, b):
    return jnp.dot(a.astype(BF16), b.astype(BF16), preferred_element_type=F32)


def _hdot(a, b):
    return jnp.dot(a, b, preferred_element_type=F32, precision=lax.Precision.HIGHEST)


def _unit_lower_inverse(mats, c):
    ii = lax.broadcasted_iota(jnp.int32, (c, c), 0)
    jj = lax.broadcasted_iota(jnp.int32, (c, c), 1)
    eye = jnp.where(ii == jj, 1.0, 0.0).astype(F32)
    ps = [jnp.where((ii // INV_BASE) == (jj // INV_BASE), a, 0.0) for a in mats]
    ts = [eye - p for p in ps]
    for _ in range(int(math.log2(INV_BASE)) - 1):
        ps = [_hdot(p, p) for p in ps]
        ts = [_hdot(t, eye + p) for t, p in zip(ts, ps)]
    size = INV_BASE
    while size < c:
        half_blocks = ((ii // (2 * size)) == (jj // (2 * size))) & ((ii // size) != (jj // size))
        tl = [_hdot(t, jnp.where(half_blocks, a, 0.0)) for t, a in zip(ts, mats)]
        ts = [t - _hdot(x, t) for t, x in zip(ts, tl)]
        size *= 2
    return ts


def _delta_kernel(xq_ref, xk_ref, xv_ref, z_ref, small_ref, cwq_ref, cwk_ref, cwv_ref, hq_ref, hk_ref, hv_ref,
                  par_ref, dn_ref, s0_ref, o_ref, sfin_ref,
                  bq_ref, bk_ref, bv_ref, qn_ref, kn_ref, vv_ref, g_ref, beta_ref, s_ref, *, tb, c):
    hg = pl.program_id(1)
    tblk = pl.program_id(2)

    @pl.when(tblk == 0)
    def _():
        bq_ref[0:HIST_ROWS, :] = hq_ref[0]
        bk_ref[0:HIST_ROWS, :] = hk_ref[0]
        bv_ref[0:HIST_ROWS, :] = hv_ref[0]
        s_ref[...] = s0_ref[0]

    def conv_silu(x_ref, buf_ref, cw_ref):
        buf_ref[HIST_ROWS:HIST_ROWS + tb, :] = x_ref[...]
        y = jnp.zeros((tb, DELTA_W), F32)
        for j in range(CONV_W):
            lo = HIST_ROWS - (CONV_W - 1) + j
            y = y + buf_ref[lo:lo + tb, :] * cw_ref[j:j + 1, :]
        buf_ref[0:HIST_ROWS, :] = buf_ref[tb:tb + HIST_ROWS, :]
        return y * jax.nn.sigmoid(y)

    def l2n(x):
        parts = []
        for h in range(DELTA_HEADS):
            xh = x[:, h * DK_B:(h + 1) * DK_B]
            parts.append(xh * lax.rsqrt(jnp.sum(xh * xh, axis=-1, keepdims=True) + EPS))
        return jnp.concatenate(parts, axis=1)

    qn_ref[...] = l2n(conv_silu(xq_ref, bq_ref, cwq_ref)) * DK_B ** -0.5
    kn_ref[...] = l2n(conv_silu(xk_ref, bk_ref, cwk_ref))
    vv_ref[...] = conv_silu(xv_ref, bv_ref, cwv_ref)
    sm = small_ref[...]
    x = sm + par_ref[1:2, :]
    softplus = jnp.maximum(x, 0.0) + jnp.log1p(jnp.exp(-jnp.abs(x)))
    g_ref[...] = -jnp.exp(par_ref[0:1, :]) * softplus
    beta_ref[...] = jax.nn.sigmoid(sm)

    ii = lax.broadcasted_iota(jnp.int32, (c, c), 0)
    jj = lax.broadcasted_iota(jnp.int32, (c, c), 1)
    incl = ii >= jj
    strict = ii > jj
    diag = ii == jj
    tril_ones = jnp.where(incl, 1.0, 0.0).astype(F32)
    a_lane = COL["a_b"][0] - COL["k_i"][0]
    b_lane = COL["b_b"][0] - COL["k_i"][0]

    def chunk_step(ci, carry):
        c0 = pl.multiple_of(ci * c, c)
        gcum_all = _hdot(tril_ones, g_ref[pl.ds(c0, c), :])
        beta_all = beta_ref[pl.ds(c0, c), :]
        lane = lax.broadcasted_iota(jnp.int32, (c, LANE), 1)
        heads = range(DELTA_HEADS)
        cols = [slice(h * DK_B, (h + 1) * DK_B) for h in heads]
        gc = [jnp.sum(jnp.where(lane == a_lane + hg * DELTA_HEADS + h, gcum_all, 0.0), axis=-1, keepdims=True)
              for h in heads]
        bc = [jnp.sum(jnp.where(lane == b_lane + hg * DELTA_HEADS + h, beta_all, 0.0), axis=-1, keepdims=True)
              for h in heads]
        gr = [jnp.sum(jnp.where(diag, jnp.broadcast_to(g, (c, c)), 0.0), axis=0, keepdims=True) for g in gc]
        g_last = [jnp.sum(jnp.where(ii[:, 0:1] == c - 1, g, 0.0), axis=0, keepdims=True) for g in gc]
        gam = [jnp.exp(jnp.where(incl, g - r, -jnp.inf)) for g, r in zip(gc, gr)]
        eg = [jnp.exp(g) for g in gc]
        qh = [qn_ref[pl.ds(c0, c), cs] for cs in cols]
        kh = [kn_ref[pl.ds(c0, c), cs] for cs in cols]
        vh = [vv_ref[pl.ds(c0, c), cs] for cs in cols]
        kb = [k * b for k, b in zip(kh, bc)]
        a = [jnp.where(strict, _dot_nt(x.astype(BF16), k.astype(BF16)) * gm, 0.0) for x, k, gm in zip(kb, kh, gam)]
        qk = [_dot_nt(q.astype(BF16), k.astype(BF16)) * gm for q, k, gm in zip(qh, kh, gam)]
        tinv = _unit_lower_inverse(a, c)
        u = [_bdot(t, v * b) for t, v, b in zip(tinv, vh, bc)]
        w = [_bdot(t, x * e) for t, x, e in zip(tinv, kb, eg)]
        s = [s_ref[h] for h in heads]
        v_new = [x - _bdot(y, st) for x, y, st in zip(u, w, s)]
        o = [_bdot(q * e, st) + _bdot(m, vn) for q, e, st, m, vn in zip(qh, eg, s, qk, v_new)]
        for h in heads:
            kd = (kh[h] * jnp.exp(g_last[h] - gc[h])).astype(BF16)
            s_ref[h] = s[h] * jnp.exp(g_last[h]) + lax.dot_general(
                kd, v_new[h].astype(BF16), (((0,), (0,)), ((), ())), preferred_element_type=F32)
        for h in heads:
            on = o[h] * lax.rsqrt(jnp.mean(o[h] * o[h], axis=-1, keepdims=True) + EPS) * dn_ref[...]
            zh = z_ref[pl.ds(c0, c), cols[h]]
            o_ref[pl.ds(c0, c), cols[h]] = (on * (zh * jax.nn.sigmoid(zh))).astype(o_ref.dtype)
        return carry

    lax.fori_loop(0, tb // c, chunk_step, 0)

    @pl.when(tblk == pl.num_programs(2) - 1)
    def _():
        sfin_ref[0] = s_ref[...]


def gated_delta(proj, row0, b, t, s_conv, s_delta, conv_w, a_log, dt_bias, d_norm):
    c = CHUNK if t % CHUNK == 0 else t
    tb = 512 if t % 512 == 0 else t
    assert tb % c == 0 and c % INV_BASE == 0 and tb >= HIST_ROWS and row0 % tb == 0
    nt = t // tb
    rb = row0 // tb
    ngrp = H_B // DELTA_HEADS
    qcol = COL["qkv_b"][0] // DELTA_W
    hist = jnp.concatenate([jnp.zeros((b, HIST_ROWS - (CONV_W - 1), CONV_CH), F32), s_conv], axis=1)
    par = jnp.zeros((8, LANE), F32)
    a_lane = COL["a_b"][0] - COL["k_i"][0]
    par = par.at[0, a_lane:a_lane + H_B].set(a_log).at[1, a_lane:a_lane + H_B].set(dt_bias)

    def xspec(part):
        return pl.BlockSpec((tb, DELTA_W), lambda bi, g, k: (rb + bi * nt + k, qcol + part * ngrp + g))

    def wspec(part):
        return pl.BlockSpec((CONV_W, DELTA_W), lambda bi, g, k: (0, part * ngrp + g))

    def hspec(part):
        return pl.BlockSpec((1, HIST_ROWS, DELTA_W), lambda bi, g, k: (bi, 0, part * ngrp + g))

    sspec = pl.BlockSpec((1, DELTA_HEADS, DK_B, DV_B), lambda bi, g, k: (bi, g, 0, 0))
    return pl.pallas_call(
        partial(_delta_kernel, tb=tb, c=c),
        grid=(b, ngrp, nt),
        in_specs=[xspec(0), xspec(1), xspec(2),
                  pl.BlockSpec((tb, DELTA_W), lambda bi, g, k: (rb + bi * nt + k, COL["z_b"][0] // DELTA_W + g)),
                  pl.BlockSpec((tb, LANE), lambda bi, g, k: (rb + bi * nt + k, COL["k_i"][0] // LANE)),
                  wspec(0), wspec(1), wspec(2), hspec(0), hspec(1), hspec(2),
                  pl.BlockSpec((8, LANE), lambda bi, g, k: (0, 0)),
                  pl.BlockSpec((1, DV_B), lambda bi, g, k: (0, 0)),
                  sspec],
        out_specs=[pl.BlockSpec((tb, DELTA_W), lambda bi, g, k: (bi * nt + k, g)), sspec],
        out_shape=[jax.ShapeDtypeStruct((b * t, W_B), BF16),
                   jax.ShapeDtypeStruct((b, H_B, DK_B, DV_B), F32)],
        scratch_shapes=[pltpu.VMEM((tb + HIST_ROWS, DELTA_W), F32)] * 3
        + [pltpu.VMEM((tb, DELTA_W), F32)] * 3
        + [pltpu.VMEM((tb, LANE), F32)] * 2
        + [pltpu.VMEM((DELTA_HEADS, DK_B, DV_B), F32)],
        compiler_params=_cparams(("parallel", "parallel", "arbitrary")),
        name="gated_delta",
    )(proj, proj, proj, proj, proj, conv_w, conv_w, conv_w, hist, hist, hist, par, d_norm.reshape(1, DV_B), s_delta)


def t5_bucket(rel):
    half = NUM_BUCKETS // 2
    exact = half // 2
    n = jnp.abs(rel)
    large = exact + (jnp.log(jnp.maximum(n, 1).astype(jnp.float32) / exact)
                     / math.log(MAX_DISTANCE / exact) * (half - exact)).astype(jnp.int32)
    large = jnp.minimum(large, half - 1)
    return jnp.where(rel > 0, half, 0) + jnp.where(n < exact, n, large)


KEY_TILE = LANE
N_BIAS_NEAR = 6
MASKED = -1e30
INT32_MIN = -2 ** 31
GROUP = H_C // H_KV


def _dot_nt(a, b):
    return lax.dot_general(a, b, (((1,), (1,)), ((), ())), preferred_element_type=F32)


def _dsa_kernel(qc_ref, qi_ref, small_ref, k_ref, v_ref, kidx_ref, bias_ref, o_ref,
                key_ref, qis_ref, wb_ref, m_ref, l_ref, acc_ref, *, tq, q0_tile, seq_len, topk):
    qt = q0_tile + pl.program_id(1)
    n_vis = qt + 1
    row = lax.broadcasted_iota(jnp.int32, (tq, KEY_TILE), 0)
    lane = lax.broadcasted_iota(jnp.int32, (tq, KEY_TILE), 1)
    q_chunk = (qt * KEY_TILE + row) // CHUNK

    qi = (qi_ref[...] * D_IDX ** -0.5).astype(BF16)
    w = small_ref[:, D_IDX:D_IDX + H_IDX] * H_IDX ** -0.5
    for h in range(H_IDX):
        qis_ref[h * tq:(h + 1) * tq, :] = qi[:, h * D_IDX:(h + 1) * D_IDX]
        wb_ref[h * tq:(h + 1) * tq, :] = jnp.broadcast_to(w[:, h:h + 1], (tq, KEY_TILE))

    def score_tile(t, carry):
        start = pl.multiple_of(t * KEY_TILE, KEY_TILE)
        kt = kidx_ref[0, pl.ds(start, KEY_TILE), :]
        s = _dot_nt(qis_ref[...], kt)
        r = jnp.maximum(s, 0.0) * wb_ref[...]
        score = jnp.sum(r.reshape(H_IDX, tq, KEY_TILE), axis=0) + 0.0
        bits = pltpu.bitcast(score, jnp.int32)
        key = jnp.where(bits < 0, bits ^ jnp.int32(0x7FFFFFFF), bits)
        kpos = start + lane
        visible = ((kpos // CHUNK) <= q_chunk) & (kpos < seq_len)
        key_ref[t] = jnp.where(visible, key, jnp.int32(INT32_MIN))
        return carry

    lax.fori_loop(0, n_vis, score_tile, 0)

    thr = jnp.full((tq, KEY_TILE), INT32_MIN, jnp.int32)
    for bit in range(31, -1, -1):
        cand = thr + jnp.int32(-2 ** 31 if bit == 31 else 2 ** bit)

        def count_tile(t, cnt, cand=cand):
            return cnt + jnp.where(key_ref[t] >= cand, 1.0, 0.0)

        cnt = lax.fori_loop(0, n_vis, count_tile, jnp.zeros((tq, KEY_TILE), F32))
        total = jnp.sum(cnt, axis=-1, keepdims=True)
        thr = jnp.where(total >= float(topk), cand, thr)
    thr = jnp.maximum(thr, jnp.int32(INT32_MIN + 1))

    for kv in range(H_KV):
        qg = jnp.concatenate(
            [(qc_ref[:, (kv * GROUP + g) * DH_C:(kv * GROUP + g + 1) * DH_C] * DH_C ** -0.5).astype(BF16)
             for g in range(GROUP)], axis=0)
        m_ref[...] = jnp.full(m_ref.shape, MASKED, F32)
        l_ref[...] = jnp.zeros(l_ref.shape, F32)
        acc_ref[...] = jnp.zeros(acc_ref.shape, F32)

        def attend_tile(t, carry, kv=kv, qg=qg):
            start = pl.multiple_of(t * KEY_TILE, KEY_TILE)
            kt = k_ref[0, pl.ds(start, KEY_TILE), kv * DH_C:(kv + 1) * DH_C]
            vt = v_ref[0, pl.ds(start, KEY_TILE), kv * DH_C:(kv + 1) * DH_C]
            s = _dot_nt(qg, kt)
            sel = key_ref[t] >= thr
            dd = jnp.minimum(qt - t, N_BIAS_NEAR)
            parts = []
            for g in range(GROUP):
                bias = bias_ref[dd, kv * GROUP + g, 0:tq, :]
                parts.append(jnp.where(sel, s[g * tq:(g + 1) * tq] + bias, MASKED))
            sm = jnp.concatenate(parts, axis=0)
            m_old = m_ref[...]
            m_new = jnp.maximum(m_old, jnp.max(sm, axis=-1, keepdims=True))
            alpha = jnp.exp(m_old - m_new)
            p = jnp.exp(sm - m_new)
            l_ref[...] = alpha * l_ref[...] + jnp.sum(p, axis=-1, keepdims=True)
            acc_ref[...] = alpha * acc_ref[...] + jnp.dot(p.astype(BF16), vt, preferred_element_type=F32)
            m_ref[...] = m_new
            return carry

        lax.fori_loop(0, n_vis, attend_tile, 0)
        out = acc_ref[...] / l_ref[...]
        for g in range(GROUP):
            c0 = (kv * GROUP + g) * DH_C
            o_ref[:, c0:c0 + DH_C] = out[g * tq:(g + 1) * tq].astype(o_ref.dtype)


def dsa_bias_tiles(rel_bias):
    i = jnp.arange(KEY_TILE)
    d = jnp.arange(N_BIAS_NEAR + 1)
    rel = (i[None, None, :] - i[None, :, None]) - KEY_TILE * d[:, None, None]
    return jnp.transpose(rel_bias[t5_bucket(rel)], (0, 3, 1, 2)).astype(F32)


def dsa_attention(proj, row0, b, t, k_all, v_all, kidx_all, bias_tiles, p_len, seq_len, topk):
    tq = KEY_TILE if t % KEY_TILE == 0 else t
    assert p_len % KEY_TILE == 0 and KEY_TILE % tq == 0 and (tq == KEY_TILE or t == tq) and row0 % tq == 0
    nb = t // tq
    lp = k_all.shape[1]
    assert lp % KEY_TILE == 0 and lp >= p_len + nb * KEY_TILE
    assert 8 * (MAX_DISTANCE / 8) ** (7 / 8) < KEY_TILE * N_BIAS_NEAR - (KEY_TILE - 1)
    rb = row0 // tq
    kern = partial(_dsa_kernel, tq=tq, q0_tile=p_len // KEY_TILE, seq_len=seq_len, topk=topk)
    return pl.pallas_call(
        kern,
        grid=(b, nb),
        in_specs=[pl.BlockSpec((tq, W_C), lambda bi, j: (rb + bi * nb + j, COL["q_c"][0] // W_C)),
                  pl.BlockSpec((tq, H_IDX * D_IDX), lambda bi, j: (rb + bi * nb + j, COL["q_i"][0] // (H_IDX * D_IDX))),
                  pl.BlockSpec((tq, LANE), lambda bi, j: (rb + bi * nb + j, COL["k_i"][0] // LANE)),
                  pl.BlockSpec((1, lp, H_KV * DH_C), lambda bi, j: (bi, 0, 0)),
                  pl.BlockSpec((1, lp, H_KV * DH_C), lambda bi, j: (bi, 0, 0)),
                  pl.BlockSpec((1, lp, D_IDX), lambda bi, j: (bi, 0, 0)),
                  pl.BlockSpec((N_BIAS_NEAR + 1, H_C, KEY_TILE, KEY_TILE), lambda bi, j: (0, 0, 0, 0))],
        out_specs=pl.BlockSpec((tq, W_C), lambda bi, j: (bi * nb + j, 0)),
        out_shape=jax.ShapeDtypeStruct((b * t, W_C), BF16),
        scratch_shapes=[pltpu.VMEM((lp // KEY_TILE, tq, KEY_TILE), jnp.int32),
                        pltpu.VMEM((H_IDX * tq, D_IDX), BF16),
                        pltpu.VMEM((H_IDX * tq, KEY_TILE), F32),
                        pltpu.VMEM((GROUP * tq, 1), F32),
                        pltpu.VMEM((GROUP * tq, 1), F32),
                        pltpu.VMEM((GROUP * tq, DH_C), F32)],
        compiler_params=_cparams(("parallel", "arbitrary")),
        name="dsa_attention",
    )(proj, proj, proj, k_all, v_all, kidx_all, bias_tiles)


def branches(proj_all, row0, b, t, past_k, past_v, past_kidx, s_ret, s_delta, s_conv,
             ret_gn, conv_w, a_log, dt_bias, d_norm, bias_tiles):
    p_len = past_k.shape[1]
    seq_len = p_len + t
    topk = min(TOPK_MAX, seq_len // 4)
    lp = -(-seq_len // KEY_TILE) * KEY_TILE

    def new_cols(name):
        o, w = COL[name]
        return proj_all[row0:row0 + b * t, o:o + w].reshape(b, t, w)

    k_c, v_c, k_i = new_cols("k_c"), new_cols("v_c"), new_cols("k_i")
    qkv_tail = new_cols("qkv_b")[:, -(CONV_W - 1):]
    conv_new = jnp.concatenate([s_conv, qkv_tail], axis=1)[:, -(CONV_W - 1):]

    o_a, ret_new = retention(proj_all, row0, b, t, p_len, s_ret, ret_gn)
    o_b, delta_new = gated_delta(proj_all, row0, b, t, s_conv, s_delta, conv_w, a_log, dt_bias, d_norm)

    def all_keys(past, new):
        w = new.shape[-1]
        return jnp.concatenate([past.reshape(b, p_len, w).astype(BF16), new.astype(BF16),
                                jnp.zeros((b, lp - seq_len, w), BF16)], axis=1)

    o_c = dsa_attention(proj_all, row0, b, t, all_keys(past_k, k_c), all_keys(past_v, v_c),
                        all_keys(past_kidx, k_i), bias_tiles, p_len, seq_len, topk)
    caches = (k_c.reshape(b, t, H_KV, DH_C), v_c.reshape(b, t, H_KV, DH_C), k_i, ret_new, delta_new, conv_new)
    return jnp.stack([o_a, o_b, o_c]), caches


def _prep_w_in(w):
    return jnp.concatenate([w[:, 0:14336], w[:, 14368:16416], w[:, 17440:19488], w[:, 16416:17440],
                            w[:, 19584:31872], w[:, 19488:19584], w[:, 14336:14368],
                            jnp.zeros((w.shape[0], D_IN_PAD - D_IN), w.dtype)], axis=1).astype(BF16)


def kernel(x_prompt, x_sample, cache_k, cache_v, cache_kidx, state_ret, state_delta, state_conv, norm_mix, w_in, ret_gn, conv_w, delta_a_log, delta_dt_bias, delta_norm, rel_bias, w_branch_a, w_branch_b, w_branch_c, w_out, norm_ffn, w_ffn_gate, w_ffn_up, w_ffn_down, norm_final):
    dt = x_prompt.dtype
    bp, tp, _ = x_prompt.shape
    bs, ts, _ = x_sample.shape
    depth = w_in.shape[0]
    n_p = bp * tp
    zk = jnp.zeros((bp, 0, H_KV, DH_C), dt)
    zkidx = jnp.zeros((bp, 0, D_IDX), dt)
    zret = jnp.zeros((bp, H_A, DK_A, DV_A), dt)
    zdelta = jnp.zeros((bp, H_B, DK_B, DV_B), dt)
    zconv = jnp.zeros((bp, CONV_W - 1, CONV_CH), dt)

    h = jnp.concatenate([x_prompt.reshape(n_p, D_MODEL), x_sample.reshape(bs * ts, D_MODEL)], axis=0)
    new_p, new_s = [], []
    bias_tiles = dsa_bias_tiles(rel_bias)
    for l in range(depth):
        u = rmsnorm(h, norm_mix[l], BF16)
        proj = matmul(u, _prep_w_in(w_in[l]), F32)
        bw = (ret_gn[l], conv_w[l], delta_a_log[l], delta_dt_bias[l], delta_norm[l], bias_tiles)
        o_p, sp = branches(proj, 0, bp, tp, zk, zk, zkidx, zret, zdelta, zconv, *bw)
        o_s, ss = branches(proj, n_p, bs, ts, cache_k[l], cache_v[l], cache_kidx[l],
                           state_ret[l], state_delta[l], state_conv[l], *bw)
        new_p.append(sp)
        new_s.append(ss)
        o_abc = jnp.concatenate([o_p, o_s], axis=1)
        w_br = jnp.stack([w_branch_a[l], w_branch_b[l], w_branch_c[l]]).astype(BF16)
        merged = merge_branches(o_abc, w_br, proj)
        h = matmul_residual(merged, w_out[l].astype(BF16), h)
        u2 = rmsnorm(h, norm_ffn[l], BF16)
        pad = D_FF_PAD - D_FF
        wg = jnp.pad(w_ffn_gate[l], ((0, 0), (0, pad))).astype(BF16)
        wu = jnp.pad(w_ffn_up[l], ((0, 0), (0, pad))).astype(BF16)
        wd = jnp.pad(w_ffn_down[l], ((0, pad), (0, 0))).astype(BF16)
        act = ffn_gate_up(u2, wg, wu)
        h = ffn_down(act, wd, h)

    y = rmsnorm(h, norm_final, F32)
    y_prompt = y[:n_p].reshape(bp, tp, D_MODEL)
    y_sample = y[n_p:].reshape(bs, ts, D_MODEL)
    k_p, v_p, kidx_p, ret_p, delta_p, conv_p = [jnp.stack([s[i] for s in new_p]) for i in range(6)]
    k_s, v_s, kidx_s, ret_s, delta_s, conv_s = [jnp.stack([s[i] for s in new_s]) for i in range(6)]
    return (y_prompt, y_sample, k_p, v_p, kidx_p, ret_p, delta_p, conv_p,
            k_s, v_s, kidx_s, ret_s, delta_s, conv_s)
```

```python
import math
from functools import partial

import jax
import jax.numpy as jnp
import numpy as np
from jax import lax
from jax.experimental import pallas as pl
from jax.experimental.pallas import tpu as pltpu

F32 = jnp.float32
BF16 = jnp.bfloat16

D_MODEL = 4096
CHUNK = 64
QUERY_BLOCK = 128
EPS = 1e-6
H_A, DK_A, DV_A = 8, 128, 256
W_A = H_A * DV_A
ROPE_BASE = 10000.0
H_B, DK_B, DV_B = 16, 128, 128
W_B = H_B * DV_B
CONV_W = 4
CONV_CH = 2 * H_B * DK_B + H_B * DV_B
H_C, H_KV, DH_C = 16, 4, 128
W_C = H_C * DH_C
H_IDX, D_IDX = 32, 64
TOPK_MAX = 256
NUM_BUCKETS = 32
MAX_DISTANCE = 1024
D_FF = 11008

LANE = 128
VMEM_LIMIT_V7X = 56 * 1024 * 1024

COL = {}
_off = 0
for _name, _w in [("q_a", 1024), ("k_a", 1024), ("v_a", 2048), ("g_a", 2048), ("qkv_b", 6144), ("z_b", 2048),
                  ("q_c", 2048), ("q_i", 2048), ("k_c", 512), ("v_c", 512),
                  ("gate_a", 4096), ("gate_b", 4096), ("gate_c", 4096),
                  ("k_i", 64), ("w_i", 32), ("a_b", 16), ("b_b", 16)]:
    COL[_name] = (_off, _w)
    _off += _w
D_IN = _off
D_IN_PAD = 32768
D_FF_PAD = 11264
TM = 1024


def _cparams(sem):
    return pltpu.CompilerParams(dimension_semantics=sem, vmem_limit_bytes=VMEM_LIMIT_V7X)


def _rmsnorm_kernel(x_ref, g_ref, o_ref):
    x = x_ref[...]
    y = x * lax.rsqrt(jnp.mean(x * x, axis=-1, keepdims=True) + EPS)
    o_ref[...] = (y * g_ref[...]).astype(o_ref.dtype)


def rmsnorm(x, g, out_dtype, tr=512):
    m, d = x.shape
    return pl.pallas_call(
        _rmsnorm_kernel,
        grid=(m // tr,),
        in_specs=[pl.BlockSpec((tr, d), lambda i: (i, 0)), pl.BlockSpec((1, d), lambda i: (0, 0))],
        out_specs=pl.BlockSpec((tr, d), lambda i: (i, 0)),
        out_shape=jax.ShapeDtypeStruct((m, d), out_dtype),
        compiler_params=_cparams(("parallel",)),
        name="rmsnorm",
    )(x, g.reshape(1, d))


def _mm_kernel(a_ref, b_ref, o_ref):
    o_ref[...] = jnp.dot(a_ref[...], b_ref[...], preferred_element_type=F32).astype(o_ref.dtype)


def matmul(a, b, out_dtype, tn=1024):
    m, k = a.shape
    n = b.shape[1]
    return pl.pallas_call(
        _mm_kernel,
        grid=(m // TM, n // tn),
        in_specs=[pl.BlockSpec((TM, k), lambda i, j: (i, 0)), pl.BlockSpec((k, tn), lambda i, j: (0, j))],
        out_specs=pl.BlockSpec((TM, tn), lambda i, j: (i, j)),
        out_shape=jax.ShapeDtypeStruct((m, n), out_dtype),
        compiler_params=_cparams(("parallel", "parallel")),
        name="matmul_in",
    )(a, b)


def _merge_kernel(o_ref, w_ref, g_ref, out_ref, acc_ref):
    br = pl.program_id(2)
    y = jnp.dot(o_ref[0], w_ref[0], preferred_element_type=F32) * jax.nn.sigmoid(g_ref[...])

    @pl.when(br == 0)
    def _():
        acc_ref[...] = y

    @pl.when(br != 0)
    def _():
        acc_ref[...] += y

    @pl.when(br == pl.num_programs(2) - 1)
    def _():
        out_ref[...] = acc_ref[...].astype(out_ref.dtype)


def merge_branches(o_abc, w_br, proj, tn=1024):
    _, m, k = o_abc.shape
    n = w_br.shape[2]
    gate0 = COL["gate_a"][0] // tn
    nj = n // tn
    return pl.pallas_call(
        _merge_kernel,
        grid=(m // TM, nj, 3),
        in_specs=[pl.BlockSpec((1, TM, k), lambda i, j, r: (r, i, 0)),
                  pl.BlockSpec((1, k, tn), lambda i, j, r: (r, 0, j)),
                  pl.BlockSpec((TM, tn), lambda i, j, r: (i, gate0 + r * nj + j))],
        out_specs=pl.BlockSpec((TM, tn), lambda i, j, r: (i, j)),
        out_shape=jax.ShapeDtypeStruct((m, n), BF16),
        scratch_shapes=[pltpu.VMEM((TM, tn), F32)],
        compiler_params=_cparams(("parallel", "parallel", "arbitrary")),
        name="merge_branches",
    )(o_abc, w_br, proj)


def _mm_res_kernel(a_ref, b_ref, h_ref, o_ref):
    o_ref[...] = h_ref[...] + jnp.dot(a_ref[...], b_ref[...], preferred_element_type=F32)


def matmul_residual(a, b, h, tn=1024):
    m, k = a.shape
    n = b.shape[1]
    return pl.pallas_call(
        _mm_res_kernel,
        grid=(m // TM, n // tn),
        in_specs=[pl.BlockSpec((TM, k), lambda i, j: (i, 0)), pl.BlockSpec((k, tn), lambda i, j: (0, j)),
                  pl.BlockSpec((TM, tn), lambda i, j: (i, j))],
        out_specs=pl.BlockSpec((TM, tn), lambda i, j: (i, j)),
        out_shape=jax.ShapeDtypeStruct((m, n), F32),
        compiler_params=_cparams(("parallel", "parallel")),
        name="matmul_out",
    )(a, b, h)


def _gateup_kernel(u_ref, wg_ref, wu_ref, o_ref):
    u = u_ref[...]
    g = jnp.dot(u, wg_ref[...], preferred_element_type=F32)
    up = jnp.dot(u, wu_ref[...], preferred_element_type=F32)
    o_ref[...] = (jax.nn.silu(g) * up).astype(o_ref.dtype)


def ffn_gate_up(u, wg, wu, tn=512):
    m, k = u.shape
    n = wg.shape[1]
    return pl.pallas_call(
        _gateup_kernel,
        grid=(m // TM, n // tn),
        in_specs=[pl.BlockSpec((TM, k), lambda i, j: (i, 0)), pl.BlockSpec((k, tn), lambda i, j: (0, j)),
                  pl.BlockSpec((k, tn), lambda i, j: (0, j))],
        out_specs=pl.BlockSpec((TM, tn), lambda i, j: (i, j)),
        out_shape=jax.ShapeDtypeStruct((m, n), BF16),
        compiler_params=_cparams(("parallel", "parallel")),
        name="ffn_gate_up",
    )(u, wg, wu)


def _down_kernel(a_ref, b_ref, h_ref, o_ref, acc_ref):
    kk = pl.program_id(2)
    y = jnp.dot(a_ref[...], b_ref[...], preferred_element_type=F32)

    @pl.when(kk == 0)
    def _():
        acc_ref[...] = h_ref[...] + y

    @pl.when(kk != 0)
    def _():
        acc_ref[...] += y

    @pl.when(kk == pl.num_programs(2) - 1)
    def _():
        o_ref[...] = acc_ref[...]


def ffn_down(a, b, h, tn=1024, tk=2816):
    m, k = a.shape
    n = b.shape[1]
    return pl.pallas_call(
        _down_kernel,
        grid=(m // TM, n // tn, k // tk),
        in_specs=[pl.BlockSpec((TM, tk), lambda i, j, kk: (i, kk)), pl.BlockSpec((tk, tn), lambda i, j, kk: (kk, j)),
                  pl.BlockSpec((TM, tn), lambda i, j, kk: (i, j))],
        out_specs=pl.BlockSpec((TM, tn), lambda i, j, kk: (i, j)),
        out_shape=jax.ShapeDtypeStruct((m, n), F32),
        scratch_shapes=[pltpu.VMEM((TM, tn), F32)],
        compiler_params=_cparams(("parallel", "parallel", "arbitrary")),
        name="ffn_down",
    )(a, b, h)


def _retention_kernel(q_ref, k_ref, v_ref, g_ref, cos_ref, sin_ref, idec_ref, qdec_ref, kdec_ref, cdec_ref, gn_ref,
                      s0_ref, o_ref, sfin_ref, s_ref, *, tb, c):
    tblk = pl.program_id(2)

    @pl.when(tblk == 0)
    def _():
        s_ref[...] = s0_ref[0, 0]

    def rotate(x, cos, sin):
        return x * cos + pltpu.roll(x, DK_A // 2, 1) * sin

    def chunk_step(ci, carry):
        c0 = pl.multiple_of(ci * c, c)
        cos = cos_ref[pl.ds(c0, c), :]
        sin = sin_ref[pl.ds(c0, c), :]
        qr = rotate(q_ref[pl.ds(c0, c), :], cos, sin)
        kr = rotate(k_ref[pl.ds(c0, c), :], cos, sin) * DK_A ** -0.5
        v = v_ref[pl.ds(c0, c), :].astype(BF16)
        s = s_ref[...]
        att = _dot_nt(qr.astype(BF16), kr.astype(BF16)) * idec_ref[0]
        o = _bdot(att, v) + _bdot(qr * qdec_ref[0], s)
        s_ref[...] = s * cdec_ref[0, 0:1, :] + lax.dot_general(
            (kr * kdec_ref[0]).astype(BF16), v, (((0,), (0,)), ((), ())), preferred_element_type=F32)
        mu = jnp.mean(o, axis=-1, keepdims=True)
        d = o - mu
        var = jnp.mean(d * d, axis=-1, keepdims=True)
        g = g_ref[pl.ds(c0, c), :]
        o_ref[pl.ds(c0, c), :] = (g * jax.nn.sigmoid(g) * (d * lax.rsqrt(var + EPS) * gn_ref[...])).astype(o_ref.dtype)
        return carry

    lax.fori_loop(0, tb // c, chunk_step, 0)

    @pl.when(tblk == pl.num_programs(2) - 1)
    def _():
        sfin_ref[0, 0] = s_ref[...]


def retention(proj, row0, b, t, p_len, s_ret, ret_gn):
    c = 2 * CHUNK if t % (2 * CHUNK) == 0 else t
    tb = 512 if t % 512 == 0 else t
    assert tb % c == 0 and row0 % tb == 0
    nt = t // tb
    rb = row0 // tb
    half = DK_A // 2
    inv = ROPE_BASE ** (-jnp.arange(half, dtype=F32) / half)
    ang = (p_len + jnp.arange(t)).astype(F32)[:, None] * inv[None, :]
    cos2 = jnp.concatenate([jnp.cos(ang), jnp.cos(ang)], axis=1)
    sin2 = jnp.concatenate([-jnp.sin(ang), jnp.sin(ang)], axis=1)
    log_gamma = jnp.log(1.0 - 2.0 ** (-5.0 - jnp.arange(H_A, dtype=F32)))
    i = jnp.arange(c, dtype=F32)
    rel = i[:, None] - i[None, :]
    idec = jnp.exp(jnp.where(rel[None] >= 0, rel[None] * log_gamma[:, None, None], -jnp.inf))
    qdec = jnp.broadcast_to(jnp.exp((i[None, :] + 1.0) * log_gamma[:, None])[:, :, None], (H_A, c, DK_A))
    kdec = jnp.broadcast_to(jnp.exp((c - 1.0 - i[None, :]) * log_gamma[:, None])[:, :, None], (H_A, c, DK_A))
    cdec = jnp.broadcast_to(jnp.exp(c * log_gamma)[:, None, None], (H_A, 8, DV_A))
    qb0 = COL["q_a"][0] // DK_A
    kb0 = COL["k_a"][0] // DK_A
    vb0 = COL["v_a"][0] // DV_A
    gb0 = COL["g_a"][0] // DV_A
    sspec = pl.BlockSpec((1, 1, DK_A, DV_A), lambda bi, h, k: (bi, h, 0, 0))
    return pl.pallas_call(
        partial(_retention_kernel, tb=tb, c=c),
        grid=(b, H_A, nt),
        in_specs=[pl.BlockSpec((tb, DK_A), lambda bi, h, k: (rb + bi * nt + k, qb0 + h)),
                  pl.BlockSpec((tb, DK_A), lambda bi, h, k: (rb + bi * nt + k, kb0 + h)),
                  pl.BlockSpec((tb, DV_A), lambda bi, h, k: (rb + bi * nt + k, vb0 + h)),
                  pl.BlockSpec((tb, DV_A), lambda bi, h, k: (rb + bi * nt + k, gb0 + h)),
                  pl.BlockSpec((tb, DK_A), lambda bi, h, k: (k, 0)),
                  pl.BlockSpec((tb, DK_A), lambda bi, h, k: (k, 0)),
                  pl.BlockSpec((1, c, c), lambda bi, h, k: (h, 0, 0)),
                  pl.BlockSpec((1, c, DK_A), lambda bi, h, k: (h, 0, 0)),
                  pl.BlockSpec((1, c, DK_A), lambda bi, h, k: (h, 0, 0)),
                  pl.BlockSpec((1, 8, DV_A), lambda bi, h, k: (h, 0, 0)),
                  pl.BlockSpec((1, DV_A), lambda bi, h, k: (0, h)),
                  sspec],
        out_specs=[pl.BlockSpec((tb, DV_A), lambda bi, h, k: (bi * nt + k, h)), sspec],
        out_shape=[jax.ShapeDtypeStruct((b * t, W_A), BF16),
                   jax.ShapeDtypeStruct((b, H_A, DK_A, DV_A), F32)],
        scratch_shapes=[pltpu.VMEM((DK_A, DV_A), F32)],
        compiler_params=_cparams(("parallel", "parallel", "arbitrary")),
        name="retention",
    )(proj, proj, proj, proj, cos2, sin2, idec, qdec, kdec, cdec, ret_gn.reshape(1, W_A), s_ret)


DELTA_HEADS = 8
DELTA_W = DELTA_HEADS * DK_B
HIST_ROWS = 8
INV_BASE = 16


def _bdot(a, b):
    return jnp.dot(a.astype(BF16), b.astype(BF16), preferred_element_type=F32)


def _hdot(a, b):
    return jnp.dot(a, b, preferred_element_type=F32, precision=lax.Precision.HIGHEST)


def _split_bf16(x):
    hi = x.astype(BF16)
    return hi, (x - hi.astype(F32)).astype(BF16)


def _dot3(a, b):
    a_hi, a_lo = _split_bf16(a)
    b_hi, b_lo = _split_bf16(b)
    dot = partial(jnp.dot, preferred_element_type=F32)
    return dot(a_hi, b_hi) + (dot(a_hi, b_lo) + dot(a_lo, b_hi))


def _unit_lower_inverse(mats, c):
    ii = lax.broadcasted_iota(jnp.int32, (c, c), 0)
    jj = lax.broadcasted_iota(jnp.int32, (c, c), 1)
    eye = jnp.where(ii == jj, 1.0, 0.0).astype(F32)
    ps = [jnp.where((ii // INV_BASE) == (jj // INV_BASE), a, 0.0) for a in mats]
    ts = [eye - p for p in ps]
    for _ in range(int(math.log2(INV_BASE)) - 1):
        ps = [_dot3(p, p) for p in ps]
        ts = [_dot3(t, eye + p) for t, p in zip(ts, ps)]
    size = INV_BASE
    while size < c:
        half_blocks = ((ii // (2 * size)) == (jj // (2 * size))) & ((ii // size) != (jj // size))
        tl = [_dot3(t, jnp.where(half_blocks, a, 0.0)) for t, a in zip(ts, mats)]
        ts = [t - _dot3(x, t) for t, x in zip(ts, tl)]
        size *= 2
    return ts


def _delta_kernel(xq_ref, xk_ref, xv_ref, z_ref, small_ref, cwq_ref, cwk_ref, cwv_ref, hq_ref, hk_ref, hv_ref,
                  par_ref, dn_ref, s0_ref, o_ref, sfin_ref,
                  bq_ref, bk_ref, bv_ref, qn_ref, kn_ref, vv_ref, g_ref, beta_ref, s_ref, *, tb, c):
    hg = pl.program_id(1)
    tblk = pl.program_id(2)

    @pl.when(tblk == 0)
    def _():
        bq_ref[0:HIST_ROWS, :] = hq_ref[0]
        bk_ref[0:HIST_ROWS, :] = hk_ref[0]
        bv_ref[0:HIST_ROWS, :] = hv_ref[0]
        s_ref[...] = s0_ref[0]

    def conv_silu(x_ref, buf_ref, cw_ref):
        buf_ref[HIST_ROWS:HIST_ROWS + tb, :] = x_ref[...]
        y = jnp.zeros((tb, DELTA_W), F32)
        for j in range(CONV_W):
            lo = HIST_ROWS - (CONV_W - 1) + j
            y = y + buf_ref[lo:lo + tb, :] * cw_ref[j:j + 1, :]
        buf_ref[0:HIST_ROWS, :] = buf_ref[tb:tb + HIST_ROWS, :]
        return y * jax.nn.sigmoid(y)

    def l2n(x):
        parts = []
        for h in range(DELTA_HEADS):
            xh = x[:, h * DK_B:(h + 1) * DK_B]
            parts.append(xh * lax.rsqrt(jnp.sum(xh * xh, axis=-1, keepdims=True) + EPS))
        return jnp.concatenate(parts, axis=1)

    qn_ref[...] = l2n(conv_silu(xq_ref, bq_ref, cwq_ref)) * DK_B ** -0.5
    kn_ref[...] = l2n(conv_silu(xk_ref, bk_ref, cwk_ref))
    vv_ref[...] = conv_silu(xv_ref, bv_ref, cwv_ref)
    sm = small_ref[...]
    x = sm + par_ref[1:2, :]
    softplus = jnp.maximum(x, 0.0) + jnp.log1p(jnp.exp(-jnp.abs(x)))
    g_ref[...] = -jnp.exp(par_ref[0:1, :]) * softplus
    beta_ref[...] = jax.nn.sigmoid(sm)

    ii = lax.broadcasted_iota(jnp.int32, (c, c), 0)
    jj = lax.broadcasted_iota(jnp.int32, (c, c), 1)
    incl = ii >= jj
    strict = ii > jj
    diag = ii == jj
    tril_ones = jnp.where(incl, 1.0, 0.0).astype(F32)
    a_lane = COL["a_b"][0] - COL["k_i"][0]
    b_lane = COL["b_b"][0] - COL["k_i"][0]

    def chunk_step(ci, carry):
        c0 = pl.multiple_of(ci * c, c)
        gcum_all = _hdot(tril_ones, g_ref[pl.ds(c0, c), :])
        beta_all = beta_ref[pl.ds(c0, c), :]
        lane = lax.broadcasted_iota(jnp.int32, (c, LANE), 1)
        heads = range(DELTA_HEADS)
        cols = [slice(h * DK_B, (h + 1) * DK_B) for h in heads]
        gc = [jnp.sum(jnp.where(lane == a_lane + hg * DELTA_HEADS + h, gcum_all, 0.0), axis=-1, keepdims=True)
              for h in heads]
        bc = [jnp.sum(jnp.where(lane == b_lane + hg * DELTA_HEADS + h, beta_all, 0.0), axis=-1, keepdims=True)
              for h in heads]
        gr = [jnp.sum(jnp.where(diag, jnp.broadcast_to(g, (c, c)), 0.0), axis=0, keepdims=True) for g in gc]
        g_last = [jnp.sum(jnp.where(ii[:, 0:1] == c - 1, g, 0.0), axis=0, keepdims=True) for g in gc]
        gam = [jnp.exp(jnp.where(incl, g - r, -jnp.inf)) for g, r in zip(gc, gr)]
        eg = [jnp.exp(g) for g in gc]
        qh = [qn_ref[pl.ds(c0, c), cs] for cs in cols]
        kh = [kn_ref[pl.ds(c0, c), cs] for cs in cols]
        vh = [vv_ref[pl.ds(c0, c), cs] for cs in cols]
        kb = [k * b for k, b in zip(kh, bc)]
        a = [jnp.where(strict, _dot_nt(x.astype(BF16), k.astype(BF16)) * gm, 0.0) for x, k, gm in zip(kb, kh, gam)]
        qk = [_dot_nt(q.astype(BF16), k.astype(BF16)) * gm for q, k, gm in zip(qh, kh, gam)]
        tinv = _unit_lower_inverse(a, c)
        u = [_bdot(t, v * b) for t, v, b in zip(tinv, vh, bc)]
        w = [_bdot(t, x * e) for t, x, e in zip(tinv, kb, eg)]
        s = [s_ref[h] for h in heads]
        v_new = [x - _bdot(y, st) for x, y, st in zip(u, w, s)]
        o = [_bdot(q * e, st) + _bdot(m, vn) for q, e, st, m, vn in zip(qh, eg, s, qk, v_new)]
        for h in heads:
            kd = (kh[h] * jnp.exp(g_last[h] - gc[h])).astype(BF16)
            s_ref[h] = s[h] * jnp.exp(g_last[h]) + lax.dot_general(
                kd, v_new[h].astype(BF16), (((0,), (0,)), ((), ())), preferred_element_type=F32)
        for h in heads:
            on = o[h] * lax.rsqrt(jnp.mean(o[h] * o[h], axis=-1, keepdims=True) + EPS) * dn_ref[...]
            zh = z_ref[pl.ds(c0, c), cols[h]]
            o_ref[pl.ds(c0, c), cols[h]] = (on * (zh * jax.nn.sigmoid(zh))).astype(o_ref.dtype)
        return carry

    lax.fori_loop(0, tb // c, chunk_step, 0)

    @pl.when(tblk == pl.num_programs(2) - 1)
    def _():
        sfin_ref[0] = s_ref[...]


def gated_delta(proj, row0, b, t, s_conv, s_delta, conv_w, a_log, dt_bias, d_norm):
    c = CHUNK if t % CHUNK == 0 else t
    tb = 512 if t % 512 == 0 else t
    assert tb % c == 0 and c % INV_BASE == 0 and tb >= HIST_ROWS and row0 % tb == 0
    nt = t // tb
    rb = row0 // tb
    ngrp = H_B // DELTA_HEADS
    qcol = COL["qkv_b"][0] // DELTA_W
    hist = jnp.concatenate([jnp.zeros((b, HIST_ROWS - (CONV_W - 1), CONV_CH), F32), s_conv], axis=1)
    par = jnp.zeros((8, LANE), F32)
    a_lane = COL["a_b"][0] - COL["k_i"][0]
    par = par.at[0, a_lane:a_lane + H_B].set(a_log).at[1, a_lane:a_lane + H_B].set(dt_bias)

    def xspec(part):
        return pl.BlockSpec((tb, DELTA_W), lambda bi, g, k: (rb + bi * nt + k, qcol + part * ngrp + g))

    def wspec(part):
        return pl.BlockSpec((CONV_W, DELTA_W), lambda bi, g, k: (0, part * ngrp + g))

    def hspec(part):
        return pl.BlockSpec((1, HIST_ROWS, DELTA_W), lambda bi, g, k: (bi, 0, part * ngrp + g))

    sspec = pl.BlockSpec((1, DELTA_HEADS, DK_B, DV_B), lambda bi, g, k: (bi, g, 0, 0))
    return pl.pallas_call(
        partial(_delta_kernel, tb=tb, c=c),
        grid=(b, ngrp, nt),
        in_specs=[xspec(0), xspec(1), xspec(2),
                  pl.BlockSpec((tb, DELTA_W), lambda bi, g, k: (rb + bi * nt + k, COL["z_b"][0] // DELTA_W + g)),
                  pl.BlockSpec((tb, LANE), lambda bi, g, k: (rb + bi * nt + k, COL["k_i"][0] // LANE)),
                  wspec(0), wspec(1), wspec(2), hspec(0), hspec(1), hspec(2),
                  pl.BlockSpec((8, LANE), lambda bi, g, k: (0, 0)),
                  pl.BlockSpec((1, DV_B), lambda bi, g, k: (0, 0)),
                  sspec],
        out_specs=[pl.BlockSpec((tb, DELTA_W), lambda bi, g, k: (bi * nt + k, g)), sspec],
        out_shape=[jax.ShapeDtypeStruct((b * t, W_B), BF16),
                   jax.ShapeDtypeStruct((b, H_B, DK_B, DV_B), F32)],
        scratch_shapes=[pltpu.VMEM((tb + HIST_ROWS, DELTA_W), F32)] * 3
        + [pltpu.VMEM((tb, DELTA_W), F32)] * 3
        + [pltpu.VMEM((tb, LANE), F32)] * 2
        + [pltpu.VMEM((DELTA_HEADS, DK_B, DV_B), F32)],
        compiler_params=_cparams(("parallel", "parallel", "arbitrary")),
        name="gated_delta",
    )(proj, proj, proj, proj, proj, conv_w, conv_w, conv_w, hist, hist, hist, par, d_norm.reshape(1, DV_B), s_delta)


def t5_bucket(rel):
    half = NUM_BUCKETS // 2
    exact = half // 2
    n = jnp.abs(rel)
    large = exact + (jnp.log(jnp.maximum(n, 1).astype(jnp.float32) / exact)
                     / math.log(MAX_DISTANCE / exact) * (half - exact)).astype(jnp.int32)
    large = jnp.minimum(large, half - 1)
    return jnp.where(rel > 0, half, 0) + jnp.where(n < exact, n, large)


KEY_TILE = LANE
N_BIAS_NEAR = 6
MASKED = -1e30
INT32_MIN = -2 ** 31
GROUP = H_C // H_KV


def _dot_nt(a, b):
    return lax.dot_general(a, b, (((1,), (1,)), ((), ())), preferred_element_type=F32)


def _dsa_kernel(qc_ref, qi_ref, small_ref, k_ref, v_ref, kidx_ref, bias_ref, o_ref,
                key_ref, qis_ref, wb_ref, qg_ref, acc_ref, m_ref, l_ref, *, tq, q0_tile, seq_len, topk):
    qt = q0_tile + pl.program_id(1)
    n_vis = qt + 1
    row = lax.broadcasted_iota(jnp.int32, (tq, KEY_TILE), 0)
    lane = lax.broadcasted_iota(jnp.int32, (tq, KEY_TILE), 1)
    q_chunk = (qt * KEY_TILE + row) // CHUNK
    pad_rows = KEY_TILE - tq

    qi = (qi_ref[...] * D_IDX ** -0.5).astype(BF16)
    w = small_ref[:, D_IDX:D_IDX + H_IDX] * H_IDX ** -0.5
    for i in range(H_IDX // 2):
        qis_ref[i * tq:(i + 1) * tq, :] = qi[:, i * LANE:(i + 1) * LANE]
        for j in range(2):
            wb_ref[i * tq:(i + 1) * tq, j * KEY_TILE:(j + 1) * KEY_TILE] = jnp.broadcast_to(
                w[:, 2 * i + j:2 * i + j + 1], (tq, KEY_TILE))
    first_head = lax.broadcasted_iota(jnp.int32, (KEY_TILE, LANE), 1) < D_IDX

    def score_tile(t, carry):
        start = pl.multiple_of(t * KEY_TILE, KEY_TILE)
        kd = kidx_ref[0, pl.ds(start, KEY_TILE), :]
        zero = jnp.zeros_like(kd)
        kt2 = jnp.concatenate([jnp.where(first_head, kd, zero), jnp.where(first_head, zero, kd)], axis=0)
        s = _dot_nt(qis_ref[...], kt2)
        r = jnp.maximum(s, 0.0) * wb_ref[...]
        r = jnp.sum(r.reshape(H_IDX // 2, tq, 2 * KEY_TILE), axis=0)
        score = r[:, :KEY_TILE] + r[:, KEY_TILE:] + 0.0
        bits = pltpu.bitcast(score, jnp.int32)
        key = jnp.where(bits < 0, bits ^ jnp.int32(0x7FFFFFFF), bits)
        kpos = start + lane
        visible = ((kpos // CHUNK) <= q_chunk) & (kpos < seq_len)
        key = jnp.where(visible, key, jnp.int32(INT32_MIN))
        if pad_rows:
            key = jnp.concatenate([key, jnp.full((pad_rows, KEY_TILE), INT32_MIN, jnp.int32)], axis=0)
        key_ref[t] = key.T
        return carry

    lax.fori_loop(0, n_vis, score_tile, 0)

    thr = jnp.full((1, KEY_TILE), INT32_MIN, jnp.int32)
    for bit in range(31, -1, -1):
        cand = thr + jnp.int32(-2 ** 31 if bit == 31 else 2 ** bit)

        def count_tile(t, cnt, cand=cand):
            return cnt + jnp.where(key_ref[t] >= cand, 1.0, 0.0)

        cnt = lax.fori_loop(0, n_vis, count_tile, jnp.zeros((KEY_TILE, KEY_TILE), F32))
        total = jnp.sum(cnt, axis=0, keepdims=True)
        thr = jnp.where(total >= float(topk), cand, thr)
    thr = jnp.maximum(thr, jnp.int32(INT32_MIN + 1))
    thr_g = jnp.concatenate([thr] * GROUP, axis=1)

    kvs = range(H_KV)
    for kv in kvs:
        for g in range(GROUP):
            c0 = (kv * GROUP + g) * DH_C
            qh = (qc_ref[:, c0:c0 + DH_C] * DH_C ** -0.5).astype(BF16)
            if pad_rows:
                qh = jnp.concatenate([qh, jnp.zeros((pad_rows, DH_C), BF16)], axis=0)
            qg_ref[kv, g * KEY_TILE:(g + 1) * KEY_TILE, :] = qh
    acc_ref[...] = jnp.zeros(acc_ref.shape, F32)
    m_ref[...] = jnp.full(m_ref.shape, MASKED, F32)
    l_ref[...] = jnp.zeros(l_ref.shape, F32)

    def attend_tile(t, carry):
        start = pl.multiple_of(t * KEY_TILE, KEY_TILE)
        sel = jnp.concatenate([key_ref[t]] * GROUP, axis=1) >= thr_g
        dd = jnp.minimum(qt - t, N_BIAS_NEAR)
        s = [_dot_nt(k_ref[0, pl.ds(start, KEY_TILE), kv * DH_C:(kv + 1) * DH_C], qg_ref[kv]) for kv in kvs]
        sm = [jnp.where(sel, s[kv] + bias_ref[dd, kv], MASKED) for kv in kvs]
        m_old = [m_ref[kv:kv + 1, :] for kv in kvs]
        m_new = [jnp.maximum(m_old[kv], jnp.max(sm[kv], axis=0, keepdims=True)) for kv in kvs]
        alpha = [jnp.exp(m_old[kv] - m_new[kv]) for kv in kvs]
        p = [jnp.exp(sm[kv] - m_new[kv]) for kv in kvs]
        pv = [lax.dot_general(v_ref[0, pl.ds(start, KEY_TILE), kv * DH_C:(kv + 1) * DH_C], p[kv].astype(BF16),
                              (((0,), (0,)), ((), ())), preferred_element_type=F32) for kv in kvs]
        for kv in kvs:
            l_ref[kv:kv + 1, :] = alpha[kv] * l_ref[kv:kv + 1, :] + jnp.sum(p[kv], axis=0, keepdims=True)
            acc_ref[kv] = alpha[kv] * acc_ref[kv] + pv[kv]
            m_ref[kv:kv + 1, :] = m_new[kv]
        return carry

    lax.fori_loop(0, n_vis, attend_tile, 0)
    for kv in kvs:
        out = (acc_ref[kv] / l_ref[kv:kv + 1, :]).T
        for g in range(GROUP):
            c0 = (kv * GROUP + g) * DH_C
            o_ref[:, c0:c0 + DH_C] = out[g * KEY_TILE:g * KEY_TILE + tq].astype(o_ref.dtype)


def dsa_bias_tiles(rel_bias):
    i = jnp.arange(KEY_TILE)
    d = jnp.arange(N_BIAS_NEAR + 1)
    rel = (i[None, :, None] - i[None, None, :]) - KEY_TILE * d[:, None, None]
    onehot = (t5_bucket(rel)[..., None] == jnp.arange(NUM_BUCKETS)).astype(F32)
    tiles = jnp.einsum('dkqn,nh->dkqh', onehot, rel_bias.astype(F32), precision=lax.Precision.HIGHEST)
    tiles = tiles.reshape(N_BIAS_NEAR + 1, KEY_TILE, KEY_TILE, H_KV, GROUP)
    return jnp.transpose(tiles, (0, 3, 1, 4, 2)).reshape(N_BIAS_NEAR + 1, H_KV, KEY_TILE, GROUP * KEY_TILE)


def dsa_attention(proj, row0, b, t, k_all, v_all, kidx_all, bias_tiles, p_len, seq_len, topk):
    tq = KEY_TILE if t % KEY_TILE == 0 else t
    assert p_len % KEY_TILE == 0 and KEY_TILE % tq == 0 and (tq == KEY_TILE or t == tq) and row0 % tq == 0
    nb = t // tq
    lp = k_all.shape[1]
    assert lp % KEY_TILE == 0 and lp >= p_len + nb * KEY_TILE
    assert 8 * (MAX_DISTANCE / 8) ** (7 / 8) < KEY_TILE * N_BIAS_NEAR - (KEY_TILE - 1)
    rb = row0 // tq
    kern = partial(_dsa_kernel, tq=tq, q0_tile=p_len // KEY_TILE, seq_len=seq_len, topk=topk)
    return pl.pallas_call(
        kern,
        grid=(b, nb),
        in_specs=[pl.BlockSpec((tq, W_C), lambda bi, j: (rb + bi * nb + j, COL["q_c"][0] // W_C)),
                  pl.BlockSpec((tq, H_IDX * D_IDX), lambda bi, j: (rb + bi * nb + j, COL["q_i"][0] // (H_IDX * D_IDX))),
                  pl.BlockSpec((tq, LANE), lambda bi, j: (rb + bi * nb + j, COL["k_i"][0] // LANE)),
                  pl.BlockSpec((1, lp, H_KV * DH_C), lambda bi, j: (bi, 0, 0)),
                  pl.BlockSpec((1, lp, H_KV * DH_C), lambda bi, j: (bi, 0, 0)),
                  pl.BlockSpec((1, lp, 2 * D_IDX), lambda bi, j: (bi, 0, 0)),
                  pl.BlockSpec((N_BIAS_NEAR + 1, H_KV, KEY_TILE, GROUP * KEY_TILE), lambda bi, j: (0, 0, 0, 0))],
        out_specs=pl.BlockSpec((tq, W_C), lambda bi, j: (bi * nb + j, 0)),
        out_shape=jax.ShapeDtypeStruct((b * t, W_C), BF16),
        scratch_shapes=[pltpu.VMEM((lp // KEY_TILE, KEY_TILE, KEY_TILE), jnp.int32),
                        pltpu.VMEM((H_IDX // 2 * tq, 2 * D_IDX), BF16),
                        pltpu.VMEM((H_IDX // 2 * tq, 2 * KEY_TILE), F32),
                        pltpu.VMEM((H_KV, GROUP * KEY_TILE, DH_C), BF16),
                        pltpu.VMEM((H_KV, DH_C, GROUP * KEY_TILE), F32),
                        pltpu.VMEM((8, GROUP * KEY_TILE), F32),
                        pltpu.VMEM((8, GROUP * KEY_TILE), F32)],
        compiler_params=_cparams(("parallel", "arbitrary")),
        name="dsa_attention",
    )(proj, proj, proj, k_all, v_all, kidx_all, bias_tiles)


def branches(proj_all, row0, b, t, past_k, past_v, past_kidx, s_ret, s_delta, s_conv,
             ret_gn, conv_w, a_log, dt_bias, d_norm, bias_tiles):
    p_len = past_k.shape[1]
    seq_len = p_len + t
    topk = min(TOPK_MAX, seq_len // 4)
    lp = -(-seq_len // KEY_TILE) * KEY_TILE

    def new_cols(name):
        o, w = COL[name]
        return proj_all[row0:row0 + b * t, o:o + w].reshape(b, t, w)

    k_c, v_c, k_i = new_cols("k_c"), new_cols("v_c"), new_cols("k_i")
    qkv_tail = new_cols("qkv_b")[:, -(CONV_W - 1):]
    conv_new = jnp.concatenate([s_conv, qkv_tail], axis=1)[:, -(CONV_W - 1):]

    o_a, ret_new = retention(proj_all, row0, b, t, p_len, s_ret, ret_gn)
    o_b, delta_new = gated_delta(proj_all, row0, b, t, s_conv, s_delta, conv_w, a_log, dt_bias, d_norm)

    def all_keys(past, new):
        w = new.shape[-1]
        return jnp.concatenate([past.reshape(b, p_len, w).astype(BF16), new.astype(BF16),
                                jnp.zeros((b, lp - seq_len, w), BF16)], axis=1)

    o_c = dsa_attention(proj_all, row0, b, t, all_keys(past_k, k_c), all_keys(past_v, v_c),
                        jnp.tile(all_keys(past_kidx, k_i), (1, 1, 2)), bias_tiles, p_len, seq_len, topk)
    caches = (k_c.reshape(b, t, H_KV, DH_C), v_c.reshape(b, t, H_KV, DH_C), k_i, ret_new, delta_new, conv_new)
    return jnp.stack([o_a, o_b, o_c]), caches


def _prep_w_in(w):
    return jnp.concatenate([w[:, 0:14336], w[:, 14368:16416], w[:, 17440:19488], w[:, 16416:17440],
                            w[:, 19584:31872], w[:, 19488:19584], w[:, 14336:14368],
                            jnp.zeros((w.shape[0], D_IN_PAD - D_IN), w.dtype)], axis=1).astype(BF16)


def kernel(x_prompt, x_sample, cache_k, cache_v, cache_kidx, state_ret, state_delta, state_conv, norm_mix, w_in, ret_gn, conv_w, delta_a_log, delta_dt_bias, delta_norm, rel_bias, w_branch_a, w_branch_b, w_branch_c, w_out, norm_ffn, w_ffn_gate, w_ffn_up, w_ffn_down, norm_final):
    dt = x_prompt.dtype
    bp, tp, _ = x_prompt.shape
    bs, ts, _ = x_sample.shape
    depth = w_in.shape[0]
    n_p = bp * tp
    zk = jnp.zeros((bp, 0, H_KV, DH_C), dt)
    zkidx = jnp.zeros((bp, 0, D_IDX), dt)
    zret = jnp.zeros((bp, H_A, DK_A, DV_A), dt)
    zdelta = jnp.zeros((bp, H_B, DK_B, DV_B), dt)
    zconv = jnp.zeros((bp, CONV_W - 1, CONV_CH), dt)

    h = jnp.concatenate([x_prompt.reshape(n_p, D_MODEL), x_sample.reshape(bs * ts, D_MODEL)], axis=0)
    new_p, new_s = [], []
    bias_tiles = dsa_bias_tiles(rel_bias)
    for l in range(depth):
        u = rmsnorm(h, norm_mix[l], BF16)
        proj = matmul(u, _prep_w_in(w_in[l]), F32)
        bw = (ret_gn[l], conv_w[l], delta_a_log[l], delta_dt_bias[l], delta_norm[l], bias_tiles)
        o_p, sp = branches(proj, 0, bp, tp, zk, zk, zkidx, zret, zdelta, zconv, *bw)
        o_s, ss = branches(proj, n_p, bs, ts, cache_k[l], cache_v[l], cache_kidx[l],
                           state_ret[l], state_delta[l], state_conv[l], *bw)
        new_p.append(sp)
        new_s.append(ss)
        o_abc = jnp.concatenate([o_p, o_s], axis=1)
        w_br = jnp.stack([w_branch_a[l], w_branch_b[l], w_branch_c[l]]).astype(BF16)
        merged = merge_branches(o_abc, w_br, proj)
        h = matmul_residual(merged, w_out[l].astype(BF16), h)
        u2 = rmsnorm(h, norm_ffn[l], BF16)
        pad = D_FF_PAD - D_FF
        wg = jnp.pad(w_ffn_gate[l], ((0, 0), (0, pad))).astype(BF16)
        wu = jnp.pad(w_ffn_up[l], ((0, 0), (0, pad))).astype(BF16)
        wd = jnp.pad(w_ffn_down[l], ((0, pad), (0, 0))).astype(BF16)
        act = ffn_gate_up(u2, wg, wu)
        h = ffn_down(act, wd, h)

    y = rmsnorm(h, norm_final, F32)
    y_prompt = y[:n_p].reshape(bp, tp, D_MODEL)
    y_sample = y[n_p:].reshape(bs, ts, D_MODEL)
    k_p, v_p, kidx_p, ret_p, delta_p, conv_p = [jnp.stack([s[i] for s in new_p]) for i in range(6)]
    k_s, v_s, kidx_s, ret_s, delta_s, conv_s = [jnp.stack([s[i] for s in new_s]) for i in range(6)]
    return (y_prompt, y_sample, k_p, v_p, kidx_p, ret_p, delta_p, conv_p,
            k_s, v_s, kidx_s, ret_s, delta_s, conv_s)
```

```python
import math
from functools import partial

import jax
import jax.numpy as jnp
import numpy as np
from jax import lax
from jax.experimental import pallas as pl
from jax.experimental.pallas import tpu as pltpu

F32 = jnp.float32
BF16 = jnp.bfloat16

D_MODEL = 4096
CHUNK = 64
QUERY_BLOCK = 128
EPS = 1e-6
H_A, DK_A, DV_A = 8, 128, 256
W_A = H_A * DV_A
ROPE_BASE = 10000.0
H_B, DK_B, DV_B = 16, 128, 128
W_B = H_B * DV_B
CONV_W = 4
CONV_CH = 2 * H_B * DK_B + H_B * DV_B
H_C, H_KV, DH_C = 16, 4, 128
W_C = H_C * DH_C
H_IDX, D_IDX = 32, 64
TOPK_MAX = 256
NUM_BUCKETS = 32
MAX_DISTANCE = 1024
D_FF = 11008

LANE = 128
VMEM_LIMIT_V7X = 56 * 1024 * 1024

COL = {}
_off = 0
for _name, _w in [("q_a", 1024), ("k_a", 1024), ("v_a", 2048), ("g_a", 2048), ("qkv_b", 6144), ("z_b", 2048),
                  ("q_c", 2048), ("q_i", 2048), ("k_c", 512), ("v_c", 512),
                  ("gate_a", 4096), ("gate_b", 4096), ("gate_c", 4096),
                  ("k_i", 64), ("w_i", 32), ("a_b", 16), ("b_b", 16)]:
    COL[_name] = (_off, _w)
    _off += _w
D_IN = _off
D_IN_PAD = 32768
D_FF_PAD = 11264
TM = 1024


def _cparams(sem):
    return pltpu.CompilerParams(dimension_semantics=sem, vmem_limit_bytes=VMEM_LIMIT_V7X)


def _rmsnorm_kernel(x_ref, g_ref, o_ref):
    x = x_ref[...]
    y = x * lax.rsqrt(jnp.mean(x * x, axis=-1, keepdims=True) + EPS)
    o_ref[...] = (y * g_ref[...]).astype(o_ref.dtype)


def rmsnorm(x, g, out_dtype, tr=512):
    m, d = x.shape
    return pl.pallas_call(
        _rmsnorm_kernel,
        grid=(m // tr,),
        in_specs=[pl.BlockSpec((tr, d), lambda i: (i, 0)), pl.BlockSpec((1, d), lambda i: (0, 0))],
        out_specs=pl.BlockSpec((tr, d), lambda i: (i, 0)),
        out_shape=jax.ShapeDtypeStruct((m, d), out_dtype),
        compiler_params=_cparams(("parallel",)),
        name="rmsnorm",
    )(x, g.reshape(1, d))


def _mm_kernel(a_ref, b_ref, o_ref):
    o_ref[...] = jnp.dot(a_ref[...], b_ref[...], preferred_element_type=F32).astype(o_ref.dtype)


def matmul(a, b, out_dtype, tn=1024):
    m, k = a.shape
    n = b.shape[1]
    return pl.pallas_call(
        _mm_kernel,
        grid=(m // TM, n // tn),
        in_specs=[pl.BlockSpec((TM, k), lambda i, j: (i, 0)), pl.BlockSpec((k, tn), lambda i, j: (0, j))],
        out_specs=pl.BlockSpec((TM, tn), lambda i, j: (i, j)),
        out_shape=jax.ShapeDtypeStruct((m, n), out_dtype),
        compiler_params=_cparams(("parallel", "parallel")),
        name="matmul_in",
    )(a, b)


def _merge_kernel(o_ref, w_ref, g_ref, out_ref, acc_ref):
    br = pl.program_id(2)
    y = jnp.dot(o_ref[0], w_ref[0], preferred_element_type=F32) * jax.nn.sigmoid(g_ref[...])

    @pl.when(br == 0)
    def _():
        acc_ref[...] = y

    @pl.when(br != 0)
    def _():
        acc_ref[...] += y

    @pl.when(br == pl.num_programs(2) - 1)
    def _():
        out_ref[...] = acc_ref[...].astype(out_ref.dtype)


def merge_branches(o_abc, w_br, proj, tn=1024):
    _, m, k = o_abc.shape
    n = w_br.shape[2]
    gate0 = COL["gate_a"][0] // tn
    nj = n // tn
    return pl.pallas_call(
        _merge_kernel,
        grid=(m // TM, nj, 3),
        in_specs=[pl.BlockSpec((1, TM, k), lambda i, j, r: (r, i, 0)),
                  pl.BlockSpec((1, k, tn), lambda i, j, r: (r, 0, j)),
                  pl.BlockSpec((TM, tn), lambda i, j, r: (i, gate0 + r * nj + j))],
        out_specs=pl.BlockSpec((TM, tn), lambda i, j, r: (i, j)),
        out_shape=jax.ShapeDtypeStruct((m, n), BF16),
        scratch_shapes=[pltpu.VMEM((TM, tn), F32)],
        compiler_params=_cparams(("parallel", "parallel", "arbitrary")),
        name="merge_branches",
    )(o_abc, w_br, proj)


def _mm_res_kernel(a_ref, b_ref, h_ref, o_ref):
    o_ref[...] = h_ref[...] + jnp.dot(a_ref[...], b_ref[...], preferred_element_type=F32)


def matmul_residual(a, b, h, tn=1024):
    m, k = a.shape
    n = b.shape[1]
    return pl.pallas_call(
        _mm_res_kernel,
        grid=(m // TM, n // tn),
        in_specs=[pl.BlockSpec((TM, k), lambda i, j: (i, 0)), pl.BlockSpec((k, tn), lambda i, j: (0, j)),
                  pl.BlockSpec((TM, tn), lambda i, j: (i, j))],
        out_specs=pl.BlockSpec((TM, tn), lambda i, j: (i, j)),
        out_shape=jax.ShapeDtypeStruct((m, n), F32),
        compiler_params=_cparams(("parallel", "parallel")),
        name="matmul_out",
    )(a, b, h)


def _gateup_kernel(u_ref, wg_ref, wu_ref, o_ref):
    u = u_ref[...]
    g = jnp.dot(u, wg_ref[...], preferred_element_type=F32)
    up = jnp.dot(u, wu_ref[...], preferred_element_type=F32)
    o_ref[...] = (jax.nn.silu(g) * up).astype(o_ref.dtype)


def ffn_gate_up(u, wg, wu, tn=512):
    m, k = u.shape
    n = wg.shape[1]
    return pl.pallas_call(
        _gateup_kernel,
        grid=(m // TM, n // tn),
        in_specs=[pl.BlockSpec((TM, k), lambda i, j: (i, 0)), pl.BlockSpec((k, tn), lambda i, j: (0, j)),
                  pl.BlockSpec((k, tn), lambda i, j: (0, j))],
        out_specs=pl.BlockSpec((TM, tn), lambda i, j: (i, j)),
        out_shape=jax.ShapeDtypeStruct((m, n), BF16),
        compiler_params=_cparams(("parallel", "parallel")),
        name="ffn_gate_up",
    )(u, wg, wu)


def _down_kernel(a_ref, b_ref, h_ref, o_ref, acc_ref):
    kk = pl.program_id(2)
    y = jnp.dot(a_ref[...], b_ref[...], preferred_element_type=F32)

    @pl.when(kk == 0)
    def _():
        acc_ref[...] = h_ref[...] + y

    @pl.when(kk != 0)
    def _():
        acc_ref[...] += y

    @pl.when(kk == pl.num_programs(2) - 1)
    def _():
        o_ref[...] = acc_ref[...]


def ffn_down(a, b, h, tn=1024, tk=2816):
    m, k = a.shape
    n = b.shape[1]
    return pl.pallas_call(
        _down_kernel,
        grid=(m // TM, n // tn, k // tk),
        in_specs=[pl.BlockSpec((TM, tk), lambda i, j, kk: (i, kk)), pl.BlockSpec((tk, tn), lambda i, j, kk: (kk, j)),
                  pl.BlockSpec((TM, tn), lambda i, j, kk: (i, j))],
        out_specs=pl.BlockSpec((TM, tn), lambda i, j, kk: (i, j)),
        out_shape=jax.ShapeDtypeStruct((m, n), F32),
        scratch_shapes=[pltpu.VMEM((TM, tn), F32)],
        compiler_params=_cparams(("parallel", "parallel", "arbitrary")),
        name="ffn_down",
    )(a, b, h)


RET_HEADS = 4


def _retention_kernel(q_ref, k_ref, v_ref, g_ref, cos_ref, sin_ref, idec_ref, qdec_ref, kdec_ref, cdec_ref, gn_ref,
                      s0_ref, o_ref, sfin_ref, s_ref, *, tb, c):
    tblk = pl.program_id(2)
    heads = range(RET_HEADS)

    @pl.when(tblk == 0)
    def _():
        s_ref[...] = s0_ref[0]

    def rotate(x, cos, sin):
        return x * cos + pltpu.roll(x, DK_A // 2, 1) * sin

    def chunk_step(ci, carry):
        c0 = pl.multiple_of(ci * c, c)
        cos = cos_ref[pl.ds(c0, c), :]
        sin = sin_ref[pl.ds(c0, c), :]
        qr = [rotate(q_ref[pl.ds(c0, c), h * DK_A:(h + 1) * DK_A], cos, sin) for h in heads]
        kr = [rotate(k_ref[pl.ds(c0, c), h * DK_A:(h + 1) * DK_A], cos, sin) * DK_A ** -0.5 for h in heads]
        v = [v_ref[pl.ds(c0, c), h * DV_A:(h + 1) * DV_A].astype(BF16) for h in heads]
        s = [s_ref[h] for h in heads]
        att = [_dot_nt(qr[h].astype(BF16), kr[h].astype(BF16)) * idec_ref[h] for h in heads]
        o = [_bdot(att[h], v[h]) + _bdot(qr[h] * qdec_ref[h], s[h]) for h in heads]
        for h in heads:
            s_ref[h] = s[h] * cdec_ref[h, 0:1, :] + lax.dot_general(
                (kr[h] * kdec_ref[h]).astype(BF16), v[h], (((0,), (0,)), ((), ())), preferred_element_type=F32)
        for h in heads:
            mu = jnp.mean(o[h], axis=-1, keepdims=True)
            d = o[h] - mu
            var = jnp.mean(d * d, axis=-1, keepdims=True)
            cols = slice(h * DV_A, (h + 1) * DV_A)
            g = g_ref[pl.ds(c0, c), cols]
            o_ref[pl.ds(c0, c), cols] = (
                g * jax.nn.sigmoid(g) * (d * lax.rsqrt(var + EPS) * gn_ref[:, cols])).astype(o_ref.dtype)
        return carry

    lax.fori_loop(0, tb // c, chunk_step, 0)

    @pl.when(tblk == pl.num_programs(2) - 1)
    def _():
        sfin_ref[0] = s_ref[...]


def retention(proj, row0, b, t, p_len, s_ret, ret_gn):
    c = 2 * CHUNK if t % (2 * CHUNK) == 0 else t
    tb = 512 if t % 512 == 0 else t
    assert tb % c == 0 and row0 % tb == 0
    nt = t // tb
    rb = row0 // tb
    half = DK_A // 2
    inv = ROPE_BASE ** (-jnp.arange(half, dtype=F32) / half)
    ang = (p_len + jnp.arange(t)).astype(F32)[:, None] * inv[None, :]
    cos2 = jnp.concatenate([jnp.cos(ang), jnp.cos(ang)], axis=1)
    sin2 = jnp.concatenate([-jnp.sin(ang), jnp.sin(ang)], axis=1)
    log_gamma = jnp.log(1.0 - 2.0 ** (-5.0 - jnp.arange(H_A, dtype=F32)))
    i = jnp.arange(c, dtype=F32)
    rel = i[:, None] - i[None, :]
    idec = jnp.exp(jnp.where(rel[None] >= 0, rel[None] * log_gamma[:, None, None], -jnp.inf))
    qdec = jnp.broadcast_to(jnp.exp((i[None, :] + 1.0) * log_gamma[:, None])[:, :, None], (H_A, c, DK_A))
    kdec = jnp.broadcast_to(jnp.exp((c - 1.0 - i[None, :]) * log_gamma[:, None])[:, :, None], (H_A, c, DK_A))
    cdec = jnp.broadcast_to(jnp.exp(c * log_gamma)[:, None, None], (H_A, 8, DV_A))
    wk, wv = RET_HEADS * DK_A, RET_HEADS * DV_A
    qb0 = COL["q_a"][0] // wk
    kb0 = COL["k_a"][0] // wk
    vb0 = COL["v_a"][0] // wv
    gb0 = COL["g_a"][0] // wv
    sspec = pl.BlockSpec((1, RET_HEADS, DK_A, DV_A), lambda bi, h, k: (bi, h, 0, 0))
    return pl.pallas_call(
        partial(_retention_kernel, tb=tb, c=c),
        grid=(b, H_A // RET_HEADS, nt),
        in_specs=[pl.BlockSpec((tb, wk), lambda bi, h, k: (rb + bi * nt + k, qb0 + h)),
                  pl.BlockSpec((tb, wk), lambda bi, h, k: (rb + bi * nt + k, kb0 + h)),
                  pl.BlockSpec((tb, wv), lambda bi, h, k: (rb + bi * nt + k, vb0 + h)),
                  pl.BlockSpec((tb, wv), lambda bi, h, k: (rb + bi * nt + k, gb0 + h)),
                  pl.BlockSpec((tb, DK_A), lambda bi, h, k: (k, 0)),
                  pl.BlockSpec((tb, DK_A), lambda bi, h, k: (k, 0)),
                  pl.BlockSpec((RET_HEADS, c, c), lambda bi, h, k: (h, 0, 0)),
                  pl.BlockSpec((RET_HEADS, c, DK_A), lambda bi, h, k: (h, 0, 0)),
                  pl.BlockSpec((RET_HEADS, c, DK_A), lambda bi, h, k: (h, 0, 0)),
                  pl.BlockSpec((RET_HEADS, 8, DV_A), lambda bi, h, k: (h, 0, 0)),
                  pl.BlockSpec((1, wv), lambda bi, h, k: (0, h)),
                  sspec],
        out_specs=[pl.BlockSpec((tb, wv), lambda bi, h, k: (bi * nt + k, h)), sspec],
        out_shape=[jax.ShapeDtypeStruct((b * t, W_A), BF16),
                   jax.ShapeDtypeStruct((b, H_A, DK_A, DV_A), F32)],
        scratch_shapes=[pltpu.VMEM((RET_HEADS, DK_A, DV_A), F32)],
        compiler_params=_cparams(("parallel", "parallel", "arbitrary")),
        name="retention",
    )(proj, proj, proj, proj, cos2, sin2, idec, qdec, kdec, cdec, ret_gn.reshape(1, W_A), s_ret)


DELTA_HEADS = 8
DELTA_W = DELTA_HEADS * DK_B
HIST_ROWS = 8
INV_BASE = 16


def _bdot(a, b):
    return jnp.dot(a.astype(BF16), b.astype(BF16), preferred_element_type=F32)


def _hdot(a, b):
    return jnp.dot(a, b, preferred_element_type=F32, precision=lax.Precision.HIGHEST)


def _split_bf16(x):
    hi = x.astype(BF16)
    return hi, (x - hi.astype(F32)).astype(BF16)


def _dot3(a, b):
    a_hi, a_lo = _split_bf16(a)
    b_hi, b_lo = _split_bf16(b)
    dot = partial(jnp.dot, preferred_element_type=F32)
    return dot(a_hi, b_hi) + (dot(a_hi, b_lo) + dot(a_lo, b_hi))


def _unit_lower_inverse(mats, c):
    ii = lax.broadcasted_iota(jnp.int32, (c, c), 0)
    jj = lax.broadcasted_iota(jnp.int32, (c, c), 1)
    eye = jnp.where(ii == jj, 1.0, 0.0).astype(F32)
    ps = [jnp.where((ii // INV_BASE) == (jj // INV_BASE), a, 0.0) for a in mats]
    ts = [eye - p for p in ps]
    for _ in range(int(math.log2(INV_BASE)) - 1):
        ps = [_dot3(p, p) for p in ps]
        ts = [_dot3(t, eye + p) for t, p in zip(ts, ps)]
    size = INV_BASE
    while size < c:
        half_blocks = ((ii // (2 * size)) == (jj // (2 * size))) & ((ii // size) != (jj // size))
        tl = [_dot3(t, jnp.where(half_blocks, a, 0.0)) for t, a in zip(ts, mats)]
        ts = [t - _dot3(x, t) for t, x in zip(ts, tl)]
        size *= 2
    return ts


def _delta_kernel(xq_ref, xk_ref, xv_ref, z_ref, small_ref, cwq_ref, cwk_ref, cwv_ref, hq_ref, hk_ref, hv_ref,
                  par_ref, dn_ref, s0_ref, o_ref, sfin_ref,
                  bq_ref, bk_ref, bv_ref, qn_ref, kn_ref, vv_ref, g_ref, beta_ref, s_ref, *, tb, c):
    hg = pl.program_id(1)
    tblk = pl.program_id(2)

    @pl.when(tblk == 0)
    def _():
        bq_ref[0:HIST_ROWS, :] = hq_ref[0]
        bk_ref[0:HIST_ROWS, :] = hk_ref[0]
        bv_ref[0:HIST_ROWS, :] = hv_ref[0]
        s_ref[...] = s0_ref[0]

    def conv_silu(x_ref, buf_ref, cw_ref):
        buf_ref[HIST_ROWS:HIST_ROWS + tb, :] = x_ref[...]
        y = jnp.zeros((tb, DELTA_W), F32)
        for j in range(CONV_W):
            lo = HIST_ROWS - (CONV_W - 1) + j
            y = y + buf_ref[lo:lo + tb, :] * cw_ref[j:j + 1, :]
        buf_ref[0:HIST_ROWS, :] = buf_ref[tb:tb + HIST_ROWS, :]
        return y * jax.nn.sigmoid(y)

    def l2n(x):
        parts = []
        for h in range(DELTA_HEADS):
            xh = x[:, h * DK_B:(h + 1) * DK_B]
            parts.append(xh * lax.rsqrt(jnp.sum(xh * xh, axis=-1, keepdims=True) + EPS))
        return jnp.concatenate(parts, axis=1)

    qn_ref[...] = l2n(conv_silu(xq_ref, bq_ref, cwq_ref)) * DK_B ** -0.5
    kn_ref[...] = l2n(conv_silu(xk_ref, bk_ref, cwk_ref))
    vv_ref[...] = conv_silu(xv_ref, bv_ref, cwv_ref)
    sm = small_ref[...]
    x = sm + par_ref[1:2, :]
    softplus = jnp.maximum(x, 0.0) + jnp.log1p(jnp.exp(-jnp.abs(x)))
    g_ref[...] = -jnp.exp(par_ref[0:1, :]) * softplus
    beta_ref[...] = jax.nn.sigmoid(sm)

    ii = lax.broadcasted_iota(jnp.int32, (c, c), 0)
    jj = lax.broadcasted_iota(jnp.int32, (c, c), 1)
    incl = ii >= jj
    strict = ii > jj
    diag = ii == jj
    tril_ones = jnp.where(incl, 1.0, 0.0).astype(F32)
    a_lane = COL["a_b"][0] - COL["k_i"][0]
    b_lane = COL["b_b"][0] - COL["k_i"][0]

    def chunk_step(ci, carry):
        c0 = pl.multiple_of(ci * c, c)
        gcum_all = _hdot(tril_ones, g_ref[pl.ds(c0, c), :])
        beta_all = beta_ref[pl.ds(c0, c), :]
        lane = lax.broadcasted_iota(jnp.int32, (c, LANE), 1)
        heads = range(DELTA_HEADS)
        cols = [slice(h * DK_B, (h + 1) * DK_B) for h in heads]
        gc = [jnp.sum(jnp.where(lane == a_lane + hg * DELTA_HEADS + h, gcum_all, 0.0), axis=-1, keepdims=True)
              for h in heads]
        bc = [jnp.sum(jnp.where(lane == b_lane + hg * DELTA_HEADS + h, beta_all, 0.0), axis=-1, keepdims=True)
              for h in heads]
        gr = [jnp.sum(jnp.where(diag, jnp.broadcast_to(g, (c, c)), 0.0), axis=0, keepdims=True) for g in gc]
        g_last = [jnp.sum(jnp.where(ii[:, 0:1] == c - 1, g, 0.0), axis=0, keepdims=True) for g in gc]
        gam = [jnp.exp(jnp.where(incl, g - r, -jnp.inf)) for g, r in zip(gc, gr)]
        eg = [jnp.exp(g) for g in gc]
        qh = [qn_ref[pl.ds(c0, c), cs] for cs in cols]
        kh = [kn_ref[pl.ds(c0, c), cs] for cs in cols]
        vh = [vv_ref[pl.ds(c0, c), cs] for cs in cols]
        kb = [k * b for k, b in zip(kh, bc)]
        a = [jnp.where(strict, _dot_nt(x.astype(BF16), k.astype(BF16)) * gm, 0.0) for x, k, gm in zip(kb, kh, gam)]
        qk = [_dot_nt(q.astype(BF16), k.astype(BF16)) * gm for q, k, gm in zip(qh, kh, gam)]
        tinv = _unit_lower_inverse(a, c)
        u = [_bdot(t, v * b) for t, v, b in zip(tinv, vh, bc)]
        w = [_bdot(t, x * e) for t, x, e in zip(tinv, kb, eg)]
        s = [s_ref[h] for h in heads]
        v_new = [x - _bdot(y, st) for x, y, st in zip(u, w, s)]
        o = [_bdot(q * e, st) + _bdot(m, vn) for q, e, st, m, vn in zip(qh, eg, s, qk, v_new)]
        for h in heads:
            kd = (kh[h] * jnp.exp(g_last[h] - gc[h])).astype(BF16)
            s_ref[h] = s[h] * jnp.exp(g_last[h]) + lax.dot_general(
                kd, v_new[h].astype(BF16), (((0,), (0,)), ((), ())), preferred_element_type=F32)
        for h in heads:
            on = o[h] * lax.rsqrt(jnp.mean(o[h] * o[h], axis=-1, keepdims=True) + EPS) * dn_ref[...]
            zh = z_ref[pl.ds(c0, c), cols[h]]
            o_ref[pl.ds(c0, c), cols[h]] = (on * (zh * jax.nn.sigmoid(zh))).astype(o_ref.dtype)
        return carry

    lax.fori_loop(0, tb // c, chunk_step, 0)

    @pl.when(tblk == pl.num_programs(2) - 1)
    def _():
        sfin_ref[0] = s_ref[...]


def gated_delta(proj, row0, b, t, s_conv, s_delta, conv_w, a_log, dt_bias, d_norm):
    c = CHUNK if t % CHUNK == 0 else t
    tb = 512 if t % 512 == 0 else t
    assert tb % c == 0 and c % INV_BASE == 0 and tb >= HIST_ROWS and row0 % tb == 0
    nt = t // tb
    rb = row0 // tb
    ngrp = H_B // DELTA_HEADS
    qcol = COL["qkv_b"][0] // DELTA_W
    hist = jnp.concatenate([jnp.zeros((b, HIST_ROWS - (CONV_W - 1), CONV_CH), F32), s_conv], axis=1)
    par = jnp.zeros((8, LANE), F32)
    a_lane = COL["a_b"][0] - COL["k_i"][0]
    par = par.at[0, a_lane:a_lane + H_B].set(a_log).at[1, a_lane:a_lane + H_B].set(dt_bias)

    def xspec(part):
        return pl.BlockSpec((tb, DELTA_W), lambda bi, g, k: (rb + bi * nt + k, qcol + part * ngrp + g))

    def wspec(part):
        return pl.BlockSpec((CONV_W, DELTA_W), lambda bi, g, k: (0, part * ngrp + g))

    def hspec(part):
        return pl.BlockSpec((1, HIST_ROWS, DELTA_W), lambda bi, g, k: (bi, 0, part * ngrp + g))

    sspec = pl.BlockSpec((1, DELTA_HEADS, DK_B, DV_B), lambda bi, g, k: (bi, g, 0, 0))
    return pl.pallas_call(
        partial(_delta_kernel, tb=tb, c=c),
        grid=(b, ngrp, nt),
        in_specs=[xspec(0), xspec(1), xspec(2),
                  pl.BlockSpec((tb, DELTA_W), lambda bi, g, k: (rb + bi * nt + k, COL["z_b"][0] // DELTA_W + g)),
                  pl.BlockSpec((tb, LANE), lambda bi, g, k: (rb + bi * nt + k, COL["k_i"][0] // LANE)),
                  wspec(0), wspec(1), wspec(2), hspec(0), hspec(1), hspec(2),
                  pl.BlockSpec((8, LANE), lambda bi, g, k: (0, 0)),
                  pl.BlockSpec((1, DV_B), lambda bi, g, k: (0, 0)),
                  sspec],
        out_specs=[pl.BlockSpec((tb, DELTA_W), lambda bi, g, k: (bi * nt + k, g)), sspec],
        out_shape=[jax.ShapeDtypeStruct((b * t, W_B), BF16),
                   jax.ShapeDtypeStruct((b, H_B, DK_B, DV_B), F32)],
        scratch_shapes=[pltpu.VMEM((tb + HIST_ROWS, DELTA_W), F32)] * 3
        + [pltpu.VMEM((tb, DELTA_W), F32)] * 3
        + [pltpu.VMEM((tb, LANE), F32)] * 2
        + [pltpu.VMEM((DELTA_HEADS, DK_B, DV_B), F32)],
        compiler_params=_cparams(("parallel", "parallel", "arbitrary")),
        name="gated_delta",
    )(proj, proj, proj, proj, proj, conv_w, conv_w, conv_w, hist, hist, hist, par, d_norm.reshape(1, DV_B), s_delta)


def t5_bucket(rel):
    half = NUM_BUCKETS // 2
    exact = half // 2
    n = jnp.abs(rel)
    large = exact + (jnp.log(jnp.maximum(n, 1).astype(jnp.float32) / exact)
                     / math.log(MAX_DISTANCE / exact) * (half - exact)).astype(jnp.int32)
    large = jnp.minimum(large, half - 1)
    return jnp.where(rel > 0, half, 0) + jnp.where(n < exact, n, large)


KEY_TILE = LANE
SCORE_TILE = 2 * KEY_TILE
N_BIAS_NEAR = 6
MASKED = -1e30
INT32_MIN = -2 ** 31
GROUP = H_C // H_KV


def _dot_nt(a, b):
    return lax.dot_general(a, b, (((1,), (1,)), ((), ())), preferred_element_type=F32)


def _dsa_kernel(qc_ref, qi_ref, small_ref, k_ref, v_ref, kidx_ref, bias_ref, o_ref,
                key_ref, qis_ref, wb_ref, qg_ref, acc_ref, m_ref, l_ref, *, tq, q0_tile, seq_len, topk):
    qt = q0_tile + pl.program_id(1)
    n_vis = qt + 1
    row = lax.broadcasted_iota(jnp.int32, (tq, SCORE_TILE), 0)
    lane = lax.broadcasted_iota(jnp.int32, (tq, SCORE_TILE), 1)
    q_chunk = (qt * KEY_TILE + row) // CHUNK
    pad_rows = KEY_TILE - tq

    qi = (qi_ref[...] * D_IDX ** -0.5).astype(BF16)
    w = small_ref[:, D_IDX:D_IDX + H_IDX] * H_IDX ** -0.5
    for i in range(H_IDX // 2):
        qis_ref[i * tq:(i + 1) * tq, :] = qi[:, i * LANE:(i + 1) * LANE]
        for j in range(2):
            wb_ref[i * tq:(i + 1) * tq, j * SCORE_TILE:(j + 1) * SCORE_TILE] = jnp.broadcast_to(
                w[:, 2 * i + j:2 * i + j + 1], (tq, SCORE_TILE))
    first_head = lax.broadcasted_iota(jnp.int32, (SCORE_TILE, LANE), 1) < D_IDX

    def score_tile(u, carry):
        start = pl.multiple_of(u * SCORE_TILE, SCORE_TILE)
        kd = kidx_ref[0, pl.ds(start, SCORE_TILE), :]
        zero = jnp.zeros_like(kd)
        kt2 = jnp.concatenate([jnp.where(first_head, kd, zero), jnp.where(first_head, zero, kd)], axis=0)
        s = _dot_nt(qis_ref[...], kt2)
        r = jnp.maximum(s, 0.0) * wb_ref[...]
        r = jnp.sum(r.reshape(H_IDX // 2, tq, 2 * SCORE_TILE), axis=0)
        score = r[:, :SCORE_TILE] + r[:, SCORE_TILE:] + 0.0
        bits = pltpu.bitcast(score, jnp.int32)
        key = jnp.where(bits < 0, bits ^ jnp.int32(0x7FFFFFFF), bits)
        kpos = start + lane
        visible = ((kpos // CHUNK) <= q_chunk) & (kpos < seq_len)
        key = jnp.where(visible, key, jnp.int32(INT32_MIN))
        if pad_rows:
            key = jnp.concatenate([key, jnp.full((pad_rows, SCORE_TILE), INT32_MIN, jnp.int32)], axis=0)
        for j in range(SCORE_TILE // KEY_TILE):
            key_ref[u * (SCORE_TILE // KEY_TILE) + j] = key[:, j * KEY_TILE:(j + 1) * KEY_TILE].T
        return carry

    lax.fori_loop(0, pl.cdiv(n_vis * KEY_TILE, SCORE_TILE), score_tile, 0)

    def count_keys(pred):
        def count_tile(t, cnt):
            kpos = t * KEY_TILE + lax.broadcasted_iota(jnp.int32, (KEY_TILE, KEY_TILE), 0)
            return cnt + jnp.where(pred(key_ref[t], kpos), 1.0, 0.0)

        cnt = lax.fori_loop(0, n_vis, count_tile, jnp.zeros((KEY_TILE, KEY_TILE), F32))
        return jnp.sum(cnt, axis=0, keepdims=True)

    thr = jnp.full((1, KEY_TILE), INT32_MIN, jnp.int32)
    for bit in range(31, -1, -1):
        cand = thr + jnp.int32(-2 ** 31 if bit == 31 else 2 ** bit)
        thr = jnp.where(count_keys(lambda key, kpos, cand=cand: key >= cand) >= float(topk), cand, thr)
    thr = jnp.maximum(thr, jnp.int32(INT32_MIN + 1))

    n_above = count_keys(lambda key, kpos: key > thr)
    n_tied = count_keys(lambda key, kpos: key == thr)
    ties_wanted = float(topk) - n_above
    pos_bits = (key_ref.shape[0] * KEY_TILE - 1).bit_length()

    def find_last_tie():
        last = jnp.zeros((1, KEY_TILE), jnp.int32)
        for bit in range(pos_bits - 1, -1, -1):
            cand = last | jnp.int32(2 ** bit)
            before = count_keys(lambda key, kpos, cand=cand: (key == thr) & (kpos < cand))
            last = jnp.where(before < ties_wanted, cand, last)
        return last

    surplus = jnp.max(n_tied - ties_wanted) > 0.0
    last_tie = lax.cond(surplus, find_last_tie, lambda: jnp.full((1, KEY_TILE), 2 ** pos_bits - 1, jnp.int32))
    thr_g = jnp.concatenate([thr] * GROUP, axis=1)
    last_tie_g = jnp.concatenate([last_tie] * GROUP, axis=1)

    kvs = range(H_KV)
    for kv in kvs:
        for g in range(GROUP):
            c0 = (kv * GROUP + g) * DH_C
            qh = (qc_ref[:, c0:c0 + DH_C] * DH_C ** -0.5).astype(BF16)
            if pad_rows:
                qh = jnp.concatenate([qh, jnp.zeros((pad_rows, DH_C), BF16)], axis=0)
            qg_ref[kv, g * KEY_TILE:(g + 1) * KEY_TILE, :] = qh
    acc_ref[...] = jnp.zeros(acc_ref.shape, F32)
    m_ref[...] = jnp.full(m_ref.shape, MASKED, F32)
    l_ref[...] = jnp.zeros(l_ref.shape, F32)

    def attend_tile(t, carry):
        start = pl.multiple_of(t * KEY_TILE, KEY_TILE)
        key_g = jnp.concatenate([key_ref[t]] * GROUP, axis=1)
        kpos = start + lax.broadcasted_iota(jnp.int32, (KEY_TILE, GROUP * KEY_TILE), 0)
        sel = (key_g > thr_g) | ((key_g == thr_g) & (kpos <= last_tie_g))
        dd = jnp.minimum(qt - t, N_BIAS_NEAR)
        s = [_dot_nt(k_ref[0, pl.ds(start, KEY_TILE), kv * DH_C:(kv + 1) * DH_C], qg_ref[kv]) for kv in kvs]
        sm = [jnp.where(sel, s[kv] + bias_ref[dd, kv], MASKED) for kv in kvs]
        m_old = [m_ref[kv:kv + 1, :] for kv in kvs]
        m_new = [jnp.maximum(m_old[kv], jnp.max(sm[kv], axis=0, keepdims=True)) for kv in kvs]
        alpha = [jnp.exp(m_old[kv] - m_new[kv]) for kv in kvs]
        p = [jnp.exp(sm[kv] - m_new[kv]) for kv in kvs]
        pv = [lax.dot_general(v_ref[0, pl.ds(start, KEY_TILE), kv * DH_C:(kv + 1) * DH_C], p[kv].astype(BF16),
                              (((0,), (0,)), ((), ())), preferred_element_type=F32) for kv in kvs]
        for kv in kvs:
            l_ref[kv:kv + 1, :] = alpha[kv] * l_ref[kv:kv + 1, :] + jnp.sum(p[kv], axis=0, keepdims=True)
            acc_ref[kv] = alpha[kv] * acc_ref[kv] + pv[kv]
            m_ref[kv:kv + 1, :] = m_new[kv]
        return carry

    lax.fori_loop(0, n_vis, attend_tile, 0)
    for kv in kvs:
        out = (acc_ref[kv] / l_ref[kv:kv + 1, :]).T
        for g in range(GROUP):
            c0 = (kv * GROUP + g) * DH_C
            o_ref[:, c0:c0 + DH_C] = out[g * KEY_TILE:g * KEY_TILE + tq].astype(o_ref.dtype)


def dsa_bias_tiles(rel_bias):
    i = jnp.arange(KEY_TILE)
    d = jnp.arange(N_BIAS_NEAR + 1)
    rel = (i[None, :, None] - i[None, None, :]) - KEY_TILE * d[:, None, None]
    onehot = (t5_bucket(rel)[..., None] == jnp.arange(NUM_BUCKETS)).astype(F32)
    tiles = jnp.einsum('dkqn,nh->dkqh', onehot, rel_bias.astype(F32), precision=lax.Precision.HIGHEST)
    tiles = tiles.reshape(N_BIAS_NEAR + 1, KEY_TILE, KEY_TILE, H_KV, GROUP)
    return jnp.transpose(tiles, (0, 3, 1, 4, 2)).reshape(N_BIAS_NEAR + 1, H_KV, KEY_TILE, GROUP * KEY_TILE)


def dsa_attention(proj, row0, b, t, k_all, v_all, kidx_all, bias_tiles, p_len, seq_len, topk):
    tq = KEY_TILE if t % KEY_TILE == 0 else t
    assert p_len % KEY_TILE == 0 and KEY_TILE % tq == 0 and (tq == KEY_TILE or t == tq) and row0 % tq == 0
    nb = t // tq
    lp = k_all.shape[1]
    assert lp % SCORE_TILE == 0 and lp >= p_len + nb * KEY_TILE
    assert 8 * (MAX_DISTANCE / 8) ** (7 / 8) < KEY_TILE * N_BIAS_NEAR - (KEY_TILE - 1)
    rb = row0 // tq
    kern = partial(_dsa_kernel, tq=tq, q0_tile=p_len // KEY_TILE, seq_len=seq_len, topk=topk)
    return pl.pallas_call(
        kern,
        grid=(b, nb),
        in_specs=[pl.BlockSpec((tq, W_C), lambda bi, j: (rb + bi * nb + j, COL["q_c"][0] // W_C)),
                  pl.BlockSpec((tq, H_IDX * D_IDX), lambda bi, j: (rb + bi * nb + j, COL["q_i"][0] // (H_IDX * D_IDX))),
                  pl.BlockSpec((tq, LANE), lambda bi, j: (rb + bi * nb + j, COL["k_i"][0] // LANE)),
                  pl.BlockSpec((1, lp, H_KV * DH_C), lambda bi, j: (bi, 0, 0)),
                  pl.BlockSpec((1, lp, H_KV * DH_C), lambda bi, j: (bi, 0, 0)),
                  pl.BlockSpec((1, lp, 2 * D_IDX), lambda bi, j: (bi, 0, 0)),
                  pl.BlockSpec((N_BIAS_NEAR + 1, H_KV, KEY_TILE, GROUP * KEY_TILE), lambda bi, j: (0, 0, 0, 0))],
        out_specs=pl.BlockSpec((tq, W_C), lambda bi, j: (bi * nb + j, 0)),
        out_shape=jax.ShapeDtypeStruct((b * t, W_C), BF16),
        scratch_shapes=[pltpu.VMEM((lp // KEY_TILE, KEY_TILE, KEY_TILE), jnp.int32),
                        pltpu.VMEM((H_IDX // 2 * tq, 2 * D_IDX), BF16),
                        pltpu.VMEM((H_IDX // 2 * tq, 2 * SCORE_TILE), F32),
                        pltpu.VMEM((H_KV, GROUP * KEY_TILE, DH_C), BF16),
                        pltpu.VMEM((H_KV, DH_C, GROUP * KEY_TILE), F32),
                        pltpu.VMEM((8, GROUP * KEY_TILE), F32),
                        pltpu.VMEM((8, GROUP * KEY_TILE), F32)],
        compiler_params=_cparams(("parallel", "arbitrary")),
        name="dsa_attention",
    )(proj, proj, proj, k_all, v_all, kidx_all, bias_tiles)


def branches(proj_all, row0, b, t, past_k, past_v, past_kidx, s_ret, s_delta, s_conv,
             ret_gn, conv_w, a_log, dt_bias, d_norm, bias_tiles):
    p_len = past_k.shape[1]
    seq_len = p_len + t
    topk = min(TOPK_MAX, seq_len // 4)
    lp = -(-seq_len // SCORE_TILE) * SCORE_TILE

    def new_cols(name):
        o, w = COL[name]
        return proj_all[row0:row0 + b * t, o:o + w].reshape(b, t, w)

    k_c, v_c, k_i = new_cols("k_c"), new_cols("v_c"), new_cols("k_i")
    qkv_tail = new_cols("qkv_b")[:, -(CONV_W - 1):]
    conv_new = jnp.concatenate([s_conv, qkv_tail], axis=1)[:, -(CONV_W - 1):]

    o_a, ret_new = retention(proj_all, row0, b, t, p_len, s_ret, ret_gn)
    o_b, delta_new = gated_delta(proj_all, row0, b, t, s_conv, s_delta, conv_w, a_log, dt_bias, d_norm)

    def all_keys(past, new):
        w = new.shape[-1]
        return jnp.concatenate([past.reshape(b, p_len, w).astype(BF16), new.astype(BF16),
                                jnp.zeros((b, lp - seq_len, w), BF16)], axis=1)

    o_c = dsa_attention(proj_all, row0, b, t, all_keys(past_k, k_c), all_keys(past_v, v_c),
                        jnp.tile(all_keys(past_kidx, k_i), (1, 1, 2)), bias_tiles, p_len, seq_len, topk)
    caches = (k_c.reshape(b, t, H_KV, DH_C), v_c.reshape(b, t, H_KV, DH_C), k_i, ret_new, delta_new, conv_new)
    return jnp.stack([o_a, o_b, o_c]), caches


def _prep_w_in(w):
    return jnp.concatenate([w[:, 0:14336], w[:, 14368:16416], w[:, 17440:19488], w[:, 16416:17440],
                            w[:, 19584:31872], w[:, 19488:19584], w[:, 14336:14368],
                            jnp.zeros((w.shape[0], D_IN_PAD - D_IN), w.dtype)], axis=1).astype(BF16)


def kernel(x_prompt, x_sample, cache_k, cache_v, cache_kidx, state_ret, state_delta, state_conv, norm_mix, w_in, ret_gn, conv_w, delta_a_log, delta_dt_bias, delta_norm, rel_bias, w_branch_a, w_branch_b, w_branch_c, w_out, norm_ffn, w_ffn_gate, w_ffn_up, w_ffn_down, norm_final):
    dt = x_prompt.dtype
    bp, tp, _ = x_prompt.shape
    bs, ts, _ = x_sample.shape
    depth = w_in.shape[0]
    n_p = bp * tp
    zk = jnp.zeros((bp, 0, H_KV, DH_C), dt)
    zkidx = jnp.zeros((bp, 0, D_IDX), dt)
    zret = jnp.zeros((bp, H_A, DK_A, DV_A), dt)
    zdelta = jnp.zeros((bp, H_B, DK_B, DV_B), dt)
    zconv = jnp.zeros((bp, CONV_W - 1, CONV_CH), dt)

    h = jnp.concatenate([x_prompt.reshape(n_p, D_MODEL), x_sample.reshape(bs * ts, D_MODEL)], axis=0)
    new_p, new_s = [], []
    bias_tiles = dsa_bias_tiles(rel_bias)
    for l in range(depth):
        u = rmsnorm(h, norm_mix[l], BF16)
        proj = matmul(u, _prep_w_in(w_in[l]), F32)
        bw = (ret_gn[l], conv_w[l], delta_a_log[l], delta_dt_bias[l], delta_norm[l], bias_tiles)
        o_p, sp = branches(proj, 0, bp, tp, zk, zk, zkidx, zret, zdelta, zconv, *bw)
        o_s, ss = branches(proj, n_p, bs, ts, cache_k[l], cache_v[l], cache_kidx[l],
                           state_ret[l], state_delta[l], state_conv[l], *bw)
        new_p.append(sp)
        new_s.append(ss)
        o_abc = jnp.concatenate([o_p, o_s], axis=1)
        w_br = jnp.stack([w_branch_a[l], w_branch_b[l], w_branch_c[l]]).astype(BF16)
        merged = merge_branches(o_abc, w_br, proj)
        h = matmul_residual(merged, w_out[l].astype(BF16), h)
        u2 = rmsnorm(h, norm_ffn[l], BF16)
        pad = D_FF_PAD - D_FF
        wg = jnp.concatenate([w_ffn_gate[l].astype(BF16), jnp.zeros((D_MODEL, pad), BF16)], axis=1)
        wu = jnp.concatenate([w_ffn_up[l].astype(BF16), jnp.zeros((D_MODEL, pad), BF16)], axis=1)
        wd = jnp.concatenate([w_ffn_down[l].astype(BF16), jnp.zeros((pad, D_MODEL), BF16)], axis=0)
        act = ffn_gate_up(u2, wg, wu)
        h = ffn_down(act, wd, h)

    y = rmsnorm(h, norm_final, F32)
    y_prompt = y[:n_p].reshape(bp, tp, D_MODEL)
    y_sample = y[n_p:].reshape(bs, ts, D_MODEL)
    k_p, v_p, kidx_p, ret_p, delta_p, conv_p = [jnp.stack([s[i] for s in new_p]) for i in range(6)]
    k_s, v_s, kidx_s, ret_s, delta_s, conv_s = [jnp.stack([s[i] for s in new_s]) for i in range(6)]
    return (y_prompt, y_sample, k_p, v_p, kidx_p, ret_p, delta_p, conv_p,
            k_s, v_s, kidx_s, ret_s, delta_s, conv_s)
```

```python
import math
from functools import partial

import jax
import jax.numpy as jnp
import numpy as np
from jax import lax
from jax.experimental import pallas as pl
from jax.experimental.pallas import tpu as pltpu

F32 = jnp.float32
BF16 = jnp.bfloat16

D_MODEL = 4096
CHUNK = 64
QUERY_BLOCK = 128
EPS = 1e-6
H_A, DK_A, DV_A = 8, 128, 256
W_A = H_A * DV_A
ROPE_BASE = 10000.0
H_B, DK_B, DV_B = 16, 128, 128
W_B = H_B * DV_B
CONV_W = 4
CONV_CH = 2 * H_B * DK_B + H_B * DV_B
H_C, H_KV, DH_C = 16, 4, 128
W_C = H_C * DH_C
H_IDX, D_IDX = 32, 64
TOPK_MAX = 256
NUM_BUCKETS = 32
MAX_DISTANCE = 1024
D_FF = 11008

LANE = 128
VMEM_LIMIT_V7X = 56 * 1024 * 1024

COL = {}
_off = 0
for _name, _w in [("q_a", 1024), ("k_a", 1024), ("v_a", 2048), ("g_a", 2048), ("qkv_b", 6144), ("z_b", 2048),
                  ("q_c", 2048), ("q_i", 2048), ("k_c", 512), ("v_c", 512),
                  ("gate_a", 4096), ("gate_b", 4096), ("gate_c", 4096),
                  ("k_i", 64), ("w_i", 32), ("a_b", 16), ("b_b", 16)]:
    COL[_name] = (_off, _w)
    _off += _w
D_IN = _off
D_IN_PAD = 32768
TM = 1024


def _cparams(sem):
    return pltpu.CompilerParams(dimension_semantics=sem, vmem_limit_bytes=VMEM_LIMIT_V7X)


def _rmsnorm_kernel(x_ref, g_ref, o_ref):
    x = x_ref[...]
    y = x * lax.rsqrt(jnp.mean(x * x, axis=-1, keepdims=True) + EPS)
    o_ref[...] = (y * g_ref[...]).astype(o_ref.dtype)


def rmsnorm(x, g, out_dtype, tr=512):
    m, d = x.shape
    return pl.pallas_call(
        _rmsnorm_kernel,
        grid=(m // tr,),
        in_specs=[pl.BlockSpec((tr, d), lambda i: (i, 0)), pl.BlockSpec((1, d), lambda i: (0, 0))],
        out_specs=pl.BlockSpec((tr, d), lambda i: (i, 0)),
        out_shape=jax.ShapeDtypeStruct((m, d), out_dtype),
        compiler_params=_cparams(("parallel",)),
        name="rmsnorm",
    )(x, g.reshape(1, d))


KV_W = H_KV * DH_C


def _mm_in_kernel(a_ref, b_ref, o_ref, k_ref, v_ref, ki_ref, kb_ref, vb_ref, kib_ref, *, kv_block, small_block):
    j = pl.program_id(1)
    o_ref[...] = jnp.dot(a_ref[...], b_ref[...], preferred_element_type=F32)

    @pl.when(j == kv_block)
    def _():
        k, v = o_ref[:, :KV_W], o_ref[:, KV_W:2 * KV_W]
        k_ref[...] = k
        v_ref[...] = v
        kb_ref[...] = k.astype(BF16)
        vb_ref[...] = v.astype(BF16)

    @pl.when(j == small_block)
    def _():
        ki = o_ref[:, :D_IDX]
        ki_ref[...] = ki
        kib_ref[...] = jnp.concatenate([ki, ki], axis=1).astype(BF16)


def matmul_in(a, b, tn=1024):
    m, k = a.shape
    n = b.shape[1]
    assert COL["k_c"][0] % tn == 0 and COL["v_c"][0] == COL["k_c"][0] + KV_W and COL["k_i"][0] % tn == 0
    kern = partial(_mm_in_kernel, kv_block=COL["k_c"][0] // tn, small_block=COL["k_i"][0] // tn)

    def row_spec(w):
        return pl.BlockSpec((TM, w), lambda i, j: (i, 0))

    return pl.pallas_call(
        kern,
        grid=(m // TM, n // tn),
        in_specs=[pl.BlockSpec((TM, k), lambda i, j: (i, 0), pipeline_mode=pl.Buffered(1)),
                  pl.BlockSpec((k, tn), lambda i, j: (0, j))],
        out_specs=[pl.BlockSpec((TM, tn), lambda i, j: (i, j)), row_spec(KV_W), row_spec(KV_W), row_spec(D_IDX),
                   row_spec(KV_W), row_spec(KV_W), row_spec(2 * D_IDX)],
        out_shape=[jax.ShapeDtypeStruct((m, n), F32), jax.ShapeDtypeStruct((m, KV_W), F32),
                   jax.ShapeDtypeStruct((m, KV_W), F32), jax.ShapeDtypeStruct((m, D_IDX), F32),
                   jax.ShapeDtypeStruct((m, KV_W), BF16), jax.ShapeDtypeStruct((m, KV_W), BF16),
                   jax.ShapeDtypeStruct((m, 2 * D_IDX), BF16)],
        compiler_params=_cparams(("parallel", "arbitrary")),
        name="matmul_in",
    )(a, b)


def _merge_kernel(o_ref, w_ref, g_ref, out_ref, acc_ref):
    br = pl.program_id(2)
    y = jnp.dot(o_ref[0], w_ref[0], preferred_element_type=F32) * jax.nn.sigmoid(g_ref[...])

    @pl.when(br == 0)
    def _():
        acc_ref[...] = y

    @pl.when(br != 0)
    def _():
        acc_ref[...] += y

    @pl.when(br == pl.num_programs(2) - 1)
    def _():
        out_ref[...] = acc_ref[...].astype(out_ref.dtype)


def merge_branches(o_abc, w_br, proj, tn=1024):
    _, m, k = o_abc.shape
    n = w_br.shape[2]
    gate0 = COL["gate_a"][0] // tn
    nj = n // tn
    return pl.pallas_call(
        _merge_kernel,
        grid=(m // TM, nj, 3),
        in_specs=[pl.BlockSpec((1, TM, k), lambda i, j, r: (r, i, 0)),
                  pl.BlockSpec((1, k, tn), lambda i, j, r: (r, 0, j)),
                  pl.BlockSpec((TM, tn), lambda i, j, r: (i, gate0 + r * nj + j))],
        out_specs=pl.BlockSpec((TM, tn), lambda i, j, r: (i, j)),
        out_shape=jax.ShapeDtypeStruct((m, n), BF16),
        scratch_shapes=[pltpu.VMEM((TM, tn), F32)],
        compiler_params=_cparams(("parallel", "parallel", "arbitrary")),
        name="merge_branches",
    )(o_abc, w_br, proj)


def _mm_res_kernel(a_ref, b_ref, h_ref, o_ref):
    o_ref[...] = h_ref[...] + jnp.dot(a_ref[...], b_ref[...], preferred_element_type=F32)


def matmul_residual(a, b, h, tn=1024):
    m, k = a.shape
    n = b.shape[1]
    return pl.pallas_call(
        _mm_res_kernel,
        grid=(m // TM, n // tn),
        in_specs=[pl.BlockSpec((TM, k), lambda i, j: (i, 0)), pl.BlockSpec((k, tn), lambda i, j: (0, j)),
                  pl.BlockSpec((TM, tn), lambda i, j: (i, j))],
        out_specs=pl.BlockSpec((TM, tn), lambda i, j: (i, j)),
        out_shape=jax.ShapeDtypeStruct((m, n), F32),
        compiler_params=_cparams(("parallel", "parallel")),
        name="matmul_out",
    )(a, b, h)


def _gateup_kernel(u_ref, wg_ref, wu_ref, o_ref):
    u = u_ref[...]
    g = jnp.dot(u, wg_ref[...], preferred_element_type=F32)
    up = jnp.dot(u, wu_ref[...], preferred_element_type=F32)
    o_ref[...] = (jax.nn.silu(g) * up).astype(o_ref.dtype)


def ffn_gate_up(u, wg, wu, tn=512):
    m, k = u.shape
    tm = next(c for c in (1536, TM) if m % c == 0)
    n = wg.shape[1]
    return pl.pallas_call(
        _gateup_kernel,
        grid=(m // tm, pl.cdiv(n, tn)),
        in_specs=[pl.BlockSpec((tm, k), lambda i, j: (i, 0), pipeline_mode=pl.Buffered(1)),
                  pl.BlockSpec((k, tn), lambda i, j: (0, j)),
                  pl.BlockSpec((k, tn), lambda i, j: (0, j))],
        out_specs=pl.BlockSpec((tm, tn), lambda i, j: (i, j)),
        out_shape=jax.ShapeDtypeStruct((m, n), BF16),
        compiler_params=_cparams(("parallel", "parallel")),
        name="ffn_gate_up",
    )(u, wg, wu)


def ffn_down(a, b, h, tm=768, tn=512):
    m, k = a.shape
    n = b.shape[1]
    return pl.pallas_call(
        _mm_res_kernel,
        grid=(m // tm, n // tn),
        in_specs=[pl.BlockSpec((tm, k), lambda i, j: (i, 0), pipeline_mode=pl.Buffered(1)),
                  pl.BlockSpec((k, tn), lambda i, j: (0, j)),
                  pl.BlockSpec((tm, tn), lambda i, j: (i, j))],
        out_specs=pl.BlockSpec((tm, tn), lambda i, j: (i, j)),
        out_shape=jax.ShapeDtypeStruct((m, n), F32),
        compiler_params=_cparams(("parallel", "parallel")),
        name="ffn_down",
    )(a, b, h)


RET_HEADS = 4


def _retention_kernel(q_ref, k_ref, v_ref, g_ref, cos_ref, sin_ref, idec_ref, qdec_ref, kdec_ref, cdec_ref, gn_ref,
                      s0_ref, o_ref, sfin_ref, s_ref, *, tb, c):
    tblk = pl.program_id(2)
    heads = range(RET_HEADS)

    @pl.when(tblk == 0)
    def _():
        s_ref[...] = s0_ref[0]

    def rotate(x, cos, sin):
        return x * cos + pltpu.roll(x, DK_A // 2, 1) * sin

    def chunk_step(ci, carry):
        c0 = pl.multiple_of(ci * c, c)
        cos = cos_ref[pl.ds(c0, c), :]
        sin = sin_ref[pl.ds(c0, c), :]
        qr = [rotate(q_ref[pl.ds(c0, c), h * DK_A:(h + 1) * DK_A], cos, sin) for h in heads]
        kr = [rotate(k_ref[pl.ds(c0, c), h * DK_A:(h + 1) * DK_A], cos, sin) * DK_A ** -0.5 for h in heads]
        v = [v_ref[pl.ds(c0, c), h * DV_A:(h + 1) * DV_A].astype(BF16) for h in heads]
        s = [s_ref[h] for h in heads]
        att = [_dot_nt(qr[h].astype(BF16), kr[h].astype(BF16)) * idec_ref[h] for h in heads]
        o = [_bdot(att[h], v[h]) + _bdot(qr[h] * qdec_ref[h], s[h]) for h in heads]
        for h in heads:
            s_ref[h] = s[h] * cdec_ref[h, 0:1, :] + lax.dot_general(
                (kr[h] * kdec_ref[h]).astype(BF16), v[h], (((0,), (0,)), ((), ())), preferred_element_type=F32)
        for h in heads:
            mu = jnp.mean(o[h], axis=-1, keepdims=True)
            d = o[h] - mu
            var = jnp.mean(d * d, axis=-1, keepdims=True)
            cols = slice(h * DV_A, (h + 1) * DV_A)
            g = g_ref[pl.ds(c0, c), cols]
            o_ref[pl.ds(c0, c), cols] = (
                g * jax.nn.sigmoid(g) * (d * lax.rsqrt(var + EPS) * gn_ref[:, cols])).astype(o_ref.dtype)
        return carry

    lax.fori_loop(0, tb // c, chunk_step, 0)

    @pl.when(tblk == pl.num_programs(2) - 1)
    def _():
        sfin_ref[0] = s_ref[...]


def retention(proj, row0, b, t, p_len, s_ret, ret_gn):
    c = 2 * CHUNK if t % (2 * CHUNK) == 0 else t
    tb = 512 if t % 512 == 0 else t
    assert tb % c == 0 and row0 % tb == 0
    nt = t // tb
    rb = row0 // tb
    half = DK_A // 2
    inv = ROPE_BASE ** (-jnp.arange(half, dtype=F32) / half)
    ang = (p_len + jnp.arange(t)).astype(F32)[:, None] * inv[None, :]
    cos2 = jnp.concatenate([jnp.cos(ang), jnp.cos(ang)], axis=1)
    sin2 = jnp.concatenate([-jnp.sin(ang), jnp.sin(ang)], axis=1)
    log_gamma = jnp.log(1.0 - 2.0 ** (-5.0 - jnp.arange(H_A, dtype=F32)))
    i = jnp.arange(c, dtype=F32)
    rel = i[:, None] - i[None, :]
    idec = jnp.exp(jnp.where(rel[None] >= 0, rel[None] * log_gamma[:, None, None], -jnp.inf))
    qdec = jnp.broadcast_to(jnp.exp((i[None, :] + 1.0) * log_gamma[:, None])[:, :, None], (H_A, c, DK_A))
    kdec = jnp.broadcast_to(jnp.exp((c - 1.0 - i[None, :]) * log_gamma[:, None])[:, :, None], (H_A, c, DK_A))
    cdec = jnp.broadcast_to(jnp.exp(c * log_gamma)[:, None, None], (H_A, 8, DV_A))
    wk, wv = RET_HEADS * DK_A, RET_HEADS * DV_A
    qb0 = COL["q_a"][0] // wk
    kb0 = COL["k_a"][0] // wk
    vb0 = COL["v_a"][0] // wv
    gb0 = COL["g_a"][0] // wv
    sspec = pl.BlockSpec((1, RET_HEADS, DK_A, DV_A), lambda bi, h, k: (bi, h, 0, 0))
    return pl.pallas_call(
        partial(_retention_kernel, tb=tb, c=c),
        grid=(b, H_A // RET_HEADS, nt),
        in_specs=[pl.BlockSpec((tb, wk), lambda bi, h, k: (rb + bi * nt + k, qb0 + h)),
                  pl.BlockSpec((tb, wk), lambda bi, h, k: (rb + bi * nt + k, kb0 + h)),
                  pl.BlockSpec((tb, wv), lambda bi, h, k: (rb + bi * nt + k, vb0 + h)),
                  pl.BlockSpec((tb, wv), lambda bi, h, k: (rb + bi * nt + k, gb0 + h)),
                  pl.BlockSpec((tb, DK_A), lambda bi, h, k: (k, 0)),
                  pl.BlockSpec((tb, DK_A), lambda bi, h, k: (k, 0)),
                  pl.BlockSpec((RET_HEADS, c, c), lambda bi, h, k: (h, 0, 0)),
                  pl.BlockSpec((RET_HEADS, c, DK_A), lambda bi, h, k: (h, 0, 0)),
                  pl.BlockSpec((RET_HEADS, c, DK_A), lambda bi, h, k: (h, 0, 0)),
                  pl.BlockSpec((RET_HEADS, 8, DV_A), lambda bi, h, k: (h, 0, 0)),
                  pl.BlockSpec((1, wv), lambda bi, h, k: (0, h)),
                  sspec],
        out_specs=[pl.BlockSpec((tb, wv), lambda bi, h, k: (bi * nt + k, h)), sspec],
        out_shape=[jax.ShapeDtypeStruct((b * t, W_A), BF16),
                   jax.ShapeDtypeStruct((b, H_A, DK_A, DV_A), F32)],
        scratch_shapes=[pltpu.VMEM((RET_HEADS, DK_A, DV_A), F32)],
        compiler_params=_cparams(("parallel", "parallel", "arbitrary")),
        name="retention",
    )(proj, proj, proj, proj, cos2, sin2, idec, qdec, kdec, cdec, ret_gn.reshape(1, W_A), s_ret)


DELTA_HEADS = 8
DELTA_W = DELTA_HEADS * DK_B
HIST_ROWS = 8
INV_BASE = 16


def _bdot(a, b):
    return jnp.dot(a.astype(BF16), b.astype(BF16), preferred_element_type=F32)


def _hdot(a, b):
    return jnp.dot(a, b, preferred_element_type=F32, precision=lax.Precision.HIGHEST)


def _split_bf16(x):
    hi = x.astype(BF16)
    return hi, (x - hi.astype(F32)).astype(BF16)


def _dot3(a, b):
    a_hi, a_lo = _split_bf16(a)
    b_hi, b_lo = _split_bf16(b)
    dot = partial(jnp.dot, preferred_element_type=F32)
    return dot(a_hi, b_hi) + (dot(a_hi, b_lo) + dot(a_lo, b_hi))


def _unit_lower_inverse(mats, c):
    ii = lax.broadcasted_iota(jnp.int32, (c, c), 0)
    jj = lax.broadcasted_iota(jnp.int32, (c, c), 1)
    eye = jnp.where(ii == jj, 1.0, 0.0).astype(F32)
    ps = [jnp.where((ii // INV_BASE) == (jj // INV_BASE), a, 0.0) for a in mats]
    ts = [eye - p for p in ps]
    for _ in range(int(math.log2(INV_BASE)) - 1):
        ps = [_dot3(p, p) for p in ps]
        ts = [_dot3(t, eye + p) for t, p in zip(ts, ps)]
    size = INV_BASE
    while size < c:
        half_blocks = ((ii // (2 * size)) == (jj // (2 * size))) & ((ii // size) != (jj // size))
        tl = [_dot3(t, jnp.where(half_blocks, a, 0.0)) for t, a in zip(ts, mats)]
        ts = [t - _dot3(x, t) for t, x in zip(ts, tl)]
        size *= 2
    return ts


def _delta_kernel(xq_ref, xk_ref, xv_ref, z_ref, small_ref, cwq_ref, cwk_ref, cwv_ref, hq_ref, hk_ref, hv_ref,
                  par_ref, dn_ref, s0_ref, o_ref, sfin_ref,
                  bq_ref, bk_ref, bv_ref, qn_ref, kn_ref, vv_ref, g_ref, beta_ref, s_ref, *, tb, c):
    hg = pl.program_id(1)
    tblk = pl.program_id(2)

    @pl.when(tblk == 0)
    def _():
        bq_ref[0:HIST_ROWS, :] = hq_ref[0]
        bk_ref[0:HIST_ROWS, :] = hk_ref[0]
        bv_ref[0:HIST_ROWS, :] = hv_ref[0]
        s_ref[...] = s0_ref[0]

    def conv_silu(x_ref, buf_ref, cw_ref):
        buf_ref[HIST_ROWS:HIST_ROWS + tb, :] = x_ref[...]
        y = jnp.zeros((tb, DELTA_W), F32)
        for j in range(CONV_W):
            lo = HIST_ROWS - (CONV_W - 1) + j
            y = y + buf_ref[lo:lo + tb, :] * cw_ref[j:j + 1, :]
        buf_ref[0:HIST_ROWS, :] = buf_ref[tb:tb + HIST_ROWS, :]
        return y * jax.nn.sigmoid(y)

    def l2n(x):
        parts = []
        for h in range(DELTA_HEADS):
            xh = x[:, h * DK_B:(h + 1) * DK_B]
            parts.append(xh * lax.rsqrt(jnp.sum(xh * xh, axis=-1, keepdims=True) + EPS))
        return jnp.concatenate(parts, axis=1)

    qn_ref[...] = l2n(conv_silu(xq_ref, bq_ref, cwq_ref)) * DK_B ** -0.5
    kn_ref[...] = l2n(conv_silu(xk_ref, bk_ref, cwk_ref))
    vv_ref[...] = conv_silu(xv_ref, bv_ref, cwv_ref)
    sm = small_ref[...]
    x = sm + par_ref[1:2, :]
    softplus = jnp.maximum(x, 0.0) + jnp.log1p(jnp.exp(-jnp.abs(x)))
    g_ref[...] = -jnp.exp(par_ref[0:1, :]) * softplus
    beta_ref[...] = jax.nn.sigmoid(sm)

    ii = lax.broadcasted_iota(jnp.int32, (c, c), 0)
    jj = lax.broadcasted_iota(jnp.int32, (c, c), 1)
    incl = ii >= jj
    strict = ii > jj
    diag = ii == jj
    tril_ones = jnp.where(incl, 1.0, 0.0).astype(F32)
    a_lane = COL["a_b"][0] - COL["k_i"][0]
    b_lane = COL["b_b"][0] - COL["k_i"][0]

    def chunk_step(ci, carry):
        c0 = pl.multiple_of(ci * c, c)
        gcum_all = _hdot(tril_ones, g_ref[pl.ds(c0, c), :])
        beta_all = beta_ref[pl.ds(c0, c), :]
        lane = lax.broadcasted_iota(jnp.int32, (c, LANE), 1)
        heads = range(DELTA_HEADS)
        cols = [slice(h * DK_B, (h + 1) * DK_B) for h in heads]
        gc = [jnp.sum(jnp.where(lane == a_lane + hg * DELTA_HEADS + h, gcum_all, 0.0), axis=-1, keepdims=True)
              for h in heads]
        bc = [jnp.sum(jnp.where(lane == b_lane + hg * DELTA_HEADS + h, beta_all, 0.0), axis=-1, keepdims=True)
              for h in heads]
        gr = [jnp.sum(jnp.where(diag, jnp.broadcast_to(g, (c, c)), 0.0), axis=0, keepdims=True) for g in gc]
        g_last = [jnp.sum(jnp.where(ii[:, 0:1] == c - 1, g, 0.0), axis=0, keepdims=True) for g in gc]
        gam = [jnp.exp(jnp.where(incl, g - r, -jnp.inf)) for g, r in zip(gc, gr)]
        eg = [jnp.exp(g) for g in gc]
        qh = [qn_ref[pl.ds(c0, c), cs] for cs in cols]
        kh = [kn_ref[pl.ds(c0, c), cs] for cs in cols]
        vh = [vv_ref[pl.ds(c0, c), cs] for cs in cols]
        kb = [k * b for k, b in zip(kh, bc)]
        a = [jnp.where(strict, _dot_nt(x.astype(BF16), k.astype(BF16)) * gm, 0.0) for x, k, gm in zip(kb, kh, gam)]
        qk = [_dot_nt(q.astype(BF16), k.astype(BF16)) * gm for q, k, gm in zip(qh, kh, gam)]
        tinv = _unit_lower_inverse(a, c)
        u = [_bdot(t, v * b) for t, v, b in zip(tinv, vh, bc)]
        w = [_bdot(t, x * e) for t, x, e in zip(tinv, kb, eg)]
        s = [s_ref[h] for h in heads]
        v_new = [x - _bdot(y, st) for x, y, st in zip(u, w, s)]
        o = [_bdot(q * e, st) + _bdot(m, vn) for q, e, st, m, vn in zip(qh, eg, s, qk, v_new)]
        for h in heads:
            kd = (kh[h] * jnp.exp(g_last[h] - gc[h])).astype(BF16)
            s_ref[h] = s[h] * jnp.exp(g_last[h]) + lax.dot_general(
                kd, v_new[h].astype(BF16), (((0,), (0,)), ((), ())), preferred_element_type=F32)
        for h in heads:
            on = o[h] * lax.rsqrt(jnp.mean(o[h] * o[h], axis=-1, keepdims=True) + EPS) * dn_ref[...]
            zh = z_ref[pl.ds(c0, c), cols[h]]
            o_ref[pl.ds(c0, c), cols[h]] = (on * (zh * jax.nn.sigmoid(zh))).astype(o_ref.dtype)
        return carry

    lax.fori_loop(0, tb // c, chunk_step, 0)

    @pl.when(tblk == pl.num_programs(2) - 1)
    def _():
        sfin_ref[0] = s_ref[...]


def gated_delta(proj, row0, b, t, s_conv, s_delta, conv_w, a_log, dt_bias, d_norm):
    c = CHUNK if t % CHUNK == 0 else t
    tb = 512 if t % 512 == 0 else t
    assert tb % c == 0 and c % INV_BASE == 0 and tb >= HIST_ROWS and row0 % tb == 0
    nt = t // tb
    rb = row0 // tb
    ngrp = H_B // DELTA_HEADS
    qcol = COL["qkv_b"][0] // DELTA_W
    hist = jnp.concatenate([jnp.zeros((b, HIST_ROWS - (CONV_W - 1), CONV_CH), F32), s_conv], axis=1)
    par = jnp.zeros((8, LANE), F32)
    a_lane = COL["a_b"][0] - COL["k_i"][0]
    par = par.at[0, a_lane:a_lane + H_B].set(a_log).at[1, a_lane:a_lane + H_B].set(dt_bias)

    def xspec(part):
        return pl.BlockSpec((tb, DELTA_W), lambda bi, g, k: (rb + bi * nt + k, qcol + part * ngrp + g))

    def wspec(part):
        return pl.BlockSpec((CONV_W, DELTA_W), lambda bi, g, k: (0, part * ngrp + g))

    def hspec(part):
        return pl.BlockSpec((1, HIST_ROWS, DELTA_W), lambda bi, g, k: (bi, 0, part * ngrp + g))

    sspec = pl.BlockSpec((1, DELTA_HEADS, DK_B, DV_B), lambda bi, g, k: (bi, g, 0, 0))
    return pl.pallas_call(
        partial(_delta_kernel, tb=tb, c=c),
        grid=(b, ngrp, nt),
        in_specs=[xspec(0), xspec(1), xspec(2),
                  pl.BlockSpec((tb, DELTA_W), lambda bi, g, k: (rb + bi * nt + k, COL["z_b"][0] // DELTA_W + g)),
                  pl.BlockSpec((tb, LANE), lambda bi, g, k: (rb + bi * nt + k, COL["k_i"][0] // LANE)),
                  wspec(0), wspec(1), wspec(2), hspec(0), hspec(1), hspec(2),
                  pl.BlockSpec((8, LANE), lambda bi, g, k: (0, 0)),
                  pl.BlockSpec((1, DV_B), lambda bi, g, k: (0, 0)),
                  sspec],
        out_specs=[pl.BlockSpec((tb, DELTA_W), lambda bi, g, k: (bi * nt + k, g)), sspec],
        out_shape=[jax.ShapeDtypeStruct((b * t, W_B), BF16),
                   jax.ShapeDtypeStruct((b, H_B, DK_B, DV_B), F32)],
        scratch_shapes=[pltpu.VMEM((tb + HIST_ROWS, DELTA_W), F32)] * 3
        + [pltpu.VMEM((tb, DELTA_W), F32)] * 3
        + [pltpu.VMEM((tb, LANE), F32)] * 2
        + [pltpu.VMEM((DELTA_HEADS, DK_B, DV_B), F32)],
        compiler_params=_cparams(("parallel", "parallel", "arbitrary")),
        name="gated_delta",
    )(proj, proj, proj, proj, proj, conv_w, conv_w, conv_w, hist, hist, hist, par, d_norm.reshape(1, DV_B), s_delta)


def t5_bucket(rel):
    half = NUM_BUCKETS // 2
    exact = half // 2
    n = jnp.abs(rel)
    large = exact + (jnp.log(jnp.maximum(n, 1).astype(jnp.float32) / exact)
                     / math.log(MAX_DISTANCE / exact) * (half - exact)).astype(jnp.int32)
    large = jnp.minimum(large, half - 1)
    return jnp.where(rel > 0, half, 0) + jnp.where(n < exact, n, large)


KEY_TILE = LANE
SCORE_TILE = 2 * KEY_TILE
N_BIAS_NEAR = 6
MASKED = -1e30
INT32_MIN = -2 ** 31
GROUP = H_C // H_KV


def _dot_nt(a, b):
    return lax.dot_general(a, b, (((1,), (1,)), ((), ())), preferred_element_type=F32)


def _dsa_kernel(qc_ref, qi_ref, small_ref, k_ref, v_ref, kidx_ref, bias_ref, o_ref,
                key_ref, qis_ref, wb_ref, qg_ref, acc_ref, m_ref, l_ref, *, tq, q0_tile, seq_len, topk):
    qt = q0_tile + pl.program_id(1)
    n_vis = qt + 1
    row = lax.broadcasted_iota(jnp.int32, (tq, SCORE_TILE), 0)
    lane = lax.broadcasted_iota(jnp.int32, (tq, SCORE_TILE), 1)
    q_chunk = (qt * KEY_TILE + row) // CHUNK
    pad_rows = KEY_TILE - tq

    qi = (qi_ref[...] * D_IDX ** -0.5).astype(BF16)
    w = small_ref[:, D_IDX:D_IDX + H_IDX] * H_IDX ** -0.5
    for i in range(H_IDX // 2):
        qis_ref[i * tq:(i + 1) * tq, :] = qi[:, i * LANE:(i + 1) * LANE]
        for j in range(2):
            wb_ref[i * tq:(i + 1) * tq, j * SCORE_TILE:(j + 1) * SCORE_TILE] = jnp.broadcast_to(
                w[:, 2 * i + j:2 * i + j + 1], (tq, SCORE_TILE))
    first_head = lax.broadcasted_iota(jnp.int32, (SCORE_TILE, LANE), 1) < D_IDX

    def score_tile(u, carry):
        start = pl.multiple_of(u * SCORE_TILE, SCORE_TILE)
        kd = kidx_ref[0, pl.ds(start, SCORE_TILE), :]
        zero = jnp.zeros_like(kd)
        kt2 = jnp.concatenate([jnp.where(first_head, kd, zero), jnp.where(first_head, zero, kd)], axis=0)
        s = _dot_nt(qis_ref[...], kt2)
        r = jnp.maximum(s, 0.0) * wb_ref[...]
        r = jnp.sum(r.reshape(H_IDX // 2, tq, 2 * SCORE_TILE), axis=0)
        score = r[:, :SCORE_TILE] + r[:, SCORE_TILE:] + 0.0
        bits = pltpu.bitcast(score, jnp.int32)
        key = jnp.where(bits < 0, bits ^ jnp.int32(0x7FFFFFFF), bits)
        kpos = start + lane
        visible = ((kpos // CHUNK) <= q_chunk) & (kpos < seq_len)
        key = jnp.where(visible, key, jnp.int32(INT32_MIN))
        if pad_rows:
            key = jnp.concatenate([key, jnp.full((pad_rows, SCORE_TILE), INT32_MIN, jnp.int32)], axis=0)
        for j in range(SCORE_TILE // KEY_TILE):
            key_ref[u * (SCORE_TILE // KEY_TILE) + j] = key[:, j * KEY_TILE:(j + 1) * KEY_TILE].T
        return carry

    lax.fori_loop(0, pl.cdiv(n_vis * KEY_TILE, SCORE_TILE), score_tile, 0)

    def count_keys(pred):
        def count_tile(t, cnt):
            kpos = t * KEY_TILE + lax.broadcasted_iota(jnp.int32, (KEY_TILE, KEY_TILE), 0)
            return cnt + jnp.where(pred(key_ref[t], kpos), 1.0, 0.0)

        cnt = lax.fori_loop(0, n_vis, count_tile, jnp.zeros((KEY_TILE, KEY_TILE), F32))
        return jnp.sum(cnt, axis=0, keepdims=True)

    thr = jnp.full((1, KEY_TILE), INT32_MIN, jnp.int32)
    for bit in range(31, -1, -1):
        cand = thr + jnp.int32(-2 ** 31 if bit == 31 else 2 ** bit)
        thr = jnp.where(count_keys(lambda key, kpos, cand=cand: key >= cand) >= float(topk), cand, thr)
    thr = jnp.maximum(thr, jnp.int32(INT32_MIN + 1))

    n_above = count_keys(lambda key, kpos: key > thr)
    n_tied = count_keys(lambda key, kpos: key == thr)
    ties_wanted = float(topk) - n_above
    pos_bits = (key_ref.shape[0] * KEY_TILE - 1).bit_length()

    def find_last_tie():
        last = jnp.zeros((1, KEY_TILE), jnp.int32)
        for bit in range(pos_bits - 1, -1, -1):
            cand = last | jnp.int32(2 ** bit)
            before = count_keys(lambda key, kpos, cand=cand: (key == thr) & (kpos < cand))
            last = jnp.where(before < ties_wanted, cand, last)
        return last

    surplus = jnp.max(n_tied - ties_wanted) > 0.0
    last_tie = lax.cond(surplus, find_last_tie, lambda: jnp.full((1, KEY_TILE), 2 ** pos_bits - 1, jnp.int32))
    thr_g = jnp.concatenate([thr] * GROUP, axis=1)
    last_tie_g = jnp.concatenate([last_tie] * GROUP, axis=1)

    kvs = range(H_KV)
    for kv in kvs:
        for g in range(GROUP):
            c0 = (kv * GROUP + g) * DH_C
            qh = (qc_ref[:, c0:c0 + DH_C] * DH_C ** -0.5).astype(BF16)
            if pad_rows:
                qh = jnp.concatenate([qh, jnp.zeros((pad_rows, DH_C), BF16)], axis=0)
            qg_ref[kv, g * KEY_TILE:(g + 1) * KEY_TILE, :] = qh
    acc_ref[...] = jnp.zeros(acc_ref.shape, F32)
    m_ref[...] = jnp.full(m_ref.shape, MASKED, F32)
    l_ref[...] = jnp.zeros(l_ref.shape, F32)

    def attend_tile(t, carry):
        start = pl.multiple_of(t * KEY_TILE, KEY_TILE)
        key_g = jnp.concatenate([key_ref[t]] * GROUP, axis=1)
        kpos = start + lax.broadcasted_iota(jnp.int32, (KEY_TILE, GROUP * KEY_TILE), 0)
        sel = (key_g > thr_g) | ((key_g == thr_g) & (kpos <= last_tie_g))
        dd = jnp.minimum(qt - t, N_BIAS_NEAR)
        s = [_dot_nt(k_ref[0, pl.ds(start, KEY_TILE), kv * DH_C:(kv + 1) * DH_C], qg_ref[kv]) for kv in kvs]
        sm = [jnp.where(sel, s[kv] + bias_ref[dd, kv], MASKED) for kv in kvs]
        m_old = [m_ref[kv:kv + 1, :] for kv in kvs]
        m_new = [jnp.maximum(m_old[kv], jnp.max(sm[kv], axis=0, keepdims=True)) for kv in kvs]
        alpha = [jnp.exp(m_old[kv] - m_new[kv]) for kv in kvs]
        p = [jnp.exp(sm[kv] - m_new[kv]) for kv in kvs]
        pv = [lax.dot_general(v_ref[0, pl.ds(start, KEY_TILE), kv * DH_C:(kv + 1) * DH_C], p[kv].astype(BF16),
                              (((0,), (0,)), ((), ())), preferred_element_type=F32) for kv in kvs]
        for kv in kvs:
            l_ref[kv:kv + 1, :] = alpha[kv] * l_ref[kv:kv + 1, :] + jnp.sum(p[kv], axis=0, keepdims=True)
            acc_ref[kv] = alpha[kv] * acc_ref[kv] + pv[kv]
            m_ref[kv:kv + 1, :] = m_new[kv]
        return carry

    lax.fori_loop(0, n_vis, attend_tile, 0)
    for kv in kvs:
        out = (acc_ref[kv] / l_ref[kv:kv + 1, :]).T
        for g in range(GROUP):
            c0 = (kv * GROUP + g) * DH_C
            o_ref[:, c0:c0 + DH_C] = out[g * KEY_TILE:g * KEY_TILE + tq].astype(o_ref.dtype)


def dsa_bias_tiles(rel_bias):
    i = jnp.arange(KEY_TILE)
    d = jnp.arange(N_BIAS_NEAR + 1)
    rel = (i[None, :, None] - i[None, None, :]) - KEY_TILE * d[:, None, None]
    onehot = (t5_bucket(rel)[..., None] == jnp.arange(NUM_BUCKETS)).astype(F32)
    tiles = jnp.einsum('dkqn,nh->dkqh', onehot, rel_bias.astype(F32), precision=lax.Precision.HIGHEST)
    tiles = tiles.reshape(N_BIAS_NEAR + 1, KEY_TILE, KEY_TILE, H_KV, GROUP)
    return jnp.transpose(tiles, (0, 3, 1, 4, 2)).reshape(N_BIAS_NEAR + 1, H_KV, KEY_TILE, GROUP * KEY_TILE)


def dsa_attention(proj, row0, b, t, k_all, v_all, kidx_all, bias_tiles, p_len, seq_len, topk):
    tq = KEY_TILE if t % KEY_TILE == 0 else t
    assert p_len % KEY_TILE == 0 and KEY_TILE % tq == 0 and (tq == KEY_TILE or t == tq) and row0 % tq == 0
    nb = t // tq
    lp = k_all.shape[1]
    assert lp % SCORE_TILE == 0 and lp >= p_len + nb * KEY_TILE
    assert 8 * (MAX_DISTANCE / 8) ** (7 / 8) < KEY_TILE * N_BIAS_NEAR - (KEY_TILE - 1)
    rb = row0 // tq
    kern = partial(_dsa_kernel, tq=tq, q0_tile=p_len // KEY_TILE, seq_len=seq_len, topk=topk)
    return pl.pallas_call(
        kern,
        grid=(b, nb),
        in_specs=[pl.BlockSpec((tq, W_C), lambda bi, j: (rb + bi * nb + j, COL["q_c"][0] // W_C)),
                  pl.BlockSpec((tq, H_IDX * D_IDX), lambda bi, j: (rb + bi * nb + j, COL["q_i"][0] // (H_IDX * D_IDX))),
                  pl.BlockSpec((tq, LANE), lambda bi, j: (rb + bi * nb + j, COL["k_i"][0] // LANE)),
                  pl.BlockSpec((1, lp, H_KV * DH_C), lambda bi, j: (bi, 0, 0)),
                  pl.BlockSpec((1, lp, H_KV * DH_C), lambda bi, j: (bi, 0, 0)),
                  pl.BlockSpec((1, lp, 2 * D_IDX), lambda bi, j: (bi, 0, 0)),
                  pl.BlockSpec((N_BIAS_NEAR + 1, H_KV, KEY_TILE, GROUP * KEY_TILE), lambda bi, j: (0, 0, 0, 0))],
        out_specs=pl.BlockSpec((tq, W_C), lambda bi, j: (bi * nb + j, 0)),
        out_shape=jax.ShapeDtypeStruct((b * t, W_C), BF16),
        scratch_shapes=[pltpu.VMEM((lp // KEY_TILE, KEY_TILE, KEY_TILE), jnp.int32),
                        pltpu.VMEM((H_IDX // 2 * tq, 2 * D_IDX), BF16),
                        pltpu.VMEM((H_IDX // 2 * tq, 2 * SCORE_TILE), F32),
                        pltpu.VMEM((H_KV, GROUP * KEY_TILE, DH_C), BF16),
                        pltpu.VMEM((H_KV, DH_C, GROUP * KEY_TILE), F32),
                        pltpu.VMEM((8, GROUP * KEY_TILE), F32),
                        pltpu.VMEM((8, GROUP * KEY_TILE), F32)],
        compiler_params=_cparams(("parallel", "arbitrary")),
        name="dsa_attention",
    )(proj, proj, proj, k_all, v_all, kidx_all, bias_tiles)


def branches(proj_all, new_rows, row0, b, t, past_k, past_v, past_kidx, s_ret, s_delta, s_conv,
             ret_gn, conv_w, a_log, dt_bias, d_norm, bias_tiles):
    p_len = past_k.shape[1]
    seq_len = p_len + t
    topk = min(TOPK_MAX, seq_len // 4)
    lp = -(-seq_len // SCORE_TILE) * SCORE_TILE
    k_c, v_c, k_i, k_bf, v_bf, ki_bf = (x[row0:row0 + b * t].reshape(b, t, x.shape[-1]) for x in new_rows)

    o, w = COL["qkv_b"]
    qkv_tail = proj_all[row0:row0 + b * t].reshape(b, t, D_IN_PAD)[:, -(CONV_W - 1):, o:o + w]
    conv_new = jnp.concatenate([s_conv, qkv_tail], axis=1)[:, -(CONV_W - 1):]

    o_a, ret_new = retention(proj_all, row0, b, t, p_len, s_ret, ret_gn)
    o_b, delta_new = gated_delta(proj_all, row0, b, t, s_conv, s_delta, conv_w, a_log, dt_bias, d_norm)

    def all_keys(past, new, reps=1):
        if p_len == 0 and lp == seq_len:
            return new
        w = past.shape[-1] * reps
        past = jnp.tile(past.reshape(b, p_len, -1).astype(BF16), (1, 1, reps))
        return jnp.concatenate([past, new, jnp.zeros((b, lp - seq_len, w), BF16)], axis=1)

    o_c = dsa_attention(proj_all, row0, b, t, all_keys(past_k.reshape(b, p_len, KV_W), k_bf),
                        all_keys(past_v.reshape(b, p_len, KV_W), v_bf), all_keys(past_kidx, ki_bf, 2),
                        bias_tiles, p_len, seq_len, topk)
    caches = (k_c.reshape(b, t, H_KV, DH_C), v_c.reshape(b, t, H_KV, DH_C), k_i, ret_new, delta_new, conv_new)
    return jnp.stack([o_a, o_b, o_c]), caches


def _prep_w_in(w):
    return jnp.concatenate([w[:, 0:14336], w[:, 14368:16416], w[:, 17440:19488], w[:, 16416:17440],
                            w[:, 19584:31872], w[:, 19488:19584], w[:, 14336:14368],
                            jnp.zeros((w.shape[0], D_IN_PAD - D_IN), w.dtype)], axis=1).astype(BF16)


def kernel(x_prompt, x_sample, cache_k, cache_v, cache_kidx, state_ret, state_delta, state_conv, norm_mix, w_in, ret_gn, conv_w, delta_a_log, delta_dt_bias, delta_norm, rel_bias, w_branch_a, w_branch_b, w_branch_c, w_out, norm_ffn, w_ffn_gate, w_ffn_up, w_ffn_down, norm_final):
    dt = x_prompt.dtype
    bp, tp, _ = x_prompt.shape
    bs, ts, _ = x_sample.shape
    depth = w_in.shape[0]
    n_p = bp * tp
    zk = jnp.zeros((bp, 0, H_KV, DH_C), dt)
    zkidx = jnp.zeros((bp, 0, D_IDX), dt)
    zret = jnp.zeros((bp, H_A, DK_A, DV_A), dt)
    zdelta = jnp.zeros((bp, H_B, DK_B, DV_B), dt)
    zconv = jnp.zeros((bp, CONV_W - 1, CONV_CH), dt)

    h = jnp.concatenate([x_prompt.reshape(n_p, D_MODEL), x_sample.reshape(bs * ts, D_MODEL)], axis=0)
    new_p, new_s = [], []
    bias_tiles = dsa_bias_tiles(rel_bias)
    for l in range(depth):
        u = rmsnorm(h, norm_mix[l], BF16)
        proj, *new_rows = matmul_in(u, _prep_w_in(w_in[l]))
        bw = (ret_gn[l], conv_w[l], delta_a_log[l], delta_dt_bias[l], delta_norm[l], bias_tiles)
        o_p, sp = branches(proj, new_rows, 0, bp, tp, zk, zk, zkidx, zret, zdelta, zconv, *bw)
        o_s, ss = branches(proj, new_rows, n_p, bs, ts, cache_k[l], cache_v[l], cache_kidx[l],
                           state_ret[l], state_delta[l], state_conv[l], *bw)
        new_p.append(sp)
        new_s.append(ss)
        o_abc = jnp.concatenate([o_p, o_s], axis=1)
        w_br = jnp.stack([w_branch_a[l], w_branch_b[l], w_branch_c[l]]).astype(BF16)
        merged = merge_branches(o_abc, w_br, proj)
        h = matmul_residual(merged, w_out[l].astype(BF16), h)
        u2 = rmsnorm(h, norm_ffn[l], BF16)
        act = ffn_gate_up(u2, w_ffn_gate[l].astype(BF16), w_ffn_up[l].astype(BF16))
        h = ffn_down(act, w_ffn_down[l].astype(BF16), h)

    y = rmsnorm(h, norm_final, F32)
    y_prompt = y[:n_p].reshape(bp, tp, D_MODEL)
    y_sample = y[n_p:].reshape(bs, ts, D_MODEL)
    k_p, v_p, kidx_p, ret_p, delta_p, conv_p = [jnp.stack([s[i] for s in new_p]) for i in range(6)]
    k_s, v_s, kidx_s, ret_s, delta_s, conv_s = [jnp.stack([s[i] for s in new_s]) for i in range(6)]
    return (y_prompt, y_sample, k_p, v_p, kidx_p, ret_p, delta_p, conv_p,
            k_s, v_s, kidx_s, ret_s, delta_s, conv_s)
```

```python
import math
from functools import partial

import jax
import jax.numpy as jnp
import numpy as np
from jax import lax
from jax.experimental import pallas as pl
from jax.experimental.pallas import tpu as pltpu

F32 = jnp.float32
BF16 = jnp.bfloat16

D_MODEL = 4096
CHUNK = 64
QUERY_BLOCK = 128
EPS = 1e-6
H_A, DK_A, DV_A = 8, 128, 256
W_A = H_A * DV_A
ROPE_BASE = 10000.0
H_B, DK_B, DV_B = 16, 128, 128
W_B = H_B * DV_B
CONV_W = 4
CONV_CH = 2 * H_B * DK_B + H_B * DV_B
H_C, H_KV, DH_C = 16, 4, 128
W_C = H_C * DH_C
H_IDX, D_IDX = 32, 64
TOPK_MAX = 256
NUM_BUCKETS = 32
MAX_DISTANCE = 1024
D_FF = 11008

LANE = 128
VMEM_LIMIT_V7X = 56 * 1024 * 1024

COL = {}
_off = 0
for _name, _w in [("q_a", 1024), ("k_a", 1024), ("v_a", 2048), ("g_a", 2048), ("qkv_b", 6144), ("z_b", 2048),
                  ("q_c", 2048), ("q_i", 2048), ("k_c", 512), ("v_c", 512),
                  ("gate_a", 4096), ("gate_b", 4096), ("gate_c", 4096),
                  ("k_i", 64), ("w_i", 32), ("a_b", 16), ("b_b", 16)]:
    COL[_name] = (_off, _w)
    _off += _w
D_IN = _off
D_IN_PAD = 32768
TM = 1024


def _cparams(sem):
    return pltpu.CompilerParams(dimension_semantics=sem, vmem_limit_bytes=VMEM_LIMIT_V7X)


def _rmsnorm_kernel(x_ref, g_ref, o_ref):
    x = x_ref[...]
    y = x * lax.rsqrt(jnp.mean(x * x, axis=-1, keepdims=True) + EPS)
    o_ref[...] = (y * g_ref[...]).astype(o_ref.dtype)


def rmsnorm(x, g, out_dtype, tr=512):
    m, d = x.shape
    return pl.pallas_call(
        _rmsnorm_kernel,
        grid=(m // tr,),
        in_specs=[pl.BlockSpec((tr, d), lambda i: (i, 0)), pl.BlockSpec((1, d), lambda i: (0, 0))],
        out_specs=pl.BlockSpec((tr, d), lambda i: (i, 0)),
        out_shape=jax.ShapeDtypeStruct((m, d), out_dtype),
        compiler_params=_cparams(("parallel",)),
        name="rmsnorm",
    )(x, g.reshape(1, d))


KV_W = H_KV * DH_C


def _mm_in_kernel(a_ref, b_ref, o_ref, k_ref, v_ref, ki_ref, kb_ref, vb_ref, kib_ref, *, kv_block, small_block):
    j = pl.program_id(1)
    o_ref[...] = jnp.dot(a_ref[...], b_ref[...], preferred_element_type=F32)

    @pl.when(j == kv_block)
    def _():
        k, v = o_ref[:, :KV_W], o_ref[:, KV_W:2 * KV_W]
        k_ref[...] = k
        v_ref[...] = v
        kb_ref[...] = k.astype(BF16)
        vb_ref[...] = v.astype(BF16)

    @pl.when(j == small_block)
    def _():
        ki = o_ref[:, :D_IDX]
        ki_ref[...] = ki
        kib_ref[...] = jnp.concatenate([ki, ki], axis=1).astype(BF16)


def matmul_in(a, b, tn=1024):
    m, k = a.shape
    n = b.shape[1]
    assert COL["k_c"][0] % tn == 0 and COL["v_c"][0] == COL["k_c"][0] + KV_W and COL["k_i"][0] % tn == 0
    kern = partial(_mm_in_kernel, kv_block=COL["k_c"][0] // tn, small_block=COL["k_i"][0] // tn)

    def row_spec(w):
        return pl.BlockSpec((TM, w), lambda i, j: (i, 0))

    return pl.pallas_call(
        kern,
        grid=(m // TM, n // tn),
        in_specs=[pl.BlockSpec((TM, k), lambda i, j: (i, 0), pipeline_mode=pl.Buffered(1)),
                  pl.BlockSpec((k, tn), lambda i, j: (0, j))],
        out_specs=[pl.BlockSpec((TM, tn), lambda i, j: (i, j)), row_spec(KV_W), row_spec(KV_W), row_spec(D_IDX),
                   row_spec(KV_W), row_spec(KV_W), row_spec(2 * D_IDX)],
        out_shape=[jax.ShapeDtypeStruct((m, n), F32), jax.ShapeDtypeStruct((m, KV_W), F32),
                   jax.ShapeDtypeStruct((m, KV_W), F32), jax.ShapeDtypeStruct((m, D_IDX), F32),
                   jax.ShapeDtypeStruct((m, KV_W), BF16), jax.ShapeDtypeStruct((m, KV_W), BF16),
                   jax.ShapeDtypeStruct((m, 2 * D_IDX), BF16)],
        compiler_params=_cparams(("parallel", "arbitrary")),
        name="matmul_in",
    )(a, b)


def _merge_kernel(o_ref, w_ref, g_ref, out_ref, acc_ref):
    br = pl.program_id(2)
    y = jnp.dot(o_ref[0], w_ref[0], preferred_element_type=F32) * jax.nn.sigmoid(g_ref[...])

    @pl.when(br == 0)
    def _():
        acc_ref[...] = y

    @pl.when(br != 0)
    def _():
        acc_ref[...] += y

    @pl.when(br == pl.num_programs(2) - 1)
    def _():
        out_ref[...] = acc_ref[...].astype(out_ref.dtype)


def merge_branches(o_abc, w_br, proj, tn=1024):
    _, m, k = o_abc.shape
    n = w_br.shape[2]
    gate0 = COL["gate_a"][0] // tn
    nj = n // tn
    return pl.pallas_call(
        _merge_kernel,
        grid=(m // TM, nj, 3),
        in_specs=[pl.BlockSpec((1, TM, k), lambda i, j, r: (r, i, 0)),
                  pl.BlockSpec((1, k, tn), lambda i, j, r: (r, 0, j)),
                  pl.BlockSpec((TM, tn), lambda i, j, r: (i, gate0 + r * nj + j))],
        out_specs=pl.BlockSpec((TM, tn), lambda i, j, r: (i, j)),
        out_shape=jax.ShapeDtypeStruct((m, n), BF16),
        scratch_shapes=[pltpu.VMEM((TM, tn), F32)],
        compiler_params=_cparams(("parallel", "parallel", "arbitrary")),
        name="merge_branches",
    )(o_abc, w_br, proj)


def _mm_res_kernel(a_ref, b_ref, h_ref, o_ref):
    o_ref[...] = h_ref[...] + jnp.dot(a_ref[...], b_ref[...], preferred_element_type=F32)


def matmul_residual(a, b, h, tn=1024):
    m, k = a.shape
    n = b.shape[1]
    return pl.pallas_call(
        _mm_res_kernel,
        grid=(m // TM, n // tn),
        in_specs=[pl.BlockSpec((TM, k), lambda i, j: (i, 0)), pl.BlockSpec((k, tn), lambda i, j: (0, j)),
                  pl.BlockSpec((TM, tn), lambda i, j: (i, j))],
        out_specs=pl.BlockSpec((TM, tn), lambda i, j: (i, j)),
        out_shape=jax.ShapeDtypeStruct((m, n), F32),
        compiler_params=_cparams(("parallel", "parallel")),
        name="matmul_out",
    )(a, b, h)


def _gateup_kernel(u_ref, wg_ref, wu_ref, o_ref):
    u = u_ref[...]
    g = jnp.dot(u, wg_ref[...], preferred_element_type=F32)
    up = jnp.dot(u, wu_ref[...], preferred_element_type=F32)
    o_ref[...] = (jax.nn.silu(g) * up).astype(o_ref.dtype)


def ffn_gate_up(u, wg, wu, tn=512):
    m, k = u.shape
    n = wg.shape[1]
    return pl.pallas_call(
        _gateup_kernel,
        grid=(m // TM, pl.cdiv(n, tn)),
        in_specs=[pl.BlockSpec((TM, k), lambda i, j: (i, 0)),
                  pl.BlockSpec((k, tn), lambda i, j: (0, j)),
                  pl.BlockSpec((k, tn), lambda i, j: (0, j))],
        out_specs=pl.BlockSpec((TM, tn), lambda i, j: (i, j)),
        out_shape=jax.ShapeDtypeStruct((m, n), BF16),
        compiler_params=_cparams(("parallel", "parallel")),
        name="ffn_gate_up",
    )(u, wg, wu)


def ffn_down(a, b, h, tm=768, tn=512):
    m, k = a.shape
    n = b.shape[1]
    return pl.pallas_call(
        _mm_res_kernel,
        grid=(m // tm, n // tn),
        in_specs=[pl.BlockSpec((tm, k), lambda i, j: (i, 0), pipeline_mode=pl.Buffered(1)),
                  pl.BlockSpec((k, tn), lambda i, j: (0, j)),
                  pl.BlockSpec((tm, tn), lambda i, j: (i, j))],
        out_specs=pl.BlockSpec((tm, tn), lambda i, j: (i, j)),
        out_shape=jax.ShapeDtypeStruct((m, n), F32),
        compiler_params=_cparams(("parallel", "parallel")),
        name="ffn_down",
    )(a, b, h)


RET_HEADS = 4


def _retention_kernel(q_ref, k_ref, v_ref, g_ref, cos_ref, sin_ref, idec_ref, qdec_ref, kdec_ref, cdec_ref, gn_ref,
                      s0_ref, o_ref, sfin_ref, s_ref, *, tb, c):
    tblk = pl.program_id(2)
    heads = range(RET_HEADS)

    @pl.when(tblk == 0)
    def _():
        s_ref[...] = s0_ref[0]

    def rotate(x, cos, sin):
        return x * cos + pltpu.roll(x, DK_A // 2, 1) * sin

    def chunk_step(ci, carry):
        c0 = pl.multiple_of(ci * c, c)
        cos = cos_ref[pl.ds(c0, c), :]
        sin = sin_ref[pl.ds(c0, c), :]
        qr = [rotate(q_ref[pl.ds(c0, c), h * DK_A:(h + 1) * DK_A], cos, sin) for h in heads]
        kr = [rotate(k_ref[pl.ds(c0, c), h * DK_A:(h + 1) * DK_A], cos, sin) * DK_A ** -0.5 for h in heads]
        v = [v_ref[pl.ds(c0, c), h * DV_A:(h + 1) * DV_A].astype(BF16) for h in heads]
        s = [s_ref[h] for h in heads]
        att = [_dot_nt(qr[h].astype(BF16), kr[h].astype(BF16)) * idec_ref[h] for h in heads]
        o = [_bdot(att[h], v[h]) + _bdot(qr[h] * qdec_ref[h], s[h]) for h in heads]
        for h in heads:
            s_ref[h] = s[h] * cdec_ref[h, 0:1, :] + lax.dot_general(
                (kr[h] * kdec_ref[h]).astype(BF16), v[h], (((0,), (0,)), ((), ())), preferred_element_type=F32)
        for h in heads:
            mu = jnp.mean(o[h], axis=-1, keepdims=True)
            d = o[h] - mu
            var = jnp.mean(d * d, axis=-1, keepdims=True)
            cols = slice(h * DV_A, (h + 1) * DV_A)
            g = g_ref[pl.ds(c0, c), cols]
            o_ref[pl.ds(c0, c), cols] = (
                g * jax.nn.sigmoid(g) * (d * lax.rsqrt(var + EPS) * gn_ref[:, cols])).astype(o_ref.dtype)
        return carry

    lax.fori_loop(0, tb // c, chunk_step, 0)

    @pl.when(tblk == pl.num_programs(2) - 1)
    def _():
        sfin_ref[0] = s_ref[...]


def retention(proj, row0, b, t, p_len, s_ret, ret_gn):
    c = 2 * CHUNK if t % (2 * CHUNK) == 0 else t
    tb = 512 if t % 512 == 0 else t
    assert tb % c == 0 and row0 % tb == 0
    nt = t // tb
    rb = row0 // tb
    half = DK_A // 2
    inv = ROPE_BASE ** (-jnp.arange(half, dtype=F32) / half)
    ang = (p_len + jnp.arange(t)).astype(F32)[:, None] * inv[None, :]
    cos2 = jnp.concatenate([jnp.cos(ang), jnp.cos(ang)], axis=1)
    sin2 = jnp.concatenate([-jnp.sin(ang), jnp.sin(ang)], axis=1)
    log_gamma = jnp.log(1.0 - 2.0 ** (-5.0 - jnp.arange(H_A, dtype=F32)))
    i = jnp.arange(c, dtype=F32)
    rel = i[:, None] - i[None, :]
    idec = jnp.exp(jnp.where(rel[None] >= 0, rel[None] * log_gamma[:, None, None], -jnp.inf))
    qdec = jnp.broadcast_to(jnp.exp((i[None, :] + 1.0) * log_gamma[:, None])[:, :, None], (H_A, c, DK_A))
    kdec = jnp.broadcast_to(jnp.exp((c - 1.0 - i[None, :]) * log_gamma[:, None])[:, :, None], (H_A, c, DK_A))
    cdec = jnp.broadcast_to(jnp.exp(c * log_gamma)[:, None, None], (H_A, 8, DV_A))
    wk, wv = RET_HEADS * DK_A, RET_HEADS * DV_A
    qb0 = COL["q_a"][0] // wk
    kb0 = COL["k_a"][0] // wk
    vb0 = COL["v_a"][0] // wv
    gb0 = COL["g_a"][0] // wv
    sspec = pl.BlockSpec((1, RET_HEADS, DK_A, DV_A), lambda bi, h, k: (bi, h, 0, 0))
    return pl.pallas_call(
        partial(_retention_kernel, tb=tb, c=c),
        grid=(b, H_A // RET_HEADS, nt),
        in_specs=[pl.BlockSpec((tb, wk), lambda bi, h, k: (rb + bi * nt + k, qb0 + h)),
                  pl.BlockSpec((tb, wk), lambda bi, h, k: (rb + bi * nt + k, kb0 + h)),
                  pl.BlockSpec((tb, wv), lambda bi, h, k: (rb + bi * nt + k, vb0 + h)),
                  pl.BlockSpec((tb, wv), lambda bi, h, k: (rb + bi * nt + k, gb0 + h)),
                  pl.BlockSpec((tb, DK_A), lambda bi, h, k: (k, 0)),
                  pl.BlockSpec((tb, DK_A), lambda bi, h, k: (k, 0)),
                  pl.BlockSpec((RET_HEADS, c, c), lambda bi, h, k: (h, 0, 0)),
                  pl.BlockSpec((RET_HEADS, c, DK_A), lambda bi, h, k: (h, 0, 0)),
                  pl.BlockSpec((RET_HEADS, c, DK_A), lambda bi, h, k: (h, 0, 0)),
                  pl.BlockSpec((RET_HEADS, 8, DV_A), lambda bi, h, k: (h, 0, 0)),
                  pl.BlockSpec((1, wv), lambda bi, h, k: (0, h)),
                  sspec],
        out_specs=[pl.BlockSpec((tb, wv), lambda bi, h, k: (bi * nt + k, h)), sspec],
        out_shape=[jax.ShapeDtypeStruct((b * t, W_A), BF16),
                   jax.ShapeDtypeStruct((b, H_A, DK_A, DV_A), F32)],
        scratch_shapes=[pltpu.VMEM((RET_HEADS, DK_A, DV_A), F32)],
        compiler_params=_cparams(("parallel", "parallel", "arbitrary")),
        name="retention",
    )(proj, proj, proj, proj, cos2, sin2, idec, qdec, kdec, cdec, ret_gn.reshape(1, W_A), s_ret)


DELTA_HEADS = 8
DELTA_W = DELTA_HEADS * DK_B
HIST_ROWS = 8
INV_BASE = 16


def _bdot(a, b):
    return jnp.dot(a.astype(BF16), b.astype(BF16), preferred_element_type=F32)


def _hdot(a, b):
    return jnp.dot(a, b, preferred_element_type=F32, precision=lax.Precision.HIGHEST)


def _split_bf16(x):
    hi = x.astype(BF16)
    return hi, (x - hi.astype(F32)).astype(BF16)


def _dot3(a, b):
    a_hi, a_lo = _split_bf16(a)
    b_hi, b_lo = _split_bf16(b)
    dot = partial(jnp.dot, preferred_element_type=F32)
    return dot(a_hi, b_hi) + (dot(a_hi, b_lo) + dot(a_lo, b_hi))


def _unit_lower_inverse(mats, c):
    ii = lax.broadcasted_iota(jnp.int32, (c, c), 0)
    jj = lax.broadcasted_iota(jnp.int32, (c, c), 1)
    eye = jnp.where(ii == jj, 1.0, 0.0).astype(F32)
    ps = [jnp.where((ii // INV_BASE) == (jj // INV_BASE), a, 0.0) for a in mats]
    ts = [eye - p for p in ps]
    for _ in range(int(math.log2(INV_BASE)) - 1):
        ps = [_dot3(p, p) for p in ps]
        ts = [_dot3(t, eye + p) for t, p in zip(ts, ps)]
    size = INV_BASE
    while size < c:
        half_blocks = ((ii // (2 * size)) == (jj // (2 * size))) & ((ii // size) != (jj // size))
        tl = [_dot3(t, jnp.where(half_blocks, a, 0.0)) for t, a in zip(ts, mats)]
        ts = [t - _dot3(x, t) for t, x in zip(ts, tl)]
        size *= 2
    return ts


def _delta_kernel(xq_ref, xk_ref, xv_ref, z_ref, small_ref, cwq_ref, cwk_ref, cwv_ref, hq_ref, hk_ref, hv_ref,
                  par_ref, dn_ref, s0_ref, o_ref, sfin_ref,
                  bq_ref, bk_ref, bv_ref, qn_ref, kn_ref, vv_ref, g_ref, beta_ref, s_ref, *, tb, c):
    hg = pl.program_id(1)
    tblk = pl.program_id(2)

    @pl.when(tblk == 0)
    def _():
        bq_ref[0:HIST_ROWS, :] = hq_ref[0]
        bk_ref[0:HIST_ROWS, :] = hk_ref[0]
        bv_ref[0:HIST_ROWS, :] = hv_ref[0]
        s_ref[...] = s0_ref[0]

    def conv_silu(x_ref, buf_ref, cw_ref):
        buf_ref[HIST_ROWS:HIST_ROWS + tb, :] = x_ref[...]
        y = jnp.zeros((tb, DELTA_W), F32)
        for j in range(CONV_W):
            lo = HIST_ROWS - (CONV_W - 1) + j
            y = y + buf_ref[lo:lo + tb, :] * cw_ref[j:j + 1, :]
        buf_ref[0:HIST_ROWS, :] = buf_ref[tb:tb + HIST_ROWS, :]
        return y * jax.nn.sigmoid(y)

    def l2n(x):
        parts = []
        for h in range(DELTA_HEADS):
            xh = x[:, h * DK_B:(h + 1) * DK_B]
            parts.append(xh * lax.rsqrt(jnp.sum(xh * xh, axis=-1, keepdims=True) + EPS))
        return jnp.concatenate(parts, axis=1)

    qn_ref[...] = l2n(conv_silu(xq_ref, bq_ref, cwq_ref)) * DK_B ** -0.5
    kn_ref[...] = l2n(conv_silu(xk_ref, bk_ref, cwk_ref))
    vv_ref[...] = conv_silu(xv_ref, bv_ref, cwv_ref)
    sm = small_ref[...]
    x = sm + par_ref[1:2, :]
    softplus = jnp.maximum(x, 0.0) + jnp.log1p(jnp.exp(-jnp.abs(x)))
    g_ref[...] = -jnp.exp(par_ref[0:1, :]) * softplus
    beta_ref[...] = jax.nn.sigmoid(sm)

    ii = lax.broadcasted_iota(jnp.int32, (c, c), 0)
    jj = lax.broadcasted_iota(jnp.int32, (c, c), 1)
    incl = ii >= jj
    strict = ii > jj
    diag = ii == jj
    tril_ones = jnp.where(incl, 1.0, 0.0).astype(F32)
    a_lane = COL["a_b"][0] - COL["k_i"][0]
    b_lane = COL["b_b"][0] - COL["k_i"][0]

    def chunk_step(ci, carry):
        c0 = pl.multiple_of(ci * c, c)
        gcum_all = _hdot(tril_ones, g_ref[pl.ds(c0, c), :])
        beta_all = beta_ref[pl.ds(c0, c), :]
        lane = lax.broadcasted_iota(jnp.int32, (c, LANE), 1)
        heads = range(DELTA_HEADS)
        cols = [slice(h * DK_B, (h + 1) * DK_B) for h in heads]
        gc = [jnp.sum(jnp.where(lane == a_lane + hg * DELTA_HEADS + h, gcum_all, 0.0), axis=-1, keepdims=True)
              for h in heads]
        bc = [jnp.sum(jnp.where(lane == b_lane + hg * DELTA_HEADS + h, beta_all, 0.0), axis=-1, keepdims=True)
              for h in heads]
        gr = [jnp.sum(jnp.where(diag, jnp.broadcast_to(g, (c, c)), 0.0), axis=0, keepdims=True) for g in gc]
        g_last = [jnp.sum(jnp.where(ii[:, 0:1] == c - 1, g, 0.0), axis=0, keepdims=True) for g in gc]
        gam = [jnp.exp(jnp.where(incl, g - r, -jnp.inf)) for g, r in zip(gc, gr)]
        eg = [jnp.exp(g) for g in gc]
        qh = [qn_ref[pl.ds(c0, c), cs] for cs in cols]
        kh = [kn_ref[pl.ds(c0, c), cs] for cs in cols]
        vh = [vv_ref[pl.ds(c0, c), cs] for cs in cols]
        kb = [k * b for k, b in zip(kh, bc)]
        a = [jnp.where(strict, _dot_nt(x.astype(BF16), k.astype(BF16)) * gm, 0.0) for x, k, gm in zip(kb, kh, gam)]
        qk = [_dot_nt(q.astype(BF16), k.astype(BF16)) * gm for q, k, gm in zip(qh, kh, gam)]
        tinv = _unit_lower_inverse(a, c)
        u = [_bdot(t, v * b) for t, v, b in zip(tinv, vh, bc)]
        w = [_bdot(t, x * e) for t, x, e in zip(tinv, kb, eg)]
        s = [s_ref[h] for h in heads]
        v_new = [x - _bdot(y, st) for x, y, st in zip(u, w, s)]
        o = [_bdot(q * e, st) + _bdot(m, vn) for q, e, st, m, vn in zip(qh, eg, s, qk, v_new)]
        for h in heads:
            kd = (kh[h] * jnp.exp(g_last[h] - gc[h])).astype(BF16)
            s_ref[h] = s[h] * jnp.exp(g_last[h]) + lax.dot_general(
                kd, v_new[h].astype(BF16), (((0,), (0,)), ((), ())), preferred_element_type=F32)
        for h in heads:
            on = o[h] * lax.rsqrt(jnp.mean(o[h] * o[h], axis=-1, keepdims=True) + EPS) * dn_ref[...]
            zh = z_ref[pl.ds(c0, c), cols[h]]
            o_ref[pl.ds(c0, c), cols[h]] = (on * (zh * jax.nn.sigmoid(zh))).astype(o_ref.dtype)
        return carry

    lax.fori_loop(0, tb // c, chunk_step, 0)

    @pl.when(tblk == pl.num_programs(2) - 1)
    def _():
        sfin_ref[0] = s_ref[...]


def gated_delta(proj, row0, b, t, s_conv, s_delta, conv_w, a_log, dt_bias, d_norm):
    c = CHUNK if t % CHUNK == 0 else t
    tb = 512 if t % 512 == 0 else t
    assert tb % c == 0 and c % INV_BASE == 0 and tb >= HIST_ROWS and row0 % tb == 0
    nt = t // tb
    rb = row0 // tb
    ngrp = H_B // DELTA_HEADS
    qcol = COL["qkv_b"][0] // DELTA_W
    hist = jnp.concatenate([jnp.zeros((b, HIST_ROWS - (CONV_W - 1), CONV_CH), F32), s_conv], axis=1)
    par = jnp.zeros((8, LANE), F32)
    a_lane = COL["a_b"][0] - COL["k_i"][0]
    par = par.at[0, a_lane:a_lane + H_B].set(a_log).at[1, a_lane:a_lane + H_B].set(dt_bias)

    def xspec(part):
        return pl.BlockSpec((tb, DELTA_W), lambda bi, g, k: (rb + bi * nt + k, qcol + part * ngrp + g))

    def wspec(part):
        return pl.BlockSpec((CONV_W, DELTA_W), lambda bi, g, k: (0, part * ngrp + g))

    def hspec(part):
        return pl.BlockSpec((1, HIST_ROWS, DELTA_W), lambda bi, g, k: (bi, 0, part * ngrp + g))

    sspec = pl.BlockSpec((1, DELTA_HEADS, DK_B, DV_B), lambda bi, g, k: (bi, g, 0, 0))
    return pl.pallas_call(
        partial(_delta_kernel, tb=tb, c=c),
        grid=(b, ngrp, nt),
        in_specs=[xspec(0), xspec(1), xspec(2),
                  pl.BlockSpec((tb, DELTA_W), lambda bi, g, k: (rb + bi * nt + k, COL["z_b"][0] // DELTA_W + g)),
                  pl.BlockSpec((tb, LANE), lambda bi, g, k: (rb + bi * nt + k, COL["k_i"][0] // LANE)),
                  wspec(0), wspec(1), wspec(2), hspec(0), hspec(1), hspec(2),
                  pl.BlockSpec((8, LANE), lambda bi, g, k: (0, 0)),
                  pl.BlockSpec((1, DV_B), lambda bi, g, k: (0, 0)),
                  sspec],
        out_specs=[pl.BlockSpec((tb, DELTA_W), lambda bi, g, k: (bi * nt + k, g)), sspec],
        out_shape=[jax.ShapeDtypeStruct((b * t, W_B), BF16),
                   jax.ShapeDtypeStruct((b, H_B, DK_B, DV_B), F32)],
        scratch_shapes=[pltpu.VMEM((tb + HIST_ROWS, DELTA_W), F32)] * 3
        + [pltpu.VMEM((tb, DELTA_W), F32)] * 3
        + [pltpu.VMEM((tb, LANE), F32)] * 2
        + [pltpu.VMEM((DELTA_HEADS, DK_B, DV_B), F32)],
        compiler_params=_cparams(("parallel", "parallel", "arbitrary")),
        name="gated_delta",
    )(proj, proj, proj, proj, proj, conv_w, conv_w, conv_w, hist, hist, hist, par, d_norm.reshape(1, DV_B), s_delta)


def t5_bucket(rel):
    half = NUM_BUCKETS // 2
    exact = half // 2
    n = jnp.abs(rel)
    large = exact + (jnp.log(jnp.maximum(n, 1).astype(jnp.float32) / exact)
                     / math.log(MAX_DISTANCE / exact) * (half - exact)).astype(jnp.int32)
    large = jnp.minimum(large, half - 1)
    return jnp.where(rel > 0, half, 0) + jnp.where(n < exact, n, large)


KEY_TILE = LANE
SCORE_TILE = 2 * KEY_TILE
N_BIAS_NEAR = 6
MASKED = -1e30
INT32_MIN = -2 ** 31
GROUP = H_C // H_KV


def _dot_nt(a, b):
    return lax.dot_general(a, b, (((1,), (1,)), ((), ())), preferred_element_type=F32)


def _dsa_kernel(qc_ref, qi_ref, small_ref, k_ref, v_ref, kidx_ref, bias_ref, o_ref,
                key_ref, qis_ref, wb_ref, qg_ref, acc_ref, m_ref, l_ref, *, tq, q0_tile, seq_len, topk):
    qt = q0_tile + pl.program_id(1)
    n_vis = qt + 1
    row = lax.broadcasted_iota(jnp.int32, (tq, SCORE_TILE), 0)
    lane = lax.broadcasted_iota(jnp.int32, (tq, SCORE_TILE), 1)
    q_chunk = (qt * KEY_TILE + row) // CHUNK
    pad_rows = KEY_TILE - tq

    qi = (qi_ref[...] * D_IDX ** -0.5).astype(BF16)
    w = small_ref[:, D_IDX:D_IDX + H_IDX] * H_IDX ** -0.5
    for i in range(H_IDX // 2):
        qis_ref[i * tq:(i + 1) * tq, :] = qi[:, i * LANE:(i + 1) * LANE]
        for j in range(2):
            wb_ref[i * tq:(i + 1) * tq, j * SCORE_TILE:(j + 1) * SCORE_TILE] = jnp.broadcast_to(
                w[:, 2 * i + j:2 * i + j + 1], (tq, SCORE_TILE))
    first_head = lax.broadcasted_iota(jnp.int32, (SCORE_TILE, LANE), 1) < D_IDX

    def score_tile(u, carry):
        start = pl.multiple_of(u * SCORE_TILE, SCORE_TILE)
        kd = kidx_ref[0, pl.ds(start, SCORE_TILE), :]
        zero = jnp.zeros_like(kd)
        kt2 = jnp.concatenate([jnp.where(first_head, kd, zero), jnp.where(first_head, zero, kd)], axis=0)
        s = _dot_nt(qis_ref[...], kt2)
        r = jnp.maximum(s, 0.0) * wb_ref[...]
        r = jnp.sum(r.reshape(H_IDX // 2, tq, 2 * SCORE_TILE), axis=0)
        score = r[:, :SCORE_TILE] + r[:, SCORE_TILE:] + 0.0
        bits = pltpu.bitcast(score, jnp.int32)
        key = jnp.where(bits < 0, bits ^ jnp.int32(0x7FFFFFFF), bits)
        kpos = start + lane
        visible = ((kpos // CHUNK) <= q_chunk) & (kpos < seq_len)
        key = jnp.where(visible, key, jnp.int32(INT32_MIN))
        if pad_rows:
            key = jnp.concatenate([key, jnp.full((pad_rows, SCORE_TILE), INT32_MIN, jnp.int32)], axis=0)
        for j in range(SCORE_TILE // KEY_TILE):
            key_ref[u * (SCORE_TILE // KEY_TILE) + j] = key[:, j * KEY_TILE:(j + 1) * KEY_TILE].T
        return carry

    lax.fori_loop(0, pl.cdiv(n_vis * KEY_TILE, SCORE_TILE), score_tile, 0)

    def count_keys(pred):
        per_step = SCORE_TILE // KEY_TILE

        def count_tiles(u, cnt):
            for j in range(per_step):
                t = u * per_step + j
                kpos = t * KEY_TILE + lax.broadcasted_iota(jnp.int32, (KEY_TILE, KEY_TILE), 0)
                cnt = cnt + jnp.where(pred(key_ref[t], kpos), 1.0, 0.0)
            return cnt

        cnt = lax.fori_loop(0, pl.cdiv(n_vis, per_step), count_tiles, jnp.zeros((KEY_TILE, KEY_TILE), F32))
        return jnp.sum(cnt, axis=0, keepdims=True)

    thr = jnp.full((1, KEY_TILE), INT32_MIN, jnp.int32)
    for bit in range(31, -1, -1):
        cand = thr + jnp.int32(-2 ** 31 if bit == 31 else 2 ** bit)
        thr = jnp.where(count_keys(lambda key, kpos, cand=cand: key >= cand) >= float(topk), cand, thr)
    thr = jnp.maximum(thr, jnp.int32(INT32_MIN + 1))

    n_above = count_keys(lambda key, kpos: key > thr)
    n_tied = count_keys(lambda key, kpos: key == thr)
    ties_wanted = float(topk) - n_above
    pos_bits = (key_ref.shape[0] * KEY_TILE - 1).bit_length()

    def find_last_tie():
        last = jnp.zeros((1, KEY_TILE), jnp.int32)
        for bit in range(pos_bits - 1, -1, -1):
            cand = last | jnp.int32(2 ** bit)
            before = count_keys(lambda key, kpos, cand=cand: (key == thr) & (kpos < cand))
            last = jnp.where(before < ties_wanted, cand, last)
        return last

    surplus = jnp.max(n_tied - ties_wanted) > 0.0
    last_tie = lax.cond(surplus, find_last_tie, lambda: jnp.full((1, KEY_TILE), 2 ** pos_bits - 1, jnp.int32))
    thr_g = jnp.concatenate([thr] * GROUP, axis=1)
    last_tie_g = jnp.concatenate([last_tie] * GROUP, axis=1)

    kvs = range(H_KV)
    for kv in kvs:
        for g in range(GROUP):
            c0 = (kv * GROUP + g) * DH_C
            qh = (qc_ref[:, c0:c0 + DH_C] * DH_C ** -0.5).astype(BF16)
            if pad_rows:
                qh = jnp.concatenate([qh, jnp.zeros((pad_rows, DH_C), BF16)], axis=0)
            qg_ref[kv, g * KEY_TILE:(g + 1) * KEY_TILE, :] = qh
    acc_ref[...] = jnp.zeros(acc_ref.shape, F32)
    m_ref[...] = jnp.full(m_ref.shape, MASKED, F32)
    l_ref[...] = jnp.zeros(l_ref.shape, F32)

    def attend_tile(t, carry):
        start = pl.multiple_of(t * KEY_TILE, KEY_TILE)
        key_g = jnp.concatenate([key_ref[t]] * GROUP, axis=1)
        kpos = start + lax.broadcasted_iota(jnp.int32, (KEY_TILE, GROUP * KEY_TILE), 0)
        sel = (key_g > thr_g) | ((key_g == thr_g) & (kpos <= last_tie_g))
        dd = jnp.minimum(qt - t, N_BIAS_NEAR)
        s = [_dot_nt(k_ref[0, pl.ds(start, KEY_TILE), kv * DH_C:(kv + 1) * DH_C], qg_ref[kv]) for kv in kvs]
        sm = [jnp.where(sel, s[kv] + bias_ref[dd, kv], MASKED) for kv in kvs]
        m_old = [m_ref[kv:kv + 1, :] for kv in kvs]
        m_new = [jnp.maximum(m_old[kv], jnp.max(sm[kv], axis=0, keepdims=True)) for kv in kvs]
        alpha = [jnp.exp(m_old[kv] - m_new[kv]) for kv in kvs]
        p = [jnp.exp(sm[kv] - m_new[kv]) for kv in kvs]
        pv = [lax.dot_general(v_ref[0, pl.ds(start, KEY_TILE), kv * DH_C:(kv + 1) * DH_C], p[kv].astype(BF16),
                              (((0,), (0,)), ((), ())), preferred_element_type=F32) for kv in kvs]
        for kv in kvs:
            l_ref[kv:kv + 1, :] = alpha[kv] * l_ref[kv:kv + 1, :] + jnp.sum(p[kv], axis=0, keepdims=True)
            acc_ref[kv] = alpha[kv] * acc_ref[kv] + pv[kv]
            m_ref[kv:kv + 1, :] = m_new[kv]
        return carry

    lax.fori_loop(0, n_vis, attend_tile, 0)
    for kv in kvs:
        out = (acc_ref[kv] / l_ref[kv:kv + 1, :]).T
        for g in range(GROUP):
            c0 = (kv * GROUP + g) * DH_C
            o_ref[:, c0:c0 + DH_C] = out[g * KEY_TILE:g * KEY_TILE + tq].astype(o_ref.dtype)


def dsa_bias_tiles(rel_bias):
    i = jnp.arange(KEY_TILE)
    d = jnp.arange(N_BIAS_NEAR + 1)
    rel = (i[None, :, None] - i[None, None, :]) - KEY_TILE * d[:, None, None]
    onehot = (t5_bucket(rel)[..., None] == jnp.arange(NUM_BUCKETS)).astype(F32)
    tiles = jnp.einsum('dkqn,nh->dkqh', onehot, rel_bias.astype(F32), precision=lax.Precision.HIGHEST)
    tiles = tiles.reshape(N_BIAS_NEAR + 1, KEY_TILE, KEY_TILE, H_KV, GROUP)
    return jnp.transpose(tiles, (0, 3, 1, 4, 2)).reshape(N_BIAS_NEAR + 1, H_KV, KEY_TILE, GROUP * KEY_TILE)


def dsa_attention(proj, row0, b, t, k_all, v_all, kidx_all, bias_tiles, p_len, seq_len, topk):
    tq = KEY_TILE if t % KEY_TILE == 0 else t
    assert p_len % KEY_TILE == 0 and KEY_TILE % tq == 0 and (tq == KEY_TILE or t == tq) and row0 % tq == 0
    nb = t // tq
    lp = k_all.shape[1]
    assert lp % SCORE_TILE == 0 and lp >= p_len + nb * KEY_TILE
    assert 8 * (MAX_DISTANCE / 8) ** (7 / 8) < KEY_TILE * N_BIAS_NEAR - (KEY_TILE - 1)
    rb = row0 // tq
    kern = partial(_dsa_kernel, tq=tq, q0_tile=p_len // KEY_TILE, seq_len=seq_len, topk=topk)
    return pl.pallas_call(
        kern,
        grid=(b, nb),
        in_specs=[pl.BlockSpec((tq, W_C), lambda bi, j: (rb + bi * nb + j, COL["q_c"][0] // W_C)),
                  pl.BlockSpec((tq, H_IDX * D_IDX), lambda bi, j: (rb + bi * nb + j, COL["q_i"][0] // (H_IDX * D_IDX))),
                  pl.BlockSpec((tq, LANE), lambda bi, j: (rb + bi * nb + j, COL["k_i"][0] // LANE)),
                  pl.BlockSpec((1, lp, H_KV * DH_C), lambda bi, j: (bi, 0, 0)),
                  pl.BlockSpec((1, lp, H_KV * DH_C), lambda bi, j: (bi, 0, 0)),
                  pl.BlockSpec((1, lp, 2 * D_IDX), lambda bi, j: (bi, 0, 0)),
                  pl.BlockSpec((N_BIAS_NEAR + 1, H_KV, KEY_TILE, GROUP * KEY_TILE), lambda bi, j: (0, 0, 0, 0))],
        out_specs=pl.BlockSpec((tq, W_C), lambda bi, j: (bi * nb + j, 0)),
        out_shape=jax.ShapeDtypeStruct((b * t, W_C), BF16),
        scratch_shapes=[pltpu.VMEM((lp // KEY_TILE, KEY_TILE, KEY_TILE), jnp.int32),
                        pltpu.VMEM((H_IDX // 2 * tq, 2 * D_IDX), BF16),
                        pltpu.VMEM((H_IDX // 2 * tq, 2 * SCORE_TILE), F32),
                        pltpu.VMEM((H_KV, GROUP * KEY_TILE, DH_C), BF16),
                        pltpu.VMEM((H_KV, DH_C, GROUP * KEY_TILE), F32),
                        pltpu.VMEM((8, GROUP * KEY_TILE), F32),
                        pltpu.VMEM((8, GROUP * KEY_TILE), F32)],
        compiler_params=_cparams(("parallel", "arbitrary")),
        name="dsa_attention",
    )(proj, proj, proj, k_all, v_all, kidx_all, bias_tiles)


def branches(proj_all, new_rows, row0, b, t, past_k, past_v, past_kidx, s_ret, s_delta, s_conv,
             ret_gn, conv_w, a_log, dt_bias, d_norm, bias_tiles):
    p_len = past_k.shape[1]
    seq_len = p_len + t
    topk = min(TOPK_MAX, seq_len // 4)
    lp = -(-seq_len // SCORE_TILE) * SCORE_TILE
    k_c, v_c, k_i, k_bf, v_bf, ki_bf = (x[row0:row0 + b * t].reshape(b, t, x.shape[-1]) for x in new_rows)

    o, w = COL["qkv_b"]
    qkv_tail = jnp.stack([proj_all[row0 + t - (CONV_W - 1) + j:row0 + b * t:t, o:o + w]
                          for j in range(CONV_W - 1)], axis=1)
    conv_new = jnp.concatenate([s_conv, qkv_tail], axis=1)[:, -(CONV_W - 1):]

    o_a, ret_new = retention(proj_all, row0, b, t, p_len, s_ret, ret_gn)
    o_b, delta_new = gated_delta(proj_all, row0, b, t, s_conv, s_delta, conv_w, a_log, dt_bias, d_norm)

    def all_keys(past, new, reps=1):
        if p_len == 0 and lp == seq_len:
            return new
        w = past.shape[-1] * reps
        past = jnp.tile(past.reshape(b, p_len, -1).astype(BF16), (1, 1, reps))
        return jnp.concatenate([past, new, jnp.zeros((b, lp - seq_len, w), BF16)], axis=1)

    o_c = dsa_attention(proj_all, row0, b, t, all_keys(past_k.reshape(b, p_len, KV_W), k_bf),
                        all_keys(past_v.reshape(b, p_len, KV_W), v_bf), all_keys(past_kidx, ki_bf, 2),
                        bias_tiles, p_len, seq_len, topk)
    caches = (k_c.reshape(b, t, H_KV, DH_C), v_c.reshape(b, t, H_KV, DH_C), k_i, ret_new, delta_new, conv_new)
    return jnp.stack([o_a, o_b, o_c]), caches


def _prep_w_in(w):
    return jnp.concatenate([w[:, 0:14336], w[:, 14368:16416], w[:, 17440:19488], w[:, 16416:17440],
                            w[:, 19584:31872], w[:, 19488:19584], w[:, 14336:14368],
                            jnp.zeros((w.shape[0], D_IN_PAD - D_IN), w.dtype)], axis=1).astype(BF16)


def kernel(x_prompt, x_sample, cache_k, cache_v, cache_kidx, state_ret, state_delta, state_conv, norm_mix, w_in, ret_gn, conv_w, delta_a_log, delta_dt_bias, delta_norm, rel_bias, w_branch_a, w_branch_b, w_branch_c, w_out, norm_ffn, w_ffn_gate, w_ffn_up, w_ffn_down, norm_final):
    dt = x_prompt.dtype
    bp, tp, _ = x_prompt.shape
    bs, ts, _ = x_sample.shape
    depth = w_in.shape[0]
    n_p = bp * tp
    zk = jnp.zeros((bp, 0, H_KV, DH_C), dt)
    zkidx = jnp.zeros((bp, 0, D_IDX), dt)
    zret = jnp.zeros((bp, H_A, DK_A, DV_A), dt)
    zdelta = jnp.zeros((bp, H_B, DK_B, DV_B), dt)
    zconv = jnp.zeros((bp, CONV_W - 1, CONV_CH), dt)

    h = jnp.concatenate([x_prompt.reshape(n_p, D_MODEL), x_sample.reshape(bs * ts, D_MODEL)], axis=0)
    new_p, new_s = [], []
    bias_tiles = dsa_bias_tiles(rel_bias)
    for l in range(depth):
        u = rmsnorm(h, norm_mix[l], BF16)
        proj, *new_rows = matmul_in(u, _prep_w_in(w_in[l]))
        bw = (ret_gn[l], conv_w[l], delta_a_log[l], delta_dt_bias[l], delta_norm[l], bias_tiles)
        o_p, sp = branches(proj, new_rows, 0, bp, tp, zk, zk, zkidx, zret, zdelta, zconv, *bw)
        o_s, ss = branches(proj, new_rows, n_p, bs, ts, cache_k[l], cache_v[l], cache_kidx[l],
                           state_ret[l], state_delta[l], state_conv[l], *bw)
        new_p.append(sp)
        new_s.append(ss)
        o_abc = jnp.concatenate([o_p, o_s], axis=1)
        w_br = jnp.stack([w_branch_a[l], w_branch_b[l], w_branch_c[l]]).astype(BF16)
        merged = merge_branches(o_abc, w_br, proj)
        h = matmul_residual(merged, w_out[l].astype(BF16), h)
        u2 = rmsnorm(h, norm_ffn[l], BF16)
        act = ffn_gate_up(u2, w_ffn_gate[l].astype(BF16), w_ffn_up[l].astype(BF16))
        h = ffn_down(act, w_ffn_down[l].astype(BF16), h)

    y = rmsnorm(h, norm_final, F32)
    y_prompt = y[:n_p].reshape(bp, tp, D_MODEL)
    y_sample = y[n_p:].reshape(bs, ts, D_MODEL)
    k_p, v_p, kidx_p, ret_p, delta_p, conv_p = [jnp.stack([s[i] for s in new_p]) for i in range(6)]
    k_s, v_s, kidx_s, ret_s, delta_s, conv_s = [jnp.stack([s[i] for s in new_s]) for i in range(6)]
    return (y_prompt, y_sample, k_p, v_p, kidx_p, ret_p, delta_p, conv_p,
            k_s, v_s, kidx_s, ret_s, delta_s, conv_s)
```

```python
import math
from functools import partial

import jax
import jax.numpy as jnp
from jax import lax
from jax.experimental import pallas as pl
from jax.experimental.pallas import tpu as pltpu

F32 = jnp.float32
BF16 = jnp.bfloat16

D_MODEL = 4096
CHUNK = 64
EPS = 1e-6
H_A, DK_A, DV_A = 8, 128, 256
W_A = H_A * DV_A
ROPE_BASE = 10000.0
H_B, DK_B, DV_B = 16, 128, 128
W_B = H_B * DV_B
CONV_W = 4
CONV_CH = 2 * H_B * DK_B + H_B * DV_B
H_C, H_KV, DH_C = 16, 4, 128
W_C = H_C * DH_C
KV_W = H_KV * DH_C
H_IDX, D_IDX = 32, 64
TOPK_MAX = 256
NUM_BUCKETS = 32
MAX_DISTANCE = 1024

LANE = 128
SUBLANE = 8
VMEM_LIMIT_V7X = 56 * 1024 * 1024
TM = 1024
TN = 1024
TIME_BLOCK = 512

_MODEL_ORDER = [("q_a", H_A * DK_A), ("k_a", H_A * DK_A), ("v_a", W_A), ("g_a", W_A),
                ("qkv_b", CONV_CH), ("z_b", W_B), ("a_b", H_B), ("b_b", H_B),
                ("q_c", W_C), ("k_c", KV_W), ("v_c", KV_W), ("q_i", H_IDX * D_IDX), ("k_i", D_IDX), ("w_i", H_IDX),
                ("gate_a", D_MODEL), ("gate_b", D_MODEL), ("gate_c", D_MODEL)]
_KERNEL_ORDER = ["q_a", "k_a", "v_a", "g_a", "qkv_b", "z_b", "q_c", "q_i", "k_c", "v_c",
                 "gate_a", "gate_b", "gate_c", "k_i", "w_i", "a_b", "b_b"]


def _offsets(order, widths):
    table, off = {}, 0
    for name in order:
        table[name] = (off, widths[name])
        off += widths[name]
    return table, off


_WIDTH = dict(_MODEL_ORDER)
MODEL_COL, D_IN = _offsets([n for n, _ in _MODEL_ORDER], _WIDTH)
COL, _ = _offsets(_KERNEL_ORDER, _WIDTH)
D_IN_PAD = -(-D_IN // TN) * TN


def _cparams(sem):
    return pltpu.CompilerParams(dimension_semantics=sem, vmem_limit_bytes=VMEM_LIMIT_V7X)


def _rmsnorm_kernel(x_ref, g_ref, o_ref):
    x = x_ref[...]
    y = x * lax.rsqrt(jnp.mean(x * x, axis=-1, keepdims=True) + EPS)
    o_ref[...] = (y * g_ref[...]).astype(o_ref.dtype)


def rmsnorm(x, g, out_dtype, tr=512):
    m, d = x.shape
    return pl.pallas_call(
        _rmsnorm_kernel,
        grid=(m // tr,),
        in_specs=[pl.BlockSpec((tr, d), lambda i: (i, 0)), pl.BlockSpec((1, d), lambda i: (0, 0))],
        out_specs=pl.BlockSpec((tr, d), lambda i: (i, 0)),
        out_shape=jax.ShapeDtypeStruct((m, d), out_dtype),
        compiler_params=_cparams(("parallel",)),
        name="rmsnorm",
    )(x, g.reshape(1, d))


def _mm_in_kernel(a_ref, b_ref, o_ref, k_ref, v_ref, ki_ref, kb_ref, vb_ref, kib_ref, *, kv_block, small_block):
    j = pl.program_id(1)
    o_ref[...] = jnp.dot(a_ref[...], b_ref[...], preferred_element_type=F32)

    @pl.when(j == kv_block)
    def _():
        k, v = o_ref[:, :KV_W], o_ref[:, KV_W:2 * KV_W]
        k_ref[...] = k
        v_ref[...] = v
        kb_ref[...] = k.astype(BF16)
        vb_ref[...] = v.astype(BF16)

    @pl.when(j == small_block)
    def _():
        ki = o_ref[:, :D_IDX]
        ki_ref[...] = ki
        kib_ref[...] = jnp.concatenate([ki, ki], axis=1).astype(BF16)


def matmul_in(a, b, tn=TN):
    m, k = a.shape
    n = b.shape[1]
    assert COL["k_c"][0] % tn == 0 and COL["v_c"][0] == COL["k_c"][0] + KV_W and COL["k_i"][0] % tn == 0
    kern = partial(_mm_in_kernel, kv_block=COL["k_c"][0] // tn, small_block=COL["k_i"][0] // tn)

    def row_spec(w):
        return pl.BlockSpec((TM, w), lambda i, j: (i, 0))

    return pl.pallas_call(
        kern,
        grid=(m // TM, n // tn),
        in_specs=[pl.BlockSpec((TM, k), lambda i, j: (i, 0), pipeline_mode=pl.Buffered(1)),
                  pl.BlockSpec((k, tn), lambda i, j: (0, j))],
        out_specs=[pl.BlockSpec((TM, tn), lambda i, j: (i, j)), row_spec(KV_W), row_spec(KV_W), row_spec(D_IDX),
                   row_spec(KV_W), row_spec(KV_W), row_spec(2 * D_IDX)],
        out_shape=[jax.ShapeDtypeStruct((m, n), F32), jax.ShapeDtypeStruct((m, KV_W), F32),
                   jax.ShapeDtypeStruct((m, KV_W), F32), jax.ShapeDtypeStruct((m, D_IDX), F32),
                   jax.ShapeDtypeStruct((m, KV_W), BF16), jax.ShapeDtypeStruct((m, KV_W), BF16),
                   jax.ShapeDtypeStruct((m, 2 * D_IDX), BF16)],
        compiler_params=_cparams(("parallel", "arbitrary")),
        name="matmul_in",
    )(a, b)


def _merge_kernel(o_ref, w_ref, g_ref, out_ref, acc_ref):
    br = pl.program_id(2)
    y = jnp.dot(o_ref[0], w_ref[0], preferred_element_type=F32) * jax.nn.sigmoid(g_ref[...])

    @pl.when(br == 0)
    def _():
        acc_ref[...] = y

    @pl.when(br != 0)
    def _():
        acc_ref[...] += y

    @pl.when(br == pl.num_programs(2) - 1)
    def _():
        out_ref[...] = acc_ref[...].astype(out_ref.dtype)


def merge_branches(o_abc, w_br, proj, tn=TN):
    _, m, k = o_abc.shape
    n = w_br.shape[2]
    gate0 = COL["gate_a"][0] // tn
    nj = n // tn
    return pl.pallas_call(
        _merge_kernel,
        grid=(m // TM, nj, 3),
        in_specs=[pl.BlockSpec((1, TM, k), lambda i, j, r: (r, i, 0)),
                  pl.BlockSpec((1, k, tn), lambda i, j, r: (r, 0, j)),
                  pl.BlockSpec((TM, tn), lambda i, j, r: (i, gate0 + r * nj + j))],
        out_specs=pl.BlockSpec((TM, tn), lambda i, j, r: (i, j)),
        out_shape=jax.ShapeDtypeStruct((m, n), BF16),
        scratch_shapes=[pltpu.VMEM((TM, tn), F32)],
        compiler_params=_cparams(("parallel", "parallel", "arbitrary")),
        name="merge_branches",
    )(o_abc, w_br, proj)


def _mm_res_kernel(a_ref, b_ref, h_ref, o_ref):
    o_ref[...] = h_ref[...] + jnp.dot(a_ref[...], b_ref[...], preferred_element_type=F32)


def matmul_residual(a, b, h, tn=TN):
    m, k = a.shape
    n = b.shape[1]
    return pl.pallas_call(
        _mm_res_kernel,
        grid=(m // TM, n // tn),
        in_specs=[pl.BlockSpec((TM, k), lambda i, j: (i, 0)), pl.BlockSpec((k, tn), lambda i, j: (0, j)),
                  pl.BlockSpec((TM, tn), lambda i, j: (i, j))],
        out_specs=pl.BlockSpec((TM, tn), lambda i, j: (i, j)),
        out_shape=jax.ShapeDtypeStruct((m, n), F32),
        compiler_params=_cparams(("parallel", "parallel")),
        name="matmul_out",
    )(a, b, h)


def _gateup_kernel(u_ref, wg_ref, wu_ref, o_ref):
    u = u_ref[...]
    g = jnp.dot(u, wg_ref[...], preferred_element_type=F32)
    up = jnp.dot(u, wu_ref[...], preferred_element_type=F32)
    o_ref[...] = (jax.nn.silu(g) * up).astype(o_ref.dtype)


def ffn_gate_up(u, wg, wu, tn=512):
    m, k = u.shape
    n = wg.shape[1]
    return pl.pallas_call(
        _gateup_kernel,
        grid=(m // TM, pl.cdiv(n, tn)),
        in_specs=[pl.BlockSpec((TM, k), lambda i, j: (i, 0)),
                  pl.BlockSpec((k, tn), lambda i, j: (0, j)),
                  pl.BlockSpec((k, tn), lambda i, j: (0, j))],
        out_specs=pl.BlockSpec((TM, tn), lambda i, j: (i, j)),
        out_shape=jax.ShapeDtypeStruct((m, n), BF16),
        compiler_params=_cparams(("parallel", "parallel")),
        name="ffn_gate_up",
    )(u, wg, wu)


def ffn_down(a, b, h, tm=768, tn=512):
    m, k = a.shape
    n = b.shape[1]
    return pl.pallas_call(
        _mm_res_kernel,
        grid=(m // tm, n // tn),
        in_specs=[pl.BlockSpec((tm, k), lambda i, j: (i, 0), pipeline_mode=pl.Buffered(1)),
                  pl.BlockSpec((k, tn), lambda i, j: (0, j)),
                  pl.BlockSpec((tm, tn), lambda i, j: (i, j))],
        out_specs=pl.BlockSpec((tm, tn), lambda i, j: (i, j)),
        out_shape=jax.ShapeDtypeStruct((m, n), F32),
        compiler_params=_cparams(("parallel", "parallel")),
        name="ffn_down",
    )(a, b, h)


RET_HEADS = 4


def _retention_kernel(q_ref, k_ref, v_ref, g_ref, cos_ref, sin_ref, idec_ref, qdec_ref, kdec_ref, cdec_ref, gn_ref,
                      s0_ref, o_ref, sfin_ref, s_ref, *, tb, c):
    tblk = pl.program_id(2)
    heads = range(RET_HEADS)

    @pl.when(tblk == 0)
    def _():
        s_ref[...] = s0_ref[0]

    def rotate(x, cos, sin):
        return x * cos + pltpu.roll(x, DK_A // 2, 1) * sin

    def chunk_step(ci, carry):
        c0 = pl.multiple_of(ci * c, c)
        cos = cos_ref[pl.ds(c0, c), :]
        sin = sin_ref[pl.ds(c0, c), :]
        qr = [rotate(q_ref[pl.ds(c0, c), h * DK_A:(h + 1) * DK_A], cos, sin) for h in heads]
        kr = [rotate(k_ref[pl.ds(c0, c), h * DK_A:(h + 1) * DK_A], cos, sin) * DK_A ** -0.5 for h in heads]
        v = [v_ref[pl.ds(c0, c), h * DV_A:(h + 1) * DV_A].astype(BF16) for h in heads]
        s = [s_ref[h] for h in heads]
        att = [_dot_nt(qr[h].astype(BF16), kr[h].astype(BF16)) * idec_ref[h] for h in heads]
        o = [_bdot(att[h], v[h]) + _bdot(qr[h] * qdec_ref[h], s[h]) for h in heads]
        for h in heads:
            s_ref[h] = s[h] * cdec_ref[h, 0:1, :] + lax.dot_general(
                (kr[h] * kdec_ref[h]).astype(BF16), v[h], (((0,), (0,)), ((), ())), preferred_element_type=F32)
        for h in heads:
            mu = jnp.mean(o[h], axis=-1, keepdims=True)
            d = o[h] - mu
            var = jnp.mean(d * d, axis=-1, keepdims=True)
            cols = slice(h * DV_A, (h + 1) * DV_A)
            g = g_ref[pl.ds(c0, c), cols]
            o_ref[pl.ds(c0, c), cols] = (
                g * jax.nn.sigmoid(g) * (d * lax.rsqrt(var + EPS) * gn_ref[:, cols])).astype(o_ref.dtype)
        return carry

    lax.fori_loop(0, tb // c, chunk_step, 0)

    @pl.when(tblk == pl.num_programs(2) - 1)
    def _():
        sfin_ref[0] = s_ref[...]


def retention(proj, row0, b, t, p_len, s_ret, ret_gn):
    c = 2 * CHUNK if t % (2 * CHUNK) == 0 else t
    tb = TIME_BLOCK if t % TIME_BLOCK == 0 else t
    assert tb % c == 0 and row0 % tb == 0
    nt = t // tb
    rb = row0 // tb
    half = DK_A // 2
    inv = ROPE_BASE ** (-jnp.arange(half, dtype=F32) / half)
    ang = (p_len + jnp.arange(t)).astype(F32)[:, None] * inv[None, :]
    cos2 = jnp.concatenate([jnp.cos(ang), jnp.cos(ang)], axis=1)
    sin2 = jnp.concatenate([-jnp.sin(ang), jnp.sin(ang)], axis=1)
    log_gamma = jnp.log(1.0 - 2.0 ** (-5.0 - jnp.arange(H_A, dtype=F32)))
    i = jnp.arange(c, dtype=F32)
    rel = i[:, None] - i[None, :]
    idec = jnp.exp(jnp.where(rel[None] >= 0, rel[None] * log_gamma[:, None, None], -jnp.inf))
    qdec = jnp.broadcast_to(jnp.exp((i[None, :] + 1.0) * log_gamma[:, None])[:, :, None], (H_A, c, DK_A))
    kdec = jnp.broadcast_to(jnp.exp((c - 1.0 - i[None, :]) * log_gamma[:, None])[:, :, None], (H_A, c, DK_A))
    cdec = jnp.broadcast_to(jnp.exp(c * log_gamma)[:, None, None], (H_A, SUBLANE, DV_A))
    wk, wv = RET_HEADS * DK_A, RET_HEADS * DV_A
    qb0 = COL["q_a"][0] // wk
    kb0 = COL["k_a"][0] // wk
    vb0 = COL["v_a"][0] // wv
    gb0 = COL["g_a"][0] // wv
    sspec = pl.BlockSpec((1, RET_HEADS, DK_A, DV_A), lambda bi, h, k: (bi, h, 0, 0))
    return pl.pallas_call(
        partial(_retention_kernel, tb=tb, c=c),
        grid=(b, H_A // RET_HEADS, nt),
        in_specs=[pl.BlockSpec((tb, wk), lambda bi, h, k: (rb + bi * nt + k, qb0 + h)),
                  pl.BlockSpec((tb, wk), lambda bi, h, k: (rb + bi * nt + k, kb0 + h)),
                  pl.BlockSpec((tb, wv), lambda bi, h, k: (rb + bi * nt + k, vb0 + h)),
                  pl.BlockSpec((tb, wv), lambda bi, h, k: (rb + bi * nt + k, gb0 + h)),
                  pl.BlockSpec((tb, DK_A), lambda bi, h, k: (k, 0)),
                  pl.BlockSpec((tb, DK_A), lambda bi, h, k: (k, 0)),
                  pl.BlockSpec((RET_HEADS, c, c), lambda bi, h, k: (h, 0, 0)),
                  pl.BlockSpec((RET_HEADS, c, DK_A), lambda bi, h, k: (h, 0, 0)),
                  pl.BlockSpec((RET_HEADS, c, DK_A), lambda bi, h, k: (h, 0, 0)),
                  pl.BlockSpec((RET_HEADS, SUBLANE, DV_A), lambda bi, h, k: (h, 0, 0)),
                  pl.BlockSpec((1, wv), lambda bi, h, k: (0, h)),
                  sspec],
        out_specs=[pl.BlockSpec((tb, wv), lambda bi, h, k: (bi * nt + k, h)), sspec],
        out_shape=[jax.ShapeDtypeStruct((b * t, W_A), BF16),
                   jax.ShapeDtypeStruct((b, H_A, DK_A, DV_A), F32)],
        scratch_shapes=[pltpu.VMEM((RET_HEADS, DK_A, DV_A), F32)],
        compiler_params=_cparams(("parallel", "parallel", "arbitrary")),
        name="retention",
    )(proj, proj, proj, proj, cos2, sin2, idec, qdec, kdec, cdec, ret_gn.reshape(1, W_A), s_ret)


DELTA_HEADS = 8
DELTA_W = DELTA_HEADS * DK_B
HIST_ROWS = SUBLANE
INV_BASE = 16


def _bdot(a, b):
    return jnp.dot(a.astype(BF16), b.astype(BF16), preferred_element_type=F32)


def _hdot(a, b):
    return jnp.dot(a, b, preferred_element_type=F32, precision=lax.Precision.HIGHEST)


def _split_bf16(x):
    hi = x.astype(BF16)
    return hi, (x - hi.astype(F32)).astype(BF16)


def _dot3(a, b):
    a_hi, a_lo = _split_bf16(a)
    b_hi, b_lo = _split_bf16(b)
    dot = partial(jnp.dot, preferred_element_type=F32)
    return dot(a_hi, b_hi) + (dot(a_hi, b_lo) + dot(a_lo, b_hi))


def _unit_lower_inverse(mats, c):
    ii = lax.broadcasted_iota(jnp.int32, (c, c), 0)
    jj = lax.broadcasted_iota(jnp.int32, (c, c), 1)
    eye = jnp.where(ii == jj, 1.0, 0.0).astype(F32)
    ps = [jnp.where((ii // INV_BASE) == (jj // INV_BASE), a, 0.0) for a in mats]
    ts = [eye - p for p in ps]
    for _ in range(int(math.log2(INV_BASE)) - 1):
        ps = [_dot3(p, p) for p in ps]
        ts = [_dot3(t, eye + p) for t, p in zip(ts, ps)]
    size = INV_BASE
    while size < c:
        half_blocks = ((ii // (2 * size)) == (jj // (2 * size))) & ((ii // size) != (jj // size))
        tl = [_dot3(t, jnp.where(half_blocks, a, 0.0)) for t, a in zip(ts, mats)]
        ts = [t - _dot3(x, t) for t, x in zip(ts, tl)]
        size *= 2
    return ts


def _delta_kernel(xq_ref, xk_ref, xv_ref, z_ref, small_ref, cwq_ref, cwk_ref, cwv_ref, hq_ref, hk_ref, hv_ref,
                  par_ref, dn_ref, s0_ref, o_ref, sfin_ref,
                  bq_ref, bk_ref, bv_ref, qn_ref, kn_ref, vv_ref, g_ref, beta_ref, s_ref, *, tb, c):
    hg = pl.program_id(1)
    tblk = pl.program_id(2)

    @pl.when(tblk == 0)
    def _():
        bq_ref[0:HIST_ROWS, :] = hq_ref[0]
        bk_ref[0:HIST_ROWS, :] = hk_ref[0]
        bv_ref[0:HIST_ROWS, :] = hv_ref[0]
        s_ref[...] = s0_ref[0]

    def conv_silu(x_ref, buf_ref, cw_ref):
        buf_ref[HIST_ROWS:HIST_ROWS + tb, :] = x_ref[...]
        y = jnp.zeros((tb, DELTA_W), F32)
        for j in range(CONV_W):
            lo = HIST_ROWS - (CONV_W - 1) + j
            y = y + buf_ref[lo:lo + tb, :] * cw_ref[j:j + 1, :]
        buf_ref[0:HIST_ROWS, :] = buf_ref[tb:tb + HIST_ROWS, :]
        return y * jax.nn.sigmoid(y)

    def l2n(x):
        parts = []
        for h in range(DELTA_HEADS):
            xh = x[:, h * DK_B:(h + 1) * DK_B]
            parts.append(xh * lax.rsqrt(jnp.sum(xh * xh, axis=-1, keepdims=True) + EPS))
        return jnp.concatenate(parts, axis=1)

    qn_ref[...] = l2n(conv_silu(xq_ref, bq_ref, cwq_ref)) * DK_B ** -0.5
    kn_ref[...] = l2n(conv_silu(xk_ref, bk_ref, cwk_ref))
    vv_ref[...] = conv_silu(xv_ref, bv_ref, cwv_ref)
    sm = small_ref[...]
    x = sm + par_ref[1:2, :]
    softplus = jnp.maximum(x, 0.0) + jnp.log1p(jnp.exp(-jnp.abs(x)))
    g_ref[...] = -jnp.exp(par_ref[0:1, :]) * softplus
    beta_ref[...] = jax.nn.sigmoid(sm)

    ii = lax.broadcasted_iota(jnp.int32, (c, c), 0)
    jj = lax.broadcasted_iota(jnp.int32, (c, c), 1)
    incl = ii >= jj
    strict = ii > jj
    diag = ii == jj
    tril_ones = jnp.where(incl, 1.0, 0.0).astype(F32)
    a_lane = COL["a_b"][0] - COL["k_i"][0]
    b_lane = COL["b_b"][0] - COL["k_i"][0]

    def chunk_step(ci, carry):
        c0 = pl.multiple_of(ci * c, c)
        gcum_all = _hdot(tril_ones, g_ref[pl.ds(c0, c), :])
        beta_all = beta_ref[pl.ds(c0, c), :]
        lane = lax.broadcasted_iota(jnp.int32, (c, LANE), 1)
        heads = range(DELTA_HEADS)
        cols = [slice(h * DK_B, (h + 1) * DK_B) for h in heads]
        gc = [jnp.sum(jnp.where(lane == a_lane + hg * DELTA_HEADS + h, gcum_all, 0.0), axis=-1, keepdims=True)
              for h in heads]
        bc = [jnp.sum(jnp.where(lane == b_lane + hg * DELTA_HEADS + h, beta_all, 0.0), axis=-1, keepdims=True)
              for h in heads]
        gr = [jnp.sum(jnp.where(diag, jnp.broadcast_to(g, (c, c)), 0.0), axis=0, keepdims=True) for g in gc]
        g_last = [jnp.sum(jnp.where(ii[:, 0:1] == c - 1, g, 0.0), axis=0, keepdims=True) for g in gc]
        gam = [jnp.exp(jnp.where(incl, g - r, -jnp.inf)) for g, r in zip(gc, gr)]
        eg = [jnp.exp(g) for g in gc]
        qh = [qn_ref[pl.ds(c0, c), cs] for cs in cols]
        kh = [kn_ref[pl.ds(c0, c), cs] for cs in cols]
        vh = [vv_ref[pl.ds(c0, c), cs] for cs in cols]
        kb = [k * b for k, b in zip(kh, bc)]
        a = [jnp.where(strict, _dot_nt(x.astype(BF16), k.astype(BF16)) * gm, 0.0) for x, k, gm in zip(kb, kh, gam)]
        qk = [_dot_nt(q.astype(BF16), k.astype(BF16)) * gm for q, k, gm in zip(qh, kh, gam)]
        tinv = _unit_lower_inverse(a, c)
        u = [_bdot(t, v * b) for t, v, b in zip(tinv, vh, bc)]
        w = [_bdot(t, x * e) for t, x, e in zip(tinv, kb, eg)]
        s = [s_ref[h] for h in heads]
        v_new = [x - _bdot(y, st) for x, y, st in zip(u, w, s)]
        o = [_bdot(q * e, st) + _bdot(m, vn) for q, e, st, m, vn in zip(qh, eg, s, qk, v_new)]
        for h in heads:
            kd = (kh[h] * jnp.exp(g_last[h] - gc[h])).astype(BF16)
            s_ref[h] = s[h] * jnp.exp(g_last[h]) + lax.dot_general(
                kd, v_new[h].astype(BF16), (((0,), (0,)), ((), ())), preferred_element_type=F32)
        for h in heads:
            on = o[h] * lax.rsqrt(jnp.mean(o[h] * o[h], axis=-1, keepdims=True) + EPS) * dn_ref[...]
            zh = z_ref[pl.ds(c0, c), cols[h]]
            o_ref[pl.ds(c0, c), cols[h]] = (on * (zh * jax.nn.sigmoid(zh))).astype(o_ref.dtype)
        return carry

    lax.fori_loop(0, tb // c, chunk_step, 0)

    @pl.when(tblk == pl.num_programs(2) - 1)
    def _():
        sfin_ref[0] = s_ref[...]


def gated_delta(proj, row0, b, t, s_conv, s_delta, conv_w, a_log, dt_bias, d_norm):
    c = CHUNK if t % CHUNK == 0 else t
    tb = TIME_BLOCK if t % TIME_BLOCK == 0 else t
    assert tb % c == 0 and c % INV_BASE == 0 and tb >= HIST_ROWS and row0 % tb == 0
    nt = t // tb
    rb = row0 // tb
    ngrp = H_B // DELTA_HEADS
    qcol = COL["qkv_b"][0] // DELTA_W
    hist = jnp.concatenate([jnp.zeros((b, HIST_ROWS - (CONV_W - 1), CONV_CH), F32), s_conv], axis=1)
    par = jnp.zeros((SUBLANE, LANE), F32)
    a_lane = COL["a_b"][0] - COL["k_i"][0]
    par = par.at[0, a_lane:a_lane + H_B].set(a_log).at[1, a_lane:a_lane + H_B].set(dt_bias)

    def xspec(part):
        return pl.BlockSpec((tb, DELTA_W), lambda bi, g, k: (rb + bi * nt + k, qcol + part * ngrp + g))

    def wspec(part):
        return pl.BlockSpec((CONV_W, DELTA_W), lambda bi, g, k: (0, part * ngrp + g))

    def hspec(part):
        return pl.BlockSpec((1, HIST_ROWS, DELTA_W), lambda bi, g, k: (bi, 0, part * ngrp + g))

    sspec = pl.BlockSpec((1, DELTA_HEADS, DK_B, DV_B), lambda bi, g, k: (bi, g, 0, 0))
    return pl.pallas_call(
        partial(_delta_kernel, tb=tb, c=c),
        grid=(b, ngrp, nt),
        in_specs=[xspec(0), xspec(1), xspec(2),
                  pl.BlockSpec((tb, DELTA_W), lambda bi, g, k: (rb + bi * nt + k, COL["z_b"][0] // DELTA_W + g)),
                  pl.BlockSpec((tb, LANE), lambda bi, g, k: (rb + bi * nt + k, COL["k_i"][0] // LANE)),
                  wspec(0), wspec(1), wspec(2), hspec(0), hspec(1), hspec(2),
                  pl.BlockSpec((SUBLANE, LANE), lambda bi, g, k: (0, 0)),
                  pl.BlockSpec((1, DV_B), lambda bi, g, k: (0, 0)),
                  sspec],
        out_specs=[pl.BlockSpec((tb, DELTA_W), lambda bi, g, k: (bi * nt + k, g)), sspec],
        out_shape=[jax.ShapeDtypeStruct((b * t, W_B), BF16),
                   jax.ShapeDtypeStruct((b, H_B, DK_B, DV_B), F32)],
        scratch_shapes=[pltpu.VMEM((tb + HIST_ROWS, DELTA_W), F32)] * 3
        + [pltpu.VMEM((tb, DELTA_W), F32)] * 3
        + [pltpu.VMEM((tb, LANE), F32)] * 2
        + [pltpu.VMEM((DELTA_HEADS, DK_B, DV_B), F32)],
        compiler_params=_cparams(("parallel", "parallel", "arbitrary")),
        name="gated_delta",
    )(proj, proj, proj, proj, proj, conv_w, conv_w, conv_w, hist, hist, hist, par, d_norm.reshape(1, DV_B), s_delta)


def t5_bucket(rel):
    half = NUM_BUCKETS // 2
    exact = half // 2
    n = jnp.abs(rel)
    large = exact + (jnp.log(jnp.maximum(n, 1).astype(jnp.float32) / exact)
                     / math.log(MAX_DISTANCE / exact) * (half - exact)).astype(jnp.int32)
    large = jnp.minimum(large, half - 1)
    return jnp.where(rel > 0, half, 0) + jnp.where(n < exact, n, large)


KEY_TILE = LANE
SCORE_TILE = 2 * KEY_TILE
N_BIAS_NEAR = 6
MASKED = -1e30
INT32_MIN = -2 ** 31
GROUP = H_C // H_KV


def _dot_nt(a, b):
    return lax.dot_general(a, b, (((1,), (1,)), ((), ())), preferred_element_type=F32)


def _dsa_kernel(qc_ref, qi_ref, small_ref, k_ref, v_ref, kidx_ref, bias_ref, o_ref,
                key_ref, qis_ref, wb_ref, qg_ref, acc_ref, m_ref, l_ref, *, tq, q0_tile, seq_len, topk):
    qt = q0_tile + pl.program_id(1)
    n_vis = qt + 1
    row = lax.broadcasted_iota(jnp.int32, (tq, SCORE_TILE), 0)
    lane = lax.broadcasted_iota(jnp.int32, (tq, SCORE_TILE), 1)
    q_chunk = (qt * KEY_TILE + row) // CHUNK
    pad_rows = KEY_TILE - tq

    qi = (qi_ref[...] * D_IDX ** -0.5).astype(BF16)
    w_lane = COL["w_i"][0] - COL["k_i"][0]
    w = small_ref[:, w_lane:w_lane + H_IDX] * H_IDX ** -0.5
    for i in range(H_IDX // 2):
        qis_ref[i * tq:(i + 1) * tq, :] = qi[:, i * LANE:(i + 1) * LANE]
        for j in range(2):
            wb_ref[i * tq:(i + 1) * tq, j * SCORE_TILE:(j + 1) * SCORE_TILE] = jnp.broadcast_to(
                w[:, 2 * i + j:2 * i + j + 1], (tq, SCORE_TILE))
    first_head = lax.broadcasted_iota(jnp.int32, (SCORE_TILE, LANE), 1) < D_IDX

    def score_tile(u, carry):
        start = pl.multiple_of(u * SCORE_TILE, SCORE_TILE)
        kd = kidx_ref[0, pl.ds(start, SCORE_TILE), :]
        zero = jnp.zeros_like(kd)
        kt2 = jnp.concatenate([jnp.where(first_head, kd, zero), jnp.where(first_head, zero, kd)], axis=0)
        s = _dot_nt(qis_ref[...], kt2)
        r = jnp.maximum(s, 0.0) * wb_ref[...]
        r = jnp.sum(r.reshape(H_IDX // 2, tq, 2 * SCORE_TILE), axis=0)
        score = r[:, :SCORE_TILE] + r[:, SCORE_TILE:] + 0.0
        bits = pltpu.bitcast(score, jnp.int32)
        key = jnp.where(bits < 0, bits ^ jnp.int32(0x7FFFFFFF), bits)
        kpos = start + lane
        visible = ((kpos // CHUNK) <= q_chunk) & (kpos < seq_len)
        key = jnp.where(visible, key, jnp.int32(INT32_MIN))
        if pad_rows:
            key = jnp.concatenate([key, jnp.full((pad_rows, SCORE_TILE), INT32_MIN, jnp.int32)], axis=0)
        for j in range(SCORE_TILE // KEY_TILE):
            key_ref[u * (SCORE_TILE // KEY_TILE) + j] = key[:, j * KEY_TILE:(j + 1) * KEY_TILE].T
        return carry

    lax.fori_loop(0, pl.cdiv(n_vis * KEY_TILE, SCORE_TILE), score_tile, 0)

    def count_keys(pred):
        per_step = SCORE_TILE // KEY_TILE

        def count_tiles(u, cnt):
            for j in range(per_step):
                t = u * per_step + j
                kpos = t * KEY_TILE + lax.broadcasted_iota(jnp.int32, (KEY_TILE, KEY_TILE), 0)
                cnt = cnt + jnp.where(pred(key_ref[t], kpos), 1.0, 0.0)
            return cnt

        cnt = lax.fori_loop(0, pl.cdiv(n_vis, per_step), count_tiles, jnp.zeros((KEY_TILE, KEY_TILE), F32))
        return jnp.sum(cnt, axis=0, keepdims=True)

    thr = jnp.full((1, KEY_TILE), INT32_MIN, jnp.int32)
    for bit in range(31, -1, -1):
        cand = thr + jnp.int32(-2 ** 31 if bit == 31 else 2 ** bit)
        thr = jnp.where(count_keys(lambda key, kpos, cand=cand: key >= cand) >= float(topk), cand, thr)
    thr = jnp.maximum(thr, jnp.int32(INT32_MIN + 1))

    n_above = count_keys(lambda key, kpos: key > thr)
    n_tied = count_keys(lambda key, kpos: key == thr)
    ties_wanted = float(topk) - n_above
    pos_bits = (key_ref.shape[0] * KEY_TILE - 1).bit_length()

    def find_last_tie():
        last = jnp.zeros((1, KEY_TILE), jnp.int32)
        for bit in range(pos_bits - 1, -1, -1):
            cand = last | jnp.int32(2 ** bit)
            before = count_keys(lambda key, kpos, cand=cand: (key == thr) & (kpos < cand))
            last = jnp.where(before < ties_wanted, cand, last)
        return last

    surplus = jnp.max(n_tied - ties_wanted) > 0.0
    last_tie = lax.cond(surplus, find_last_tie, lambda: jnp.full((1, KEY_TILE), 2 ** pos_bits - 1, jnp.int32))

    kvs = range(H_KV)
    for kv in kvs:
        for g in range(GROUP):
            c0 = (kv * GROUP + g) * DH_C
            qh = (qc_ref[:, c0:c0 + DH_C] * DH_C ** -0.5).astype(BF16)
            if pad_rows:
                qh = jnp.concatenate([qh, jnp.zeros((pad_rows, DH_C), BF16)], axis=0)
            qg_ref[kv, g * KEY_TILE:(g + 1) * KEY_TILE, :] = qh
    acc_ref[...] = jnp.zeros(acc_ref.shape, F32)
    m_ref[...] = jnp.full(m_ref.shape, MASKED, F32)
    l_ref[...] = jnp.zeros(l_ref.shape, F32)

    def attend_tile(t, carry):
        start = pl.multiple_of(t * KEY_TILE, KEY_TILE)
        key = key_ref[t]
        kpos = start + lax.broadcasted_iota(jnp.int32, (KEY_TILE, KEY_TILE), 0)
        sel = (key > thr) | ((key == thr) & (kpos <= last_tie))
        penalty = jnp.concatenate([jnp.where(sel, 0.0, MASKED)] * GROUP, axis=1)
        dd = jnp.minimum(qt - t, N_BIAS_NEAR)
        s = [_dot_nt(k_ref[0, pl.ds(start, KEY_TILE), kv * DH_C:(kv + 1) * DH_C], qg_ref[kv]) for kv in kvs]
        sm = [s[kv] + bias_ref[dd, kv] + penalty for kv in kvs]
        m_old = [m_ref[kv:kv + 1, :] for kv in kvs]
        m_new = [jnp.maximum(m_old[kv], jnp.max(sm[kv], axis=0, keepdims=True)) for kv in kvs]
        alpha = [jnp.exp(m_old[kv] - m_new[kv]) for kv in kvs]
        p = [jnp.exp(sm[kv] - m_new[kv]) for kv in kvs]
        pv = [lax.dot_general(v_ref[0, pl.ds(start, KEY_TILE), kv * DH_C:(kv + 1) * DH_C], p[kv].astype(BF16),
                              (((0,), (0,)), ((), ())), preferred_element_type=F32) for kv in kvs]
        for kv in kvs:
            l_ref[kv:kv + 1, :] = alpha[kv] * l_ref[kv:kv + 1, :] + jnp.sum(p[kv], axis=0, keepdims=True)
            acc_ref[kv] = alpha[kv] * acc_ref[kv] + pv[kv]
            m_ref[kv:kv + 1, :] = m_new[kv]
        return carry

    lax.fori_loop(0, n_vis, attend_tile, 0)
    for kv in kvs:
        out = (acc_ref[kv] / l_ref[kv:kv + 1, :]).T
        for g in range(GROUP):
            c0 = (kv * GROUP + g) * DH_C
            o_ref[:, c0:c0 + DH_C] = out[g * KEY_TILE:g * KEY_TILE + tq].astype(o_ref.dtype)


def dsa_bias_tiles(rel_bias):
    i = jnp.arange(KEY_TILE)
    d = jnp.arange(N_BIAS_NEAR + 1)
    rel = (i[None, :, None] - i[None, None, :]) - KEY_TILE * d[:, None, None]
    onehot = (t5_bucket(rel)[..., None] == jnp.arange(NUM_BUCKETS)).astype(F32)
    tiles = jnp.einsum('dkqn,nh->dkqh', onehot, rel_bias.astype(F32), precision=lax.Precision.HIGHEST)
    tiles = tiles.reshape(N_BIAS_NEAR + 1, KEY_TILE, KEY_TILE, H_KV, GROUP)
    return jnp.transpose(tiles, (0, 3, 1, 4, 2)).reshape(N_BIAS_NEAR + 1, H_KV, KEY_TILE, GROUP * KEY_TILE)


def dsa_attention(proj, row0, b, t, k_all, v_all, kidx_all, bias_tiles, p_len, seq_len, topk):
    tq = KEY_TILE if t % KEY_TILE == 0 else t
    assert p_len % KEY_TILE == 0 and KEY_TILE % tq == 0 and (tq == KEY_TILE or t == tq) and row0 % tq == 0
    nb = t // tq
    lp = k_all.shape[1]
    assert lp % SCORE_TILE == 0 and lp >= p_len + nb * KEY_TILE
    assert 8 * (MAX_DISTANCE / 8) ** (7 / 8) < KEY_TILE * N_BIAS_NEAR - (KEY_TILE - 1)
    rb = row0 // tq
    kern = partial(_dsa_kernel, tq=tq, q0_tile=p_len // KEY_TILE, seq_len=seq_len, topk=topk)
    return pl.pallas_call(
        kern,
        grid=(b, nb),
        in_specs=[pl.BlockSpec((tq, W_C), lambda bi, j: (rb + bi * nb + j, COL["q_c"][0] // W_C)),
                  pl.BlockSpec((tq, H_IDX * D_IDX), lambda bi, j: (rb + bi * nb + j, COL["q_i"][0] // (H_IDX * D_IDX))),
                  pl.BlockSpec((tq, LANE), lambda bi, j: (rb + bi * nb + j, COL["k_i"][0] // LANE)),
                  pl.BlockSpec((1, lp, H_KV * DH_C), lambda bi, j: (bi, 0, 0)),
                  pl.BlockSpec((1, lp, H_KV * DH_C), lambda bi, j: (bi, 0, 0)),
                  pl.BlockSpec((1, lp, 2 * D_IDX), lambda bi, j: (bi, 0, 0)),
                  pl.BlockSpec((N_BIAS_NEAR + 1, H_KV, KEY_TILE, GROUP * KEY_TILE), lambda bi, j: (0, 0, 0, 0))],
        out_specs=pl.BlockSpec((tq, W_C), lambda bi, j: (bi * nb + j, 0)),
        out_shape=jax.ShapeDtypeStruct((b * t, W_C), BF16),
        scratch_shapes=[pltpu.VMEM((lp // KEY_TILE, KEY_TILE, KEY_TILE), jnp.int32),
                        pltpu.VMEM((H_IDX // 2 * tq, 2 * D_IDX), BF16),
                        pltpu.VMEM((H_IDX // 2 * tq, 2 * SCORE_TILE), F32),
                        pltpu.VMEM((H_KV, GROUP * KEY_TILE, DH_C), BF16),
                        pltpu.VMEM((H_KV, DH_C, GROUP * KEY_TILE), F32),
                        pltpu.VMEM((SUBLANE, GROUP * KEY_TILE), F32),
                        pltpu.VMEM((SUBLANE, GROUP * KEY_TILE), F32)],
        compiler_params=_cparams(("parallel", "arbitrary")),
        name="dsa_attention",
    )(proj, proj, proj, k_all, v_all, kidx_all, bias_tiles)


def branches(proj_all, new_rows, row0, b, t, past_k, past_v, past_kidx, s_ret, s_delta, s_conv,
             ret_gn, conv_w, a_log, dt_bias, d_norm, bias_tiles):
    p_len = past_k.shape[1]
    seq_len = p_len + t
    topk = min(TOPK_MAX, seq_len // 4)
    lp = -(-seq_len // SCORE_TILE) * SCORE_TILE
    k_c, v_c, k_i, k_bf, v_bf, ki_bf = (x[row0:row0 + b * t].reshape(b, t, x.shape[-1]) for x in new_rows)

    o, w = COL["qkv_b"]
    qkv_tail = jnp.stack([proj_all[row0 + t - (CONV_W - 1) + j:row0 + b * t:t, o:o + w]
                          for j in range(CONV_W - 1)], axis=1)
    conv_new = jnp.concatenate([s_conv, qkv_tail], axis=1)[:, -(CONV_W - 1):]

    o_a, ret_new = retention(proj_all, row0, b, t, p_len, s_ret, ret_gn)
    o_b, delta_new = gated_delta(proj_all, row0, b, t, s_conv, s_delta, conv_w, a_log, dt_bias, d_norm)

    def all_keys(past, new, reps=1):
        if p_len == 0 and lp == seq_len:
            return new
        w = past.shape[-1] * reps
        past = jnp.tile(past.reshape(b, p_len, -1).astype(BF16), (1, 1, reps))
        return jnp.concatenate([past, new, jnp.zeros((b, lp - seq_len, w), BF16)], axis=1)

    o_c = dsa_attention(proj_all, row0, b, t, all_keys(past_k.reshape(b, p_len, KV_W), k_bf),
                        all_keys(past_v.reshape(b, p_len, KV_W), v_bf), all_keys(past_kidx, ki_bf, 2),
                        bias_tiles, p_len, seq_len, topk)
    caches = (k_c.reshape(b, t, H_KV, DH_C), v_c.reshape(b, t, H_KV, DH_C), k_i, ret_new, delta_new, conv_new)
    return jnp.stack([o_a, o_b, o_c]), caches


def _prep_w_in(w):
    parts = [w[:, MODEL_COL[n][0]:MODEL_COL[n][0] + MODEL_COL[n][1]] for n in _KERNEL_ORDER]
    parts.append(jnp.zeros((w.shape[0], D_IN_PAD - D_IN), w.dtype))
    return jnp.concatenate(parts, axis=1).astype(BF16)


def kernel(x_prompt, x_sample, cache_k, cache_v, cache_kidx, state_ret, state_delta, state_conv, norm_mix, w_in, ret_gn, conv_w, delta_a_log, delta_dt_bias, delta_norm, rel_bias, w_branch_a, w_branch_b, w_branch_c, w_out, norm_ffn, w_ffn_gate, w_ffn_up, w_ffn_down, norm_final):
    dt = x_prompt.dtype
    bp, tp, _ = x_prompt.shape
    bs, ts, _ = x_sample.shape
    depth = w_in.shape[0]
    n_p = bp * tp
    zk = jnp.zeros((bp, 0, H_KV, DH_C), dt)
    zkidx = jnp.zeros((bp, 0, D_IDX), dt)
    zret = jnp.zeros((bp, H_A, DK_A, DV_A), dt)
    zdelta = jnp.zeros((bp, H_B, DK_B, DV_B), dt)
    zconv = jnp.zeros((bp, CONV_W - 1, CONV_CH), dt)

    h = jnp.concatenate([x_prompt.reshape(n_p, D_MODEL), x_sample.reshape(bs * ts, D_MODEL)], axis=0)
    new_p, new_s = [], []
    bias_tiles = dsa_bias_tiles(rel_bias)
    for l in range(depth):
        u = rmsnorm(h, norm_mix[l], BF16)
        proj, *new_rows = matmul_in(u, _prep_w_in(w_in[l]))
        bw = (ret_gn[l], conv_w[l], delta_a_log[l], delta_dt_bias[l], delta_norm[l], bias_tiles)
        o_p, sp = branches(proj, new_rows, 0, bp, tp, zk, zk, zkidx, zret, zdelta, zconv, *bw)
        o_s, ss = branches(proj, new_rows, n_p, bs, ts, cache_k[l], cache_v[l], cache_kidx[l],
                           state_ret[l], state_delta[l], state_conv[l], *bw)
        new_p.append(sp)
        new_s.append(ss)
        o_abc = jnp.concatenate([o_p, o_s], axis=1)
        w_br = jnp.stack([w_branch_a[l], w_branch_b[l], w_branch_c[l]]).astype(BF16)
        merged = merge_branches(o_abc, w_br, proj)
        h = matmul_residual(merged, w_out[l].astype(BF16), h)
        u2 = rmsnorm(h, norm_ffn[l], BF16)
        act = ffn_gate_up(u2, w_ffn_gate[l].astype(BF16), w_ffn_up[l].astype(BF16))
        h = ffn_down(act, w_ffn_down[l].astype(BF16), h)

    y = rmsnorm(h, norm_final, F32)
    y_prompt = y[:n_p].reshape(bp, tp, D_MODEL)
    y_sample = y[n_p:].reshape(bs, ts, D_MODEL)
    k_p, v_p, kidx_p, ret_p, delta_p, conv_p = [jnp.stack([s[i] for s in new_p]) for i in range(6)]
    k_s, v_s, kidx_s, ret_s, delta_s, conv_s = [jnp.stack([s[i] for s in new_s]) for i in range(6)]
    return (y_prompt, y_sample, k_p, v_p, kidx_p, ret_p, delta_p, conv_p,
            k_s, v_s, kidx_s, ret_s, delta_s, conv_s)
```

```python
import math
from functools import partial

import jax
import jax.numpy as jnp
from jax import lax
from jax.experimental import pallas as pl
from jax.experimental.pallas import tpu as pltpu

F32 = jnp.float32
BF16 = jnp.bfloat16

D_MODEL = 4096
CHUNK = 64
EPS = 1e-6
H_A, DK_A, DV_A = 8, 128, 256
W_A = H_A * DV_A
ROPE_BASE = 10000.0
H_B, DK_B, DV_B = 16, 128, 128
W_B = H_B * DV_B
CONV_W = 4
CONV_CH = 2 * H_B * DK_B + H_B * DV_B
H_C, H_KV, DH_C = 16, 4, 128
W_C = H_C * DH_C
KV_W = H_KV * DH_C
H_IDX, D_IDX = 32, 64
TOPK_MAX = 256
NUM_BUCKETS = 32
MAX_DISTANCE = 1024

LANE = 128
SUBLANE = 8
VMEM_LIMIT_V7X = 56 * 1024 * 1024
TM = 1024
TN = 1024
TIME_BLOCK = 512

_MODEL_ORDER = [("q_a", H_A * DK_A), ("k_a", H_A * DK_A), ("v_a", W_A), ("g_a", W_A),
                ("qkv_b", CONV_CH), ("z_b", W_B), ("a_b", H_B), ("b_b", H_B),
                ("q_c", W_C), ("k_c", KV_W), ("v_c", KV_W), ("q_i", H_IDX * D_IDX), ("k_i", D_IDX), ("w_i", H_IDX),
                ("gate_a", D_MODEL), ("gate_b", D_MODEL), ("gate_c", D_MODEL)]
_KERNEL_ORDER = ["q_a", "k_a", "v_a", "g_a", "qkv_b", "z_b", "q_c", "q_i", "k_c", "v_c",
                 "gate_a", "gate_b", "gate_c", "k_i", "w_i", "a_b", "b_b"]


def _offsets(order, widths):
    table, off = {}, 0
    for name in order:
        table[name] = (off, widths[name])
        off += widths[name]
    return table, off


_WIDTH = dict(_MODEL_ORDER)
MODEL_COL, D_IN = _offsets([n for n, _ in _MODEL_ORDER], _WIDTH)
COL, _ = _offsets(_KERNEL_ORDER, _WIDTH)
D_IN_PAD = -(-D_IN // TN) * TN


def _cparams(sem):
    return pltpu.CompilerParams(dimension_semantics=sem, vmem_limit_bytes=VMEM_LIMIT_V7X)


def _rmsnorm_kernel(x_ref, g_ref, o_ref):
    x = x_ref[...]
    y = x * lax.rsqrt(jnp.mean(x * x, axis=-1, keepdims=True) + EPS)
    o_ref[...] = (y * g_ref[...]).astype(o_ref.dtype)


def rmsnorm(x, g, out_dtype, tr=512):
    m, d = x.shape
    return pl.pallas_call(
        _rmsnorm_kernel,
        grid=(m // tr,),
        in_specs=[pl.BlockSpec((tr, d), lambda i: (i, 0)), pl.BlockSpec((1, d), lambda i: (0, 0))],
        out_specs=pl.BlockSpec((tr, d), lambda i: (i, 0)),
        out_shape=jax.ShapeDtypeStruct((m, d), out_dtype),
        compiler_params=_cparams(("parallel",)),
        name="rmsnorm",
    )(x, g.reshape(1, d))


def _mm_in_kernel(a_ref, b_ref, o_ref, k_ref, v_ref, ki_ref, kb_ref, vb_ref, kib_ref, *, kv_block, small_block):
    j = pl.program_id(1)
    o_ref[...] = jnp.dot(a_ref[...], b_ref[...], preferred_element_type=F32)

    @pl.when(j == kv_block)
    def _():
        k, v = o_ref[:, :KV_W], o_ref[:, KV_W:2 * KV_W]
        k_ref[...] = k
        v_ref[...] = v
        kb_ref[...] = k.astype(BF16)
        vb_ref[...] = v.astype(BF16)

    @pl.when(j == small_block)
    def _():
        ki = o_ref[:, :D_IDX]
        ki_ref[...] = ki
        kib_ref[...] = jnp.concatenate([ki, ki], axis=1).astype(BF16)


def matmul_in(a, b, tn=TN):
    m, k = a.shape
    n = b.shape[1]
    assert COL["k_c"][0] % tn == 0 and COL["v_c"][0] == COL["k_c"][0] + KV_W and COL["k_i"][0] % tn == 0
    kern = partial(_mm_in_kernel, kv_block=COL["k_c"][0] // tn, small_block=COL["k_i"][0] // tn)

    def row_spec(w):
        return pl.BlockSpec((TM, w), lambda i, j: (i, 0))

    return pl.pallas_call(
        kern,
        grid=(m // TM, n // tn),
        in_specs=[pl.BlockSpec((TM, k), lambda i, j: (i, 0), pipeline_mode=pl.Buffered(1)),
                  pl.BlockSpec((k, tn), lambda i, j: (0, j))],
        out_specs=[pl.BlockSpec((TM, tn), lambda i, j: (i, j)), row_spec(KV_W), row_spec(KV_W), row_spec(D_IDX),
                   row_spec(KV_W), row_spec(KV_W), row_spec(2 * D_IDX)],
        out_shape=[jax.ShapeDtypeStruct((m, n), F32), jax.ShapeDtypeStruct((m, KV_W), F32),
                   jax.ShapeDtypeStruct((m, KV_W), F32), jax.ShapeDtypeStruct((m, D_IDX), F32),
                   jax.ShapeDtypeStruct((m, KV_W), BF16), jax.ShapeDtypeStruct((m, KV_W), BF16),
                   jax.ShapeDtypeStruct((m, 2 * D_IDX), BF16)],
        compiler_params=_cparams(("parallel", "arbitrary")),
        name="matmul_in",
    )(a, b)


N_BRANCH = 3


def _merge_kernel(*refs):
    o_refs, w_refs, g_refs = (refs[i * N_BRANCH:(i + 1) * N_BRANCH] for i in range(3))
    out_ref = refs[3 * N_BRANCH]
    acc = None
    for o_ref, w_ref, g_ref in zip(o_refs, w_refs, g_refs):
        y = jnp.dot(o_ref[0], w_ref[0], preferred_element_type=F32) * jax.nn.sigmoid(g_ref[...])
        acc = y if acc is None else acc + y
    out_ref[...] = acc.astype(out_ref.dtype)


def merge_branches(o_abc, w_br, proj, tn=TN // 2):
    _, m, k = o_abc.shape
    n = w_br.shape[2]
    gate0 = COL["gate_a"][0] // tn
    nj = n // tn

    def per_branch(make):
        return [make(r) for r in range(N_BRANCH)]

    return pl.pallas_call(
        _merge_kernel,
        grid=(m // TM, nj),
        in_specs=per_branch(lambda r: pl.BlockSpec((1, TM, k), lambda i, j: (r, i, 0), pipeline_mode=pl.Buffered(1)))
        + per_branch(lambda r: pl.BlockSpec((1, k, tn), lambda i, j: (r, 0, j)))
        + per_branch(lambda r: pl.BlockSpec((TM, tn), lambda i, j: (i, gate0 + r * nj + j))),
        out_specs=pl.BlockSpec((TM, tn), lambda i, j: (i, j)),
        out_shape=jax.ShapeDtypeStruct((m, n), BF16),
        compiler_params=_cparams(("parallel", "parallel")),
        name="merge_branches",
    )(*([o_abc] * N_BRANCH), *([w_br] * N_BRANCH), *([proj] * N_BRANCH))


def _mm_res_kernel(a_ref, b_ref, h_ref, o_ref):
    o_ref[...] = h_ref[...] + jnp.dot(a_ref[...], b_ref[...], preferred_element_type=F32)


def matmul_residual(a, b, h, tn=TN):
    m, k = a.shape
    n = b.shape[1]
    return pl.pallas_call(
        _mm_res_kernel,
        grid=(m // TM, n // tn),
        in_specs=[pl.BlockSpec((TM, k), lambda i, j: (i, 0)), pl.BlockSpec((k, tn), lambda i, j: (0, j)),
                  pl.BlockSpec((TM, tn), lambda i, j: (i, j))],
        out_specs=pl.BlockSpec((TM, tn), lambda i, j: (i, j)),
        out_shape=jax.ShapeDtypeStruct((m, n), F32),
        compiler_params=_cparams(("parallel", "parallel")),
        name="matmul_out",
    )(a, b, h)


def _gateup_kernel(u_ref, wg_ref, wu_ref, o_ref):
    u = u_ref[...]
    g = jnp.dot(u, wg_ref[...], preferred_element_type=F32)
    up = jnp.dot(u, wu_ref[...], preferred_element_type=F32)
    o_ref[...] = (jax.nn.silu(g) * up).astype(o_ref.dtype)


def ffn_gate_up(u, wg, wu, tn=512):
    m, k = u.shape
    n = wg.shape[1]
    return pl.pallas_call(
        _gateup_kernel,
        grid=(m // TM, pl.cdiv(n, tn)),
        in_specs=[pl.BlockSpec((TM, k), lambda i, j: (i, 0)),
                  pl.BlockSpec((k, tn), lambda i, j: (0, j)),
                  pl.BlockSpec((k, tn), lambda i, j: (0, j))],
        out_specs=pl.BlockSpec((TM, tn), lambda i, j: (i, j)),
        out_shape=jax.ShapeDtypeStruct((m, n), BF16),
        compiler_params=_cparams(("parallel", "parallel")),
        name="ffn_gate_up",
    )(u, wg, wu)


def ffn_down(a, b, h, tm=768, tn=512):
    m, k = a.shape
    n = b.shape[1]
    return pl.pallas_call(
        _mm_res_kernel,
        grid=(m // tm, n // tn),
        in_specs=[pl.BlockSpec((tm, k), lambda i, j: (i, 0), pipeline_mode=pl.Buffered(1)),
                  pl.BlockSpec((k, tn), lambda i, j: (0, j)),
                  pl.BlockSpec((tm, tn), lambda i, j: (i, j))],
        out_specs=pl.BlockSpec((tm, tn), lambda i, j: (i, j)),
        out_shape=jax.ShapeDtypeStruct((m, n), F32),
        compiler_params=_cparams(("parallel", "parallel")),
        name="ffn_down",
    )(a, b, h)


RET_HEADS = 4


def _retention_kernel(q_ref, k_ref, v_ref, g_ref, cos_ref, sin_ref, idec_ref, qdec_ref, kdec_ref, cdec_ref, gn_ref,
                      s0_ref, o_ref, sfin_ref, s_ref, *, tb, c):
    tblk = pl.program_id(2)
    heads = range(RET_HEADS)

    @pl.when(tblk == 0)
    def _():
        s_ref[...] = s0_ref[0]

    def rotate(x, cos, sin):
        return x * cos + pltpu.roll(x, DK_A // 2, 1) * sin

    def chunk_step(ci, carry):
        c0 = pl.multiple_of(ci * c, c)
        cos = cos_ref[pl.ds(c0, c), :]
        sin = sin_ref[pl.ds(c0, c), :]
        qr = [rotate(q_ref[pl.ds(c0, c), h * DK_A:(h + 1) * DK_A], cos, sin) for h in heads]
        kr = [rotate(k_ref[pl.ds(c0, c), h * DK_A:(h + 1) * DK_A], cos, sin) * DK_A ** -0.5 for h in heads]
        v = [v_ref[pl.ds(c0, c), h * DV_A:(h + 1) * DV_A].astype(BF16) for h in heads]
        s = [s_ref[h] for h in heads]
        att = [_dot_nt(qr[h].astype(BF16), kr[h].astype(BF16)) * idec_ref[h] for h in heads]
        o = [_bdot(att[h], v[h]) + _bdot(qr[h] * qdec_ref[h], s[h]) for h in heads]
        for h in heads:
            s_ref[h] = s[h] * cdec_ref[h, 0:1, :] + lax.dot_general(
                (kr[h] * kdec_ref[h]).astype(BF16), v[h], (((0,), (0,)), ((), ())), preferred_element_type=F32)
        for h in heads:
            mu = jnp.mean(o[h], axis=-1, keepdims=True)
            d = o[h] - mu
            var = jnp.mean(d * d, axis=-1, keepdims=True)
            cols = slice(h * DV_A, (h + 1) * DV_A)
            g = g_ref[pl.ds(c0, c), cols]
            o_ref[pl.ds(c0, c), cols] = (
                g * jax.nn.sigmoid(g) * (d * lax.rsqrt(var + EPS) * gn_ref[:, cols])).astype(o_ref.dtype)
        return carry

    lax.fori_loop(0, tb // c, chunk_step, 0)

    @pl.when(tblk == pl.num_programs(2) - 1)
    def _():
        sfin_ref[0] = s_ref[...]


def retention(proj, row0, b, t, p_len, s_ret, ret_gn):
    c = 2 * CHUNK if t % (2 * CHUNK) == 0 else t
    tb = TIME_BLOCK if t % TIME_BLOCK == 0 else t
    assert tb % c == 0 and row0 % tb == 0
    nt = t // tb
    rb = row0 // tb
    half = DK_A // 2
    inv = ROPE_BASE ** (-jnp.arange(half, dtype=F32) / half)
    ang = (p_len + jnp.arange(t)).astype(F32)[:, None] * inv[None, :]
    cos2 = jnp.concatenate([jnp.cos(ang), jnp.cos(ang)], axis=1)
    sin2 = jnp.concatenate([-jnp.sin(ang), jnp.sin(ang)], axis=1)
    log_gamma = jnp.log(1.0 - 2.0 ** (-5.0 - jnp.arange(H_A, dtype=F32)))
    i = jnp.arange(c, dtype=F32)
    rel = i[:, None] - i[None, :]
    idec = jnp.exp(jnp.where(rel[None] >= 0, rel[None] * log_gamma[:, None, None], -jnp.inf))
    qdec = jnp.broadcast_to(jnp.exp((i[None, :] + 1.0) * log_gamma[:, None])[:, :, None], (H_A, c, DK_A))
    kdec = jnp.broadcast_to(jnp.exp((c - 1.0 - i[None, :]) * log_gamma[:, None])[:, :, None], (H_A, c, DK_A))
    cdec = jnp.broadcast_to(jnp.exp(c * log_gamma)[:, None, None], (H_A, SUBLANE, DV_A))
    wk, wv = RET_HEADS * DK_A, RET_HEADS * DV_A
    qb0 = COL["q_a"][0] // wk
    kb0 = COL["k_a"][0] // wk
    vb0 = COL["v_a"][0] // wv
    gb0 = COL["g_a"][0] // wv
    sspec = pl.BlockSpec((1, RET_HEADS, DK_A, DV_A), lambda bi, h, k: (bi, h, 0, 0))
    return pl.pallas_call(
        partial(_retention_kernel, tb=tb, c=c),
        grid=(b, H_A // RET_HEADS, nt),
        in_specs=[pl.BlockSpec((tb, wk), lambda bi, h, k: (rb + bi * nt + k, qb0 + h)),
                  pl.BlockSpec((tb, wk), lambda bi, h, k: (rb + bi * nt + k, kb0 + h)),
                  pl.BlockSpec((tb, wv), lambda bi, h, k: (rb + bi * nt + k, vb0 + h)),
                  pl.BlockSpec((tb, wv), lambda bi, h, k: (rb + bi * nt + k, gb0 + h)),
                  pl.BlockSpec((tb, DK_A), lambda bi, h, k: (k, 0)),
                  pl.BlockSpec((tb, DK_A), lambda bi, h, k: (k, 0)),
                  pl.BlockSpec((RET_HEADS, c, c), lambda bi, h, k: (h, 0, 0)),
                  pl.BlockSpec((RET_HEADS, c, DK_A), lambda bi, h, k: (h, 0, 0)),
                  pl.BlockSpec((RET_HEADS, c, DK_A), lambda bi, h, k: (h, 0, 0)),
                  pl.BlockSpec((RET_HEADS, SUBLANE, DV_A), lambda bi, h, k: (h, 0, 0)),
                  pl.BlockSpec((1, wv), lambda bi, h, k: (0, h)),
                  sspec],
        out_specs=[pl.BlockSpec((tb, wv), lambda bi, h, k: (bi * nt + k, h)), sspec],
        out_shape=[jax.ShapeDtypeStruct((b * t, W_A), BF16),
                   jax.ShapeDtypeStruct((b, H_A, DK_A, DV_A), F32)],
        scratch_shapes=[pltpu.VMEM((RET_HEADS, DK_A, DV_A), F32)],
        compiler_params=_cparams(("parallel", "parallel", "arbitrary")),
        name="retention",
    )(proj, proj, proj, proj, cos2, sin2, idec, qdec, kdec, cdec, ret_gn.reshape(1, W_A), s_ret)


DELTA_HEADS = 8
DELTA_W = DELTA_HEADS * DK_B
HIST_ROWS = SUBLANE
INV_BASE = 16


def _bdot(a, b):
    return jnp.dot(a.astype(BF16), b.astype(BF16), preferred_element_type=F32)


def _hdot(a, b):
    return jnp.dot(a, b, preferred_element_type=F32, precision=lax.Precision.HIGHEST)


def _split_bf16(x):
    hi = x.astype(BF16)
    return hi, (x - hi.astype(F32)).astype(BF16)


def _dot3(a, b):
    a_hi, a_lo = _split_bf16(a)
    b_hi, b_lo = _split_bf16(b)
    dot = partial(jnp.dot, preferred_element_type=F32)
    return dot(a_hi, b_hi) + (dot(a_hi, b_lo) + dot(a_lo, b_hi))


def _unit_lower_inverse(mats, c):
    ii = lax.broadcasted_iota(jnp.int32, (c, c), 0)
    jj = lax.broadcasted_iota(jnp.int32, (c, c), 1)
    eye = jnp.where(ii == jj, 1.0, 0.0).astype(F32)
    ps = [jnp.where((ii // INV_BASE) == (jj // INV_BASE), a, 0.0) for a in mats]
    ts = [eye - p for p in ps]
    for _ in range(int(math.log2(INV_BASE)) - 1):
        ps = [_dot3(p, p) for p in ps]
        ts = [_dot3(t, eye + p) for t, p in zip(ts, ps)]
    size = INV_BASE
    while size < c:
        half_blocks = ((ii // (2 * size)) == (jj // (2 * size))) & ((ii // size) != (jj // size))
        tl = [_dot3(t, jnp.where(half_blocks, a, 0.0)) for t, a in zip(ts, mats)]
        ts = [t - _dot3(x, t) for t, x in zip(ts, tl)]
        size *= 2
    return ts


def _delta_kernel(xq_ref, xk_ref, xv_ref, z_ref, small_ref, cwq_ref, cwk_ref, cwv_ref, hq_ref, hk_ref, hv_ref,
                  par_ref, dn_ref, s0_ref, o_ref, sfin_ref,
                  bq_ref, bk_ref, bv_ref, qn_ref, kn_ref, vv_ref, g_ref, beta_ref, s_ref, w_ref, qk_ref, el_ref,
                  *, tb, c):
    hg = pl.program_id(1)
    tblk = pl.program_id(2)

    @pl.when(tblk == 0)
    def _():
        bq_ref[0:HIST_ROWS, :] = hq_ref[0]
        bk_ref[0:HIST_ROWS, :] = hk_ref[0]
        bv_ref[0:HIST_ROWS, :] = hv_ref[0]
        s_ref[...] = s0_ref[0]

    def conv_silu(x_ref, buf_ref, cw_ref):
        buf_ref[HIST_ROWS:HIST_ROWS + tb, :] = x_ref[...]
        y = jnp.zeros((tb, DELTA_W), F32)
        for j in range(CONV_W):
            lo = HIST_ROWS - (CONV_W - 1) + j
            y = y + buf_ref[lo:lo + tb, :] * cw_ref[j:j + 1, :]
        buf_ref[0:HIST_ROWS, :] = buf_ref[tb:tb + HIST_ROWS, :]
        return y * jax.nn.sigmoid(y)

    def l2n(x):
        parts = []
        for h in range(DELTA_HEADS):
            xh = x[:, h * DK_B:(h + 1) * DK_B]
            parts.append(xh * lax.rsqrt(jnp.sum(xh * xh, axis=-1, keepdims=True) + EPS))
        return jnp.concatenate(parts, axis=1)

    qn_ref[...] = l2n(conv_silu(xq_ref, bq_ref, cwq_ref)) * DK_B ** -0.5
    kn_ref[...] = l2n(conv_silu(xk_ref, bk_ref, cwk_ref))
    vv_ref[...] = conv_silu(xv_ref, bv_ref, cwv_ref)
    sm = small_ref[...]
    x = sm + par_ref[1:2, :]
    softplus = jnp.maximum(x, 0.0) + jnp.log1p(jnp.exp(-jnp.abs(x)))
    g_ref[...] = -jnp.exp(par_ref[0:1, :]) * softplus
    beta_ref[...] = jax.nn.sigmoid(sm)

    ii = lax.broadcasted_iota(jnp.int32, (c, c), 0)
    jj = lax.broadcasted_iota(jnp.int32, (c, c), 1)
    incl = ii >= jj
    strict = ii > jj
    diag = ii == jj
    tril_ones = jnp.where(incl, 1.0, 0.0).astype(F32)
    a_lane = COL["a_b"][0] - COL["k_i"][0]
    b_lane = COL["b_b"][0] - COL["k_i"][0]

    n_chunks = tb // c
    p_unroll = 2 if n_chunks % 2 == 0 else 1

    def prepare_chunks(step, carry):
        lane = lax.broadcasted_iota(jnp.int32, (c, LANE), 1)
        items = []
        for j in range(p_unroll):
            ci = step * p_unroll + j
            c0 = pl.multiple_of(ci * c, c)
            gcum_all = _hdot(tril_ones, g_ref[pl.ds(c0, c), :])
            beta_all = beta_ref[pl.ds(c0, c), :]
            for h in range(DELTA_HEADS):
                items.append((ci, c0, h, gcum_all, beta_all))
        cols = [slice(h * DK_B, (h + 1) * DK_B) for _, _, h, _, _ in items]
        gc = [jnp.sum(jnp.where(lane == a_lane + hg * DELTA_HEADS + h, ga, 0.0), axis=-1, keepdims=True)
              for _, _, h, ga, _ in items]
        bc = [jnp.sum(jnp.where(lane == b_lane + hg * DELTA_HEADS + h, ba, 0.0), axis=-1, keepdims=True)
              for _, _, h, _, ba in items]
        gr = [jnp.sum(jnp.where(diag, jnp.broadcast_to(g, (c, c)), 0.0), axis=0, keepdims=True) for g in gc]
        g_last = [jnp.sum(jnp.where(ii[:, 0:1] == c - 1, g, 0.0), axis=0, keepdims=True) for g in gc]
        gam = [jnp.exp(jnp.where(incl, g - r, -jnp.inf)) for g, r in zip(gc, gr)]
        eg = [jnp.exp(g) for g in gc]
        qh = [qn_ref[pl.ds(it[1], c), cs] for it, cs in zip(items, cols)]
        kh = [kn_ref[pl.ds(it[1], c), cs] for it, cs in zip(items, cols)]
        vh = [vv_ref[pl.ds(it[1], c), cs] for it, cs in zip(items, cols)]
        kb = [k * b for k, b in zip(kh, bc)]
        a = [jnp.where(strict, _dot_nt(x.astype(BF16), k.astype(BF16)) * gm, 0.0) for x, k, gm in zip(kb, kh, gam)]
        qk = [_dot_nt(q.astype(BF16), k.astype(BF16)) * gm for q, k, gm in zip(qh, kh, gam)]
        tinv = _unit_lower_inverse(a, c)
        u = [_bdot(t, v * b) for t, v, b in zip(tinv, vh, bc)]
        w = [_bdot(t, x * e) for t, x, e in zip(tinv, kb, eg)]
        for n, (ci, c0, h, _, _) in enumerate(items):
            rows = pl.ds(c0, c)
            vv_ref[rows, cols[n]] = u[n]
            w_ref[rows, cols[n]] = w[n]
            qn_ref[rows, cols[n]] = qh[n] * eg[n]
            kn_ref[rows, cols[n]] = kh[n] * jnp.exp(g_last[n] - gc[n])
            qk_ref[ci * DELTA_HEADS + h] = qk[n]
            el_ref[ci * DELTA_HEADS + h] = jnp.broadcast_to(jnp.exp(g_last[n]), (SUBLANE, LANE))
        return carry

    lax.fori_loop(0, n_chunks // p_unroll, prepare_chunks, 0)

    def state_step(ci, carry):
        rows = pl.ds(pl.multiple_of(ci * c, c), c)
        heads = range(DELTA_HEADS)
        cols = [slice(h * DK_B, (h + 1) * DK_B) for h in heads]
        s = [s_ref[h] for h in heads]
        v_new = [vv_ref[rows, cols[h]] - _bdot(w_ref[rows, cols[h]], s[h]) for h in heads]
        o = [_bdot(qn_ref[rows, cols[h]], s[h]) + _bdot(qk_ref[ci * DELTA_HEADS + h], v_new[h]) for h in heads]
        for h in heads:
            s_ref[h] = s[h] * el_ref[ci * DELTA_HEADS + h][0:1, :] + lax.dot_general(
                kn_ref[rows, cols[h]].astype(BF16), v_new[h].astype(BF16), (((0,), (0,)), ((), ())),
                preferred_element_type=F32)
        for h in heads:
            on = o[h] * lax.rsqrt(jnp.mean(o[h] * o[h], axis=-1, keepdims=True) + EPS) * dn_ref[...]
            zh = z_ref[rows, cols[h]]
            o_ref[rows, cols[h]] = (on * (zh * jax.nn.sigmoid(zh))).astype(o_ref.dtype)
        return carry

    lax.fori_loop(0, n_chunks, state_step, 0)

    @pl.when(tblk == pl.num_programs(2) - 1)
    def _():
        sfin_ref[0] = s_ref[...]


def gated_delta(proj, row0, b, t, s_conv, s_delta, conv_w, a_log, dt_bias, d_norm):
    c = CHUNK if t % CHUNK == 0 else t
    tb = TIME_BLOCK if t % TIME_BLOCK == 0 else t
    assert tb % c == 0 and c % INV_BASE == 0 and tb >= HIST_ROWS and row0 % tb == 0
    nt = t // tb
    rb = row0 // tb
    ngrp = H_B // DELTA_HEADS
    qcol = COL["qkv_b"][0] // DELTA_W
    hist = jnp.concatenate([jnp.zeros((b, HIST_ROWS - (CONV_W - 1), CONV_CH), F32), s_conv], axis=1)
    par = jnp.zeros((SUBLANE, LANE), F32)
    a_lane = COL["a_b"][0] - COL["k_i"][0]
    par = par.at[0, a_lane:a_lane + H_B].set(a_log).at[1, a_lane:a_lane + H_B].set(dt_bias)

    def xspec(part):
        return pl.BlockSpec((tb, DELTA_W), lambda bi, g, k: (rb + bi * nt + k, qcol + part * ngrp + g))

    def wspec(part):
        return pl.BlockSpec((CONV_W, DELTA_W), lambda bi, g, k: (0, part * ngrp + g))

    def hspec(part):
        return pl.BlockSpec((1, HIST_ROWS, DELTA_W), lambda bi, g, k: (bi, 0, part * ngrp + g))

    sspec = pl.BlockSpec((1, DELTA_HEADS, DK_B, DV_B), lambda bi, g, k: (bi, g, 0, 0))
    return pl.pallas_call(
        partial(_delta_kernel, tb=tb, c=c),
        grid=(b, ngrp, nt),
        in_specs=[xspec(0), xspec(1), xspec(2),
                  pl.BlockSpec((tb, DELTA_W), lambda bi, g, k: (rb + bi * nt + k, COL["z_b"][0] // DELTA_W + g)),
                  pl.BlockSpec((tb, LANE), lambda bi, g, k: (rb + bi * nt + k, COL["k_i"][0] // LANE)),
                  wspec(0), wspec(1), wspec(2), hspec(0), hspec(1), hspec(2),
                  pl.BlockSpec((SUBLANE, LANE), lambda bi, g, k: (0, 0)),
                  pl.BlockSpec((1, DV_B), lambda bi, g, k: (0, 0)),
                  sspec],
        out_specs=[pl.BlockSpec((tb, DELTA_W), lambda bi, g, k: (bi * nt + k, g)), sspec],
        out_shape=[jax.ShapeDtypeStruct((b * t, W_B), BF16),
                   jax.ShapeDtypeStruct((b, H_B, DK_B, DV_B), F32)],
        scratch_shapes=[pltpu.VMEM((tb + HIST_ROWS, DELTA_W), F32)] * 3
        + [pltpu.VMEM((tb, DELTA_W), F32)] * 3
        + [pltpu.VMEM((tb, LANE), F32)] * 2
        + [pltpu.VMEM((DELTA_HEADS, DK_B, DV_B), F32),
           pltpu.VMEM((tb, DELTA_W), F32),
           pltpu.VMEM((tb // c * DELTA_HEADS, c, c), F32),
           pltpu.VMEM((tb // c * DELTA_HEADS, SUBLANE, LANE), F32)],
        compiler_params=_cparams(("parallel", "parallel", "arbitrary")),
        name="gated_delta",
    )(proj, proj, proj, proj, proj, conv_w, conv_w, conv_w, hist, hist, hist, par, d_norm.reshape(1, DV_B), s_delta)


def t5_bucket(rel):
    half = NUM_BUCKETS // 2
    exact = half // 2
    n = jnp.abs(rel)
    large = exact + (jnp.log(jnp.maximum(n, 1).astype(jnp.float32) / exact)
                     / math.log(MAX_DISTANCE / exact) * (half - exact)).astype(jnp.int32)
    large = jnp.minimum(large, half - 1)
    return jnp.where(rel > 0, half, 0) + jnp.where(n < exact, n, large)


KEY_TILE = LANE
SCORE_TILE = 2 * KEY_TILE
N_BIAS_NEAR = 6
MASKED = -1e30
INT32_MIN = -2 ** 31
GROUP = H_C // H_KV


def _dot_nt(a, b):
    return lax.dot_general(a, b, (((1,), (1,)), ((), ())), preferred_element_type=F32)


def _dsa_kernel(qc_ref, qi_ref, small_ref, k_ref, v_ref, kidx_ref, bias_ref, o_ref,
                key_ref, qis_ref, wb_ref, qg_ref, acc_ref, m_ref, l_ref, *, tq, q0_tile, seq_len, topk):
    qt = q0_tile + pl.program_id(1)
    n_vis = qt + 1
    row = lax.broadcasted_iota(jnp.int32, (tq, SCORE_TILE), 0)
    lane = lax.broadcasted_iota(jnp.int32, (tq, SCORE_TILE), 1)
    q_chunk = (qt * KEY_TILE + row) // CHUNK
    pad_rows = KEY_TILE - tq

    qi = (qi_ref[...] * D_IDX ** -0.5).astype(BF16)
    w_lane = COL["w_i"][0] - COL["k_i"][0]
    w = small_ref[:, w_lane:w_lane + H_IDX] * H_IDX ** -0.5
    for i in range(H_IDX // 2):
        qis_ref[i * tq:(i + 1) * tq, :] = qi[:, i * LANE:(i + 1) * LANE]
        for j in range(2):
            wb_ref[i * tq:(i + 1) * tq, j * SCORE_TILE:(j + 1) * SCORE_TILE] = jnp.broadcast_to(
                w[:, 2 * i + j:2 * i + j + 1], (tq, SCORE_TILE))
    first_head = lax.broadcasted_iota(jnp.int32, (SCORE_TILE, LANE), 1) < D_IDX

    def score_tile(u, carry):
        start = pl.multiple_of(u * SCORE_TILE, SCORE_TILE)
        kd = kidx_ref[0, pl.ds(start, SCORE_TILE), :]
        zero = jnp.zeros_like(kd)
        kt2 = jnp.concatenate([jnp.where(first_head, kd, zero), jnp.where(first_head, zero, kd)], axis=0)
        s = _dot_nt(qis_ref[...], kt2)
        r = jnp.maximum(s, 0.0) * wb_ref[...]
        r = jnp.sum(r.reshape(H_IDX // 2, tq, 2 * SCORE_TILE), axis=0)
        score = r[:, :SCORE_TILE] + r[:, SCORE_TILE:] + 0.0
        bits = pltpu.bitcast(score, jnp.int32)
        key = jnp.where(bits < 0, bits ^ jnp.int32(0x7FFFFFFF), bits)
        kpos = start + lane
        visible = ((kpos // CHUNK) <= q_chunk) & (kpos < seq_len)
        key = jnp.where(visible, key, jnp.int32(INT32_MIN))
        if pad_rows:
            key = jnp.concatenate([key, jnp.full((pad_rows, SCORE_TILE), INT32_MIN, jnp.int32)], axis=0)
        for j in range(SCORE_TILE // KEY_TILE):
            key_ref[u * (SCORE_TILE // KEY_TILE) + j] = key[:, j * KEY_TILE:(j + 1) * KEY_TILE].T
        return carry

    lax.fori_loop(0, pl.cdiv(n_vis * KEY_TILE, SCORE_TILE), score_tile, 0)

    def count_keys(pred):
        per_step = SCORE_TILE // KEY_TILE

        def count_tiles(u, cnt):
            for j in range(per_step):
                t = u * per_step + j
                kpos = t * KEY_TILE + lax.broadcasted_iota(jnp.int32, (KEY_TILE, KEY_TILE), 0)
                cnt = cnt + jnp.where(pred(key_ref[t], kpos), 1.0, 0.0)
            return cnt

        cnt = lax.fori_loop(0, pl.cdiv(n_vis, per_step), count_tiles, jnp.zeros((KEY_TILE, KEY_TILE), F32))
        return jnp.sum(cnt, axis=0, keepdims=True)

    thr = jnp.full((1, KEY_TILE), INT32_MIN, jnp.int32)
    for bit in range(31, -1, -1):
        cand = thr + jnp.int32(-2 ** 31 if bit == 31 else 2 ** bit)
        thr = jnp.where(count_keys(lambda key, kpos, cand=cand: key >= cand) >= float(topk), cand, thr)
    thr = jnp.maximum(thr, jnp.int32(INT32_MIN + 1))

    n_above = count_keys(lambda key, kpos: key > thr)
    n_tied = count_keys(lambda key, kpos: key == thr)
    ties_wanted = float(topk) - n_above
    pos_bits = (key_ref.shape[0] * KEY_TILE - 1).bit_length()

    def find_last_tie():
        last = jnp.zeros((1, KEY_TILE), jnp.int32)
        for bit in range(pos_bits - 1, -1, -1):
            cand = last | jnp.int32(2 ** bit)
            before = count_keys(lambda key, kpos, cand=cand: (key == thr) & (kpos < cand))
            last = jnp.where(before < ties_wanted, cand, last)
        return last

    surplus = jnp.max(n_tied - ties_wanted) > 0.0
    last_tie = lax.cond(surplus, find_last_tie, lambda: jnp.full((1, KEY_TILE), 2 ** pos_bits - 1, jnp.int32))

    kvs = range(H_KV)
    for kv in kvs:
        for g in range(GROUP):
            c0 = (kv * GROUP + g) * DH_C
            qh = (qc_ref[:, c0:c0 + DH_C] * DH_C ** -0.5).astype(BF16)
            if pad_rows:
                qh = jnp.concatenate([qh, jnp.zeros((pad_rows, DH_C), BF16)], axis=0)
            qg_ref[kv, g * KEY_TILE:(g + 1) * KEY_TILE, :] = qh
    acc_ref[...] = jnp.zeros(acc_ref.shape, F32)
    m_ref[...] = jnp.full(m_ref.shape, MASKED, F32)
    l_ref[...] = jnp.zeros(l_ref.shape, F32)

    def attend_tile(t, carry):
        start = pl.multiple_of(t * KEY_TILE, KEY_TILE)
        key = key_ref[t]
        kpos = start + lax.broadcasted_iota(jnp.int32, (KEY_TILE, KEY_TILE), 0)
        sel = (key > thr) | ((key == thr) & (kpos <= last_tie))
        penalty = jnp.concatenate([jnp.where(sel, 0.0, MASKED)] * GROUP, axis=1)
        dd = jnp.minimum(qt - t, N_BIAS_NEAR)
        s = [_dot_nt(k_ref[0, pl.ds(start, KEY_TILE), kv * DH_C:(kv + 1) * DH_C], qg_ref[kv]) for kv in kvs]
        sm = [s[kv] + bias_ref[dd, kv] + penalty for kv in kvs]
        m_old = [m_ref[kv:kv + 1, :] for kv in kvs]
        m_new = [jnp.maximum(m_old[kv], jnp.max(sm[kv], axis=0, keepdims=True)) for kv in kvs]
        alpha = [jnp.exp(m_old[kv] - m_new[kv]) for kv in kvs]
        p = [jnp.exp(sm[kv] - m_new[kv]) for kv in kvs]
        pv = [lax.dot_general(v_ref[0, pl.ds(start, KEY_TILE), kv * DH_C:(kv + 1) * DH_C], p[kv].astype(BF16),
                              (((0,), (0,)), ((), ())), preferred_element_type=F32) for kv in kvs]
        for kv in kvs:
            l_ref[kv:kv + 1, :] = alpha[kv] * l_ref[kv:kv + 1, :] + jnp.sum(p[kv], axis=0, keepdims=True)
            acc_ref[kv] = alpha[kv] * acc_ref[kv] + pv[kv]
            m_ref[kv:kv + 1, :] = m_new[kv]
        return carry

    lax.fori_loop(0, n_vis, attend_tile, 0)
    for kv in kvs:
        out = (acc_ref[kv] / l_ref[kv:kv + 1, :]).T
        for g in range(GROUP):
            c0 = (kv * GROUP + g) * DH_C
            o_ref[:, c0:c0 + DH_C] = out[g * KEY_TILE:g * KEY_TILE + tq].astype(o_ref.dtype)


def dsa_bias_tiles(rel_bias):
    i = jnp.arange(KEY_TILE)
    d = jnp.arange(N_BIAS_NEAR + 1)
    rel = (i[None, :, None] - i[None, None, :]) - KEY_TILE * d[:, None, None]
    onehot = (t5_bucket(rel)[..., None] == jnp.arange(NUM_BUCKETS)).astype(F32)
    tiles = jnp.einsum('dkqn,nh->dkqh', onehot, rel_bias.astype(F32), precision=lax.Precision.HIGHEST)
    tiles = tiles.reshape(N_BIAS_NEAR + 1, KEY_TILE, KEY_TILE, H_KV, GROUP)
    return jnp.transpose(tiles, (0, 3, 1, 4, 2)).reshape(N_BIAS_NEAR + 1, H_KV, KEY_TILE, GROUP * KEY_TILE)


def dsa_attention(proj, row0, b, t, k_all, v_all, kidx_all, bias_tiles, p_len, seq_len, topk):
    tq = KEY_TILE if t % KEY_TILE == 0 else t
    assert p_len % KEY_TILE == 0 and KEY_TILE % tq == 0 and (tq == KEY_TILE or t == tq) and row0 % tq == 0
    nb = t // tq
    lp = k_all.shape[1]
    assert lp % SCORE_TILE == 0 and lp >= p_len + nb * KEY_TILE
    assert 8 * (MAX_DISTANCE / 8) ** (7 / 8) < KEY_TILE * N_BIAS_NEAR - (KEY_TILE - 1)
    rb = row0 // tq
    kern = partial(_dsa_kernel, tq=tq, q0_tile=p_len // KEY_TILE, seq_len=seq_len, topk=topk)
    return pl.pallas_call(
        kern,
        grid=(b, nb),
        in_specs=[pl.BlockSpec((tq, W_C), lambda bi, j: (rb + bi * nb + j, COL["q_c"][0] // W_C)),
                  pl.BlockSpec((tq, H_IDX * D_IDX), lambda bi, j: (rb + bi * nb + j, COL["q_i"][0] // (H_IDX * D_IDX))),
                  pl.BlockSpec((tq, LANE), lambda bi, j: (rb + bi * nb + j, COL["k_i"][0] // LANE)),
                  pl.BlockSpec((1, lp, H_KV * DH_C), lambda bi, j: (bi, 0, 0)),
                  pl.BlockSpec((1, lp, H_KV * DH_C), lambda bi, j: (bi, 0, 0)),
                  pl.BlockSpec((1, lp, 2 * D_IDX), lambda bi, j: (bi, 0, 0)),
                  pl.BlockSpec((N_BIAS_NEAR + 1, H_KV, KEY_TILE, GROUP * KEY_TILE), lambda bi, j: (0, 0, 0, 0))],
        out_specs=pl.BlockSpec((tq, W_C), lambda bi, j: (bi * nb + j, 0)),
        out_shape=jax.ShapeDtypeStruct((b * t, W_C), BF16),
        scratch_shapes=[pltpu.VMEM((lp // KEY_TILE, KEY_TILE, KEY_TILE), jnp.int32),
                        pltpu.VMEM((H_IDX // 2 * tq, 2 * D_IDX), BF16),
                        pltpu.VMEM((H_IDX // 2 * tq, 2 * SCORE_TILE), F32),
                        pltpu.VMEM((H_KV, GROUP * KEY_TILE, DH_C), BF16),
                        pltpu.VMEM((H_KV, DH_C, GROUP * KEY_TILE), F32),
                        pltpu.VMEM((SUBLANE, GROUP * KEY_TILE), F32),
                        pltpu.VMEM((SUBLANE, GROUP * KEY_TILE), F32)],
        compiler_params=_cparams(("parallel", "arbitrary")),
        name="dsa_attention",
    )(proj, proj, proj, k_all, v_all, kidx_all, bias_tiles)


def branches(proj_all, new_rows, row0, b, t, past_k, past_v, past_kidx, s_ret, s_delta, s_conv,
             ret_gn, conv_w, a_log, dt_bias, d_norm, bias_tiles):
    p_len = past_k.shape[1]
    seq_len = p_len + t
    topk = min(TOPK_MAX, seq_len // 4)
    lp = -(-seq_len // SCORE_TILE) * SCORE_TILE
    k_c, v_c, k_i, k_bf, v_bf, ki_bf = (x[row0:row0 + b * t].reshape(b, t, x.shape[-1]) for x in new_rows)

    o, w = COL["qkv_b"]
    qkv_tail = jnp.stack([proj_all[row0 + t - (CONV_W - 1) + j:row0 + b * t:t, o:o + w]
                          for j in range(CONV_W - 1)], axis=1)
    conv_new = jnp.concatenate([s_conv, qkv_tail], axis=1)[:, -(CONV_W - 1):]

    o_a, ret_new = retention(proj_all, row0, b, t, p_len, s_ret, ret_gn)
    o_b, delta_new = gated_delta(proj_all, row0, b, t, s_conv, s_delta, conv_w, a_log, dt_bias, d_norm)

    def all_keys(past, new, reps=1):
        if p_len == 0 and lp == seq_len:
            return new
        w = past.shape[-1] * reps
        past = jnp.tile(past.reshape(b, p_len, -1).astype(BF16), (1, 1, reps))
        return jnp.concatenate([past, new, jnp.zeros((b, lp - seq_len, w), BF16)], axis=1)

    o_c = dsa_attention(proj_all, row0, b, t, all_keys(past_k.reshape(b, p_len, KV_W), k_bf),
                        all_keys(past_v.reshape(b, p_len, KV_W), v_bf), all_keys(past_kidx, ki_bf, 2),
                        bias_tiles, p_len, seq_len, topk)
    caches = (k_c.reshape(b, t, H_KV, DH_C), v_c.reshape(b, t, H_KV, DH_C), k_i, ret_new, delta_new, conv_new)
    return jnp.stack([o_a, o_b, o_c]), caches


def _prep_w_in(w):
    parts = [w[:, MODEL_COL[n][0]:MODEL_COL[n][0] + MODEL_COL[n][1]] for n in _KERNEL_ORDER]
    parts.append(jnp.zeros((w.shape[0], D_IN_PAD - D_IN), w.dtype))
    return jnp.concatenate(parts, axis=1).astype(BF16)


def kernel(x_prompt, x_sample, cache_k, cache_v, cache_kidx, state_ret, state_delta, state_conv, norm_mix, w_in, ret_gn, conv_w, delta_a_log, delta_dt_bias, delta_norm, rel_bias, w_branch_a, w_branch_b, w_branch_c, w_out, norm_ffn, w_ffn_gate, w_ffn_up, w_ffn_down, norm_final):
    dt = x_prompt.dtype
    bp, tp, _ = x_prompt.shape
    bs, ts, _ = x_sample.shape
    depth = w_in.shape[0]
    n_p = bp * tp
    zk = jnp.zeros((bp, 0, H_KV, DH_C), dt)
    zkidx = jnp.zeros((bp, 0, D_IDX), dt)
    zret = jnp.zeros((bp, H_A, DK_A, DV_A), dt)
    zdelta = jnp.zeros((bp, H_B, DK_B, DV_B), dt)
    zconv = jnp.zeros((bp, CONV_W - 1, CONV_CH), dt)

    h = jnp.concatenate([x_prompt.reshape(n_p, D_MODEL), x_sample.reshape(bs * ts, D_MODEL)], axis=0)
    new_p, new_s = [], []
    bias_tiles = dsa_bias_tiles(rel_bias)
    for l in range(depth):
        u = rmsnorm(h, norm_mix[l], BF16)
        proj, *new_rows = matmul_in(u, _prep_w_in(w_in[l]))
        bw = (ret_gn[l], conv_w[l], delta_a_log[l], delta_dt_bias[l], delta_norm[l], bias_tiles)
        o_p, sp = branches(proj, new_rows, 0, bp, tp, zk, zk, zkidx, zret, zdelta, zconv, *bw)
        o_s, ss = branches(proj, new_rows, n_p, bs, ts, cache_k[l], cache_v[l], cache_kidx[l],
                           state_ret[l], state_delta[l], state_conv[l], *bw)
        new_p.append(sp)
        new_s.append(ss)
        o_abc = jnp.concatenate([o_p, o_s], axis=1)
        w_br = jnp.stack([w_branch_a[l], w_branch_b[l], w_branch_c[l]]).astype(BF16)
        merged = merge_branches(o_abc, w_br, proj)
        h = matmul_residual(merged, w_out[l].astype(BF16), h)
        u2 = rmsnorm(h, norm_ffn[l], BF16)
        act = ffn_gate_up(u2, w_ffn_gate[l].astype(BF16), w_ffn_up[l].astype(BF16))
        h = ffn_down(act, w_ffn_down[l].astype(BF16), h)

    y = rmsnorm(h, norm_final, F32)
    y_prompt = y[:n_p].reshape(bp, tp, D_MODEL)
    y_sample = y[n_p:].reshape(bs, ts, D_MODEL)
    k_p, v_p, kidx_p, ret_p, delta_p, conv_p = [jnp.stack([s[i] for s in new_p]) for i in range(6)]
    k_s, v_s, kidx_s, ret_s, delta_s, conv_s = [jnp.stack([s[i] for s in new_s]) for i in range(6)]
    return (y_prompt, y_sample, k_p, v_p, kidx_p, ret_p, delta_p, conv_p,
            k_s, v_s, kidx_s, ret_s, delta_s, conv_s)
```

```python
import math
from functools import partial

import jax
import jax.numpy as jnp
from jax import lax
from jax.experimental import pallas as pl
from jax.experimental.pallas import tpu as pltpu

F32 = jnp.float32
BF16 = jnp.bfloat16

D_MODEL = 4096
CHUNK = 64
EPS = 1e-6
H_A, DK_A, DV_A = 8, 128, 256
W_A = H_A * DV_A
ROPE_BASE = 10000.0
H_B, DK_B, DV_B = 16, 128, 128
W_B = H_B * DV_B
CONV_W = 4
CONV_CH = 2 * H_B * DK_B + H_B * DV_B
H_C, H_KV, DH_C = 16, 4, 128
W_C = H_C * DH_C
KV_W = H_KV * DH_C
H_IDX, D_IDX = 32, 64
TOPK_MAX = 256
NUM_BUCKETS = 32
MAX_DISTANCE = 1024

LANE = 128
SUBLANE = 8
VMEM_LIMIT_V7X = 56 * 1024 * 1024
TM = 1024
TN = 1024
TIME_BLOCK = 512

_MODEL_ORDER = [("q_a", H_A * DK_A), ("k_a", H_A * DK_A), ("v_a", W_A), ("g_a", W_A),
                ("qkv_b", CONV_CH), ("z_b", W_B), ("a_b", H_B), ("b_b", H_B),
                ("q_c", W_C), ("k_c", KV_W), ("v_c", KV_W), ("q_i", H_IDX * D_IDX), ("k_i", D_IDX), ("w_i", H_IDX),
                ("gate_a", D_MODEL), ("gate_b", D_MODEL), ("gate_c", D_MODEL)]
_KERNEL_ORDER = ["q_a", "k_a", "v_a", "g_a", "qkv_b", "z_b", "q_c", "q_i", "k_c", "v_c",
                 "gate_a", "gate_b", "gate_c", "k_i", "w_i", "a_b", "b_b"]


def _offsets(order, widths):
    table, off = {}, 0
    for name in order:
        table[name] = (off, widths[name])
        off += widths[name]
    return table, off


_WIDTH = dict(_MODEL_ORDER)
MODEL_COL, D_IN = _offsets([n for n, _ in _MODEL_ORDER], _WIDTH)
COL, _ = _offsets(_KERNEL_ORDER, _WIDTH)
D_IN_PAD = -(-D_IN // TN) * TN


def _cparams(sem):
    return pltpu.CompilerParams(dimension_semantics=sem, vmem_limit_bytes=VMEM_LIMIT_V7X)


def _rmsnorm_kernel(x_ref, g_ref, o_ref):
    x = x_ref[...]
    y = x * lax.rsqrt(jnp.mean(x * x, axis=-1, keepdims=True) + EPS)
    o_ref[...] = (y * g_ref[...]).astype(o_ref.dtype)


def rmsnorm(x, g, out_dtype, row0=0, rows=None, tr=512):
    d = x.shape[1]
    m = x.shape[0] if rows is None else rows
    assert row0 % tr == 0 and m % tr == 0
    rb = row0 // tr
    return pl.pallas_call(
        _rmsnorm_kernel,
        grid=(m // tr,),
        in_specs=[pl.BlockSpec((tr, d), lambda i: (rb + i, 0)), pl.BlockSpec((1, d), lambda i: (0, 0))],
        out_specs=pl.BlockSpec((tr, d), lambda i: (i, 0)),
        out_shape=jax.ShapeDtypeStruct((m, d), out_dtype),
        compiler_params=_cparams(("parallel",)),
        name="rmsnorm",
    )(x, g.reshape(1, d))


def _mm_in_kernel(a_ref, b_ref, o_ref, k_ref, v_ref, ki_ref, kb_ref, vb_ref, kib_ref, *, kv_block, small_block):
    j = pl.program_id(1)
    o_ref[...] = jnp.dot(a_ref[...], b_ref[...], preferred_element_type=F32)

    @pl.when(j == kv_block)
    def _():
        k, v = o_ref[:, :KV_W], o_ref[:, KV_W:2 * KV_W]
        k_ref[...] = k
        v_ref[...] = v
        kb_ref[...] = k.astype(BF16)
        vb_ref[...] = v.astype(BF16)

    @pl.when(j == small_block)
    def _():
        ki = o_ref[:, :D_IDX]
        ki_ref[...] = ki
        kib_ref[...] = jnp.concatenate([ki, ki], axis=1).astype(BF16)


def matmul_in(a, b, tn=TN):
    m, k = a.shape
    n = b.shape[1]
    assert COL["k_c"][0] % tn == 0 and COL["v_c"][0] == COL["k_c"][0] + KV_W and COL["k_i"][0] % tn == 0
    kern = partial(_mm_in_kernel, kv_block=COL["k_c"][0] // tn, small_block=COL["k_i"][0] // tn)

    def row_spec(w):
        return pl.BlockSpec((TM, w), lambda i, j: (i, 0))

    return pl.pallas_call(
        kern,
        grid=(m // TM, n // tn),
        in_specs=[pl.BlockSpec((TM, k), lambda i, j: (i, 0), pipeline_mode=pl.Buffered(1)),
                  pl.BlockSpec((k, tn), lambda i, j: (0, j))],
        out_specs=[pl.BlockSpec((TM, tn), lambda i, j: (i, j)), row_spec(KV_W), row_spec(KV_W), row_spec(D_IDX),
                   row_spec(KV_W), row_spec(KV_W), row_spec(2 * D_IDX)],
        out_shape=[jax.ShapeDtypeStruct((m, n), F32), jax.ShapeDtypeStruct((m, KV_W), F32),
                   jax.ShapeDtypeStruct((m, KV_W), F32), jax.ShapeDtypeStruct((m, D_IDX), F32),
                   jax.ShapeDtypeStruct((m, KV_W), BF16), jax.ShapeDtypeStruct((m, KV_W), BF16),
                   jax.ShapeDtypeStruct((m, 2 * D_IDX), BF16)],
        compiler_params=_cparams(("parallel", "arbitrary")),
        name="matmul_in",
    )(a, b)


N_BRANCH = 3


def _merge_kernel(*refs):
    o_refs, w_refs, g_refs = (refs[i * N_BRANCH:(i + 1) * N_BRANCH] for i in range(3))
    out_ref = refs[3 * N_BRANCH]
    acc = None
    for o_ref, w_ref, g_ref in zip(o_refs, w_refs, g_refs):
        y = jnp.dot(o_ref[0], w_ref[0], preferred_element_type=F32) * jax.nn.sigmoid(g_ref[...])
        acc = y if acc is None else acc + y
    out_ref[...] = acc.astype(out_ref.dtype)


def merge_branches(o_abc, w_br, proj, tn=TN // 2):
    _, m, k = o_abc.shape
    n = w_br.shape[2]
    gate0 = COL["gate_a"][0] // tn
    nj = n // tn

    def per_branch(make):
        return [make(r) for r in range(N_BRANCH)]

    return pl.pallas_call(
        _merge_kernel,
        grid=(m // TM, nj),
        in_specs=per_branch(lambda r: pl.BlockSpec((1, TM, k), lambda i, j: (r, i, 0), pipeline_mode=pl.Buffered(1)))
        + per_branch(lambda r: pl.BlockSpec((1, k, tn), lambda i, j: (r, 0, j)))
        + per_branch(lambda r: pl.BlockSpec((TM, tn), lambda i, j: (i, gate0 + r * nj + j))),
        out_specs=pl.BlockSpec((TM, tn), lambda i, j: (i, j)),
        out_shape=jax.ShapeDtypeStruct((m, n), BF16),
        compiler_params=_cparams(("parallel", "parallel")),
        name="merge_branches",
    )(*([o_abc] * N_BRANCH), *([w_br] * N_BRANCH), *([proj] * N_BRANCH))


def _mm_res_kernel(a_ref, b_ref, h_ref, o_ref):
    o_ref[...] = h_ref[...] + jnp.dot(a_ref[...], b_ref[...], preferred_element_type=F32)


def matmul_residual(a, b, h, tn=TN):
    m, k = a.shape
    n = b.shape[1]
    return pl.pallas_call(
        _mm_res_kernel,
        grid=(m // TM, n // tn),
        in_specs=[pl.BlockSpec((TM, k), lambda i, j: (i, 0)), pl.BlockSpec((k, tn), lambda i, j: (0, j)),
                  pl.BlockSpec((TM, tn), lambda i, j: (i, j))],
        out_specs=pl.BlockSpec((TM, tn), lambda i, j: (i, j)),
        out_shape=jax.ShapeDtypeStruct((m, n), F32),
        compiler_params=_cparams(("parallel", "parallel")),
        name="matmul_out",
    )(a, b, h)


def _gateup_kernel(u_ref, wg_ref, wu_ref, o_ref):
    u = u_ref[...]
    g = jnp.dot(u, wg_ref[...], preferred_element_type=F32)
    up = jnp.dot(u, wu_ref[...], preferred_element_type=F32)
    o_ref[...] = (jax.nn.silu(g) * up).astype(o_ref.dtype)


def ffn_gate_up(u, wg, wu, tn=512):
    m, k = u.shape
    n = wg.shape[1]
    return pl.pallas_call(
        _gateup_kernel,
        grid=(m // TM, pl.cdiv(n, tn)),
        in_specs=[pl.BlockSpec((TM, k), lambda i, j: (i, 0)),
                  pl.BlockSpec((k, tn), lambda i, j: (0, j)),
                  pl.BlockSpec((k, tn), lambda i, j: (0, j))],
        out_specs=pl.BlockSpec((TM, tn), lambda i, j: (i, j)),
        out_shape=jax.ShapeDtypeStruct((m, n), BF16),
        compiler_params=_cparams(("parallel", "parallel")),
        name="ffn_gate_up",
    )(u, wg, wu)


def ffn_down(a, b, h, tm=768, tn=512):
    m, k = a.shape
    n = b.shape[1]
    return pl.pallas_call(
        _mm_res_kernel,
        grid=(m // tm, n // tn),
        in_specs=[pl.BlockSpec((tm, k), lambda i, j: (i, 0), pipeline_mode=pl.Buffered(1)),
                  pl.BlockSpec((k, tn), lambda i, j: (0, j)),
                  pl.BlockSpec((tm, tn), lambda i, j: (i, j))],
        out_specs=pl.BlockSpec((tm, tn), lambda i, j: (i, j)),
        out_shape=jax.ShapeDtypeStruct((m, n), F32),
        compiler_params=_cparams(("parallel", "parallel")),
        name="ffn_down",
    )(a, b, h)


RET_HEADS = 4


def _retention_kernel(q_ref, k_ref, v_ref, g_ref, cos_ref, sin_ref, idec_ref, qdec_ref, kdec_ref, cdec_ref, gn_ref,
                      s0_ref, o_ref, sfin_ref, s_ref, *, tb, c):
    tblk = pl.program_id(2)
    heads = range(RET_HEADS)

    @pl.when(tblk == 0)
    def _():
        s_ref[...] = s0_ref[0]

    def rotate(x, cos, sin):
        return x * cos + pltpu.roll(x, DK_A // 2, 1) * sin

    def chunk_step(ci, carry):
        c0 = pl.multiple_of(ci * c, c)
        cos = cos_ref[pl.ds(c0, c), :]
        sin = sin_ref[pl.ds(c0, c), :]
        qr = [rotate(q_ref[pl.ds(c0, c), h * DK_A:(h + 1) * DK_A], cos, sin) for h in heads]
        kr = [rotate(k_ref[pl.ds(c0, c), h * DK_A:(h + 1) * DK_A], cos, sin) * DK_A ** -0.5 for h in heads]
        v = [v_ref[pl.ds(c0, c), h * DV_A:(h + 1) * DV_A].astype(BF16) for h in heads]
        s = [s_ref[h] for h in heads]
        att = [_dot_nt(qr[h].astype(BF16), kr[h].astype(BF16)) * idec_ref[h] for h in heads]
        o = [_bdot(att[h], v[h]) + _bdot(qr[h] * qdec_ref[h], s[h]) for h in heads]
        for h in heads:
            s_ref[h] = s[h] * cdec_ref[h, 0:1, :] + lax.dot_general(
                (kr[h] * kdec_ref[h]).astype(BF16), v[h], (((0,), (0,)), ((), ())), preferred_element_type=F32)
        for h in heads:
            mu = jnp.mean(o[h], axis=-1, keepdims=True)
            d = o[h] - mu
            var = jnp.mean(d * d, axis=-1, keepdims=True)
            cols = slice(h * DV_A, (h + 1) * DV_A)
            g = g_ref[pl.ds(c0, c), cols]
            o_ref[pl.ds(c0, c), cols] = (
                g * jax.nn.sigmoid(g) * (d * lax.rsqrt(var + EPS) * gn_ref[:, cols])).astype(o_ref.dtype)
        return carry

    lax.fori_loop(0, tb // c, chunk_step, 0)

    @pl.when(tblk == pl.num_programs(2) - 1)
    def _():
        sfin_ref[0] = s_ref[...]


def retention(proj, row0, b, t, p_len, s_ret, ret_gn):
    c = 2 * CHUNK if t % (2 * CHUNK) == 0 else t
    tb = TIME_BLOCK if t % TIME_BLOCK == 0 else t
    assert tb % c == 0 and row0 % tb == 0
    nt = t // tb
    rb = row0 // tb
    half = DK_A // 2
    inv = ROPE_BASE ** (-jnp.arange(half, dtype=F32) / half)
    ang = (p_len + jnp.arange(t)).astype(F32)[:, None] * inv[None, :]
    cos2 = jnp.concatenate([jnp.cos(ang), jnp.cos(ang)], axis=1)
    sin2 = jnp.concatenate([-jnp.sin(ang), jnp.sin(ang)], axis=1)
    log_gamma = jnp.log(1.0 - 2.0 ** (-5.0 - jnp.arange(H_A, dtype=F32)))
    i = jnp.arange(c, dtype=F32)
    rel = i[:, None] - i[None, :]
    idec = jnp.exp(jnp.where(rel[None] >= 0, rel[None] * log_gamma[:, None, None], -jnp.inf))
    qdec = jnp.broadcast_to(jnp.exp((i[None, :] + 1.0) * log_gamma[:, None])[:, :, None], (H_A, c, DK_A))
    kdec = jnp.broadcast_to(jnp.exp((c - 1.0 - i[None, :]) * log_gamma[:, None])[:, :, None], (H_A, c, DK_A))
    cdec = jnp.broadcast_to(jnp.exp(c * log_gamma)[:, None, None], (H_A, SUBLANE, DV_A))
    wk, wv = RET_HEADS * DK_A, RET_HEADS * DV_A
    qb0 = COL["q_a"][0] // wk
    kb0 = COL["k_a"][0] // wk
    vb0 = COL["v_a"][0] // wv
    gb0 = COL["g_a"][0] // wv
    sspec = pl.BlockSpec((1, RET_HEADS, DK_A, DV_A), lambda bi, h, k: (bi, h, 0, 0))
    return pl.pallas_call(
        partial(_retention_kernel, tb=tb, c=c),
        grid=(b, H_A // RET_HEADS, nt),
        in_specs=[pl.BlockSpec((tb, wk), lambda bi, h, k: (rb + bi * nt + k, qb0 + h)),
                  pl.BlockSpec((tb, wk), lambda bi, h, k: (rb + bi * nt + k, kb0 + h)),
                  pl.BlockSpec((tb, wv), lambda bi, h, k: (rb + bi * nt + k, vb0 + h)),
                  pl.BlockSpec((tb, wv), lambda bi, h, k: (rb + bi * nt + k, gb0 + h)),
                  pl.BlockSpec((tb, DK_A), lambda bi, h, k: (k, 0)),
                  pl.BlockSpec((tb, DK_A), lambda bi, h, k: (k, 0)),
                  pl.BlockSpec((RET_HEADS, c, c), lambda bi, h, k: (h, 0, 0)),
                  pl.BlockSpec((RET_HEADS, c, DK_A), lambda bi, h, k: (h, 0, 0)),
                  pl.BlockSpec((RET_HEADS, c, DK_A), lambda bi, h, k: (h, 0, 0)),
                  pl.BlockSpec((RET_HEADS, SUBLANE, DV_A), lambda bi, h, k: (h, 0, 0)),
                  pl.BlockSpec((1, wv), lambda bi, h, k: (0, h)),
                  sspec],
        out_specs=[pl.BlockSpec((tb, wv), lambda bi, h, k: (bi * nt + k, h)), sspec],
        out_shape=[jax.ShapeDtypeStruct((b * t, W_A), BF16),
                   jax.ShapeDtypeStruct((b, H_A, DK_A, DV_A), F32)],
        scratch_shapes=[pltpu.VMEM((RET_HEADS, DK_A, DV_A), F32)],
        compiler_params=_cparams(("parallel", "parallel", "arbitrary")),
        name="retention",
    )(proj, proj, proj, proj, cos2, sin2, idec, qdec, kdec, cdec, ret_gn.reshape(1, W_A), s_ret)


DELTA_HEADS = 8
DELTA_W = DELTA_HEADS * DK_B
HIST_ROWS = SUBLANE
INV_BASE = 16


def _bdot(a, b):
    return jnp.dot(a.astype(BF16), b.astype(BF16), preferred_element_type=F32)


def _hdot(a, b):
    return jnp.dot(a, b, preferred_element_type=F32, precision=lax.Precision.HIGHEST)


def _split_bf16(x):
    hi = x.astype(BF16)
    return hi, (x - hi.astype(F32)).astype(BF16)


def _dot3(a, b):
    a_hi, a_lo = _split_bf16(a)
    b_hi, b_lo = _split_bf16(b)
    dot = partial(jnp.dot, preferred_element_type=F32)
    return dot(a_hi, b_hi) + (dot(a_hi, b_lo) + dot(a_lo, b_hi))


def _unit_lower_inverse(mats, c):
    ii = lax.broadcasted_iota(jnp.int32, (c, c), 0)
    jj = lax.broadcasted_iota(jnp.int32, (c, c), 1)
    eye = jnp.where(ii == jj, 1.0, 0.0).astype(F32)
    ps = [jnp.where((ii // INV_BASE) == (jj // INV_BASE), a, 0.0) for a in mats]
    ts = [eye - p for p in ps]
    for _ in range(int(math.log2(INV_BASE)) - 1):
        ps = [_dot3(p, p) for p in ps]
        ts = [_dot3(t, eye + p) for t, p in zip(ts, ps)]
    size = INV_BASE
    while size < c:
        half_blocks = ((ii // (2 * size)) == (jj // (2 * size))) & ((ii // size) != (jj // size))
        tl = [_dot3(t, jnp.where(half_blocks, a, 0.0)) for t, a in zip(ts, mats)]
        ts = [t - _dot3(x, t) for t, x in zip(ts, tl)]
        size *= 2
    return ts


def _delta_kernel(xq_ref, xk_ref, xv_ref, z_ref, small_ref, cwq_ref, cwk_ref, cwv_ref, hq_ref, hk_ref, hv_ref,
                  par_ref, dn_ref, s0_ref, o_ref, sfin_ref,
                  bq_ref, bk_ref, bv_ref, qn_ref, kn_ref, vv_ref, g_ref, beta_ref, s_ref, w_ref, qk_ref, el_ref,
                  *, tb, c):
    hg = pl.program_id(1)
    tblk = pl.program_id(2)

    @pl.when(tblk == 0)
    def _():
        bq_ref[0:HIST_ROWS, :] = hq_ref[0]
        bk_ref[0:HIST_ROWS, :] = hk_ref[0]
        bv_ref[0:HIST_ROWS, :] = hv_ref[0]
        s_ref[...] = s0_ref[0]

    def conv_silu(x_ref, buf_ref, cw_ref):
        buf_ref[HIST_ROWS:HIST_ROWS + tb, :] = x_ref[...]
        y = jnp.zeros((tb, DELTA_W), F32)
        for j in range(CONV_W):
            lo = HIST_ROWS - (CONV_W - 1) + j
            y = y + buf_ref[lo:lo + tb, :] * cw_ref[j:j + 1, :]
        buf_ref[0:HIST_ROWS, :] = buf_ref[tb:tb + HIST_ROWS, :]
        return y * jax.nn.sigmoid(y)

    def l2n(x):
        parts = []
        for h in range(DELTA_HEADS):
            xh = x[:, h * DK_B:(h + 1) * DK_B]
            parts.append(xh * lax.rsqrt(jnp.sum(xh * xh, axis=-1, keepdims=True) + EPS))
        return jnp.concatenate(parts, axis=1)

    qn_ref[...] = l2n(conv_silu(xq_ref, bq_ref, cwq_ref)) * DK_B ** -0.5
    kn_ref[...] = l2n(conv_silu(xk_ref, bk_ref, cwk_ref))
    vv_ref[...] = conv_silu(xv_ref, bv_ref, cwv_ref)
    sm = small_ref[...]
    x = sm + par_ref[1:2, :]
    softplus = jnp.maximum(x, 0.0) + jnp.log1p(jnp.exp(-jnp.abs(x)))
    g_ref[...] = -jnp.exp(par_ref[0:1, :]) * softplus
    beta_ref[...] = jax.nn.sigmoid(sm)

    ii = lax.broadcasted_iota(jnp.int32, (c, c), 0)
    jj = lax.broadcasted_iota(jnp.int32, (c, c), 1)
    incl = ii >= jj
    strict = ii > jj
    diag = ii == jj
    tril_ones = jnp.where(incl, 1.0, 0.0).astype(F32)
    a_lane = COL["a_b"][0] - COL["k_i"][0]
    b_lane = COL["b_b"][0] - COL["k_i"][0]

    n_chunks = tb // c
    p_unroll = 2 if n_chunks % 2 == 0 else 1

    def prepare_chunks(step, carry):
        lane = lax.broadcasted_iota(jnp.int32, (c, LANE), 1)
        items = []
        for j in range(p_unroll):
            ci = step * p_unroll + j
            c0 = pl.multiple_of(ci * c, c)
            gcum_all = _hdot(tril_ones, g_ref[pl.ds(c0, c), :])
            beta_all = beta_ref[pl.ds(c0, c), :]
            for h in range(DELTA_HEADS):
                items.append((ci, c0, h, gcum_all, beta_all))
        cols = [slice(h * DK_B, (h + 1) * DK_B) for _, _, h, _, _ in items]
        gc = [jnp.sum(jnp.where(lane == a_lane + hg * DELTA_HEADS + h, ga, 0.0), axis=-1, keepdims=True)
              for _, _, h, ga, _ in items]
        bc = [jnp.sum(jnp.where(lane == b_lane + hg * DELTA_HEADS + h, ba, 0.0), axis=-1, keepdims=True)
              for _, _, h, _, ba in items]
        gr = [jnp.sum(jnp.where(diag, jnp.broadcast_to(g, (c, c)), 0.0), axis=0, keepdims=True) for g in gc]
        g_last = [jnp.sum(jnp.where(ii[:, 0:1] == c - 1, g, 0.0), axis=0, keepdims=True) for g in gc]
        gam = [jnp.exp(jnp.where(incl, g - r, -jnp.inf)) for g, r in zip(gc, gr)]
        eg = [jnp.exp(g) for g in gc]
        qh = [qn_ref[pl.ds(it[1], c), cs] for it, cs in zip(items, cols)]
        kh = [kn_ref[pl.ds(it[1], c), cs] for it, cs in zip(items, cols)]
        vh = [vv_ref[pl.ds(it[1], c), cs] for it, cs in zip(items, cols)]
        kb = [k * b for k, b in zip(kh, bc)]
        a = [jnp.where(strict, _dot_nt(x.astype(BF16), k.astype(BF16)) * gm, 0.0) for x, k, gm in zip(kb, kh, gam)]
        qk = [_dot_nt(q.astype(BF16), k.astype(BF16)) * gm for q, k, gm in zip(qh, kh, gam)]
        tinv = _unit_lower_inverse(a, c)
        u = [_bdot(t, v * b) for t, v, b in zip(tinv, vh, bc)]
        w = [_bdot(t, x * e) for t, x, e in zip(tinv, kb, eg)]
        for n, (ci, c0, h, _, _) in enumerate(items):
            rows = pl.ds(c0, c)
            vv_ref[rows, cols[n]] = u[n]
            w_ref[rows, cols[n]] = w[n]
            qn_ref[rows, cols[n]] = qh[n] * eg[n]
            kn_ref[rows, cols[n]] = kh[n] * jnp.exp(g_last[n] - gc[n])
            qk_ref[ci * DELTA_HEADS + h] = qk[n]
            el_ref[ci * DELTA_HEADS + h] = jnp.broadcast_to(jnp.exp(g_last[n]), (SUBLANE, LANE))
        return carry

    lax.fori_loop(0, n_chunks // p_unroll, prepare_chunks, 0)

    def state_step(ci, carry):
        rows = pl.ds(pl.multiple_of(ci * c, c), c)
        heads = range(DELTA_HEADS)
        cols = [slice(h * DK_B, (h + 1) * DK_B) for h in heads]
        s = [s_ref[h] for h in heads]
        v_new = [vv_ref[rows, cols[h]] - _bdot(w_ref[rows, cols[h]], s[h]) for h in heads]
        o = [_bdot(qn_ref[rows, cols[h]], s[h]) + _bdot(qk_ref[ci * DELTA_HEADS + h], v_new[h]) for h in heads]
        for h in heads:
            s_ref[h] = s[h] * el_ref[ci * DELTA_HEADS + h][0:1, :] + lax.dot_general(
                kn_ref[rows, cols[h]].astype(BF16), v_new[h].astype(BF16), (((0,), (0,)), ((), ())),
                preferred_element_type=F32)
        for h in heads:
            on = o[h] * lax.rsqrt(jnp.mean(o[h] * o[h], axis=-1, keepdims=True) + EPS) * dn_ref[...]
            zh = z_ref[rows, cols[h]]
            o_ref[rows, cols[h]] = (on * (zh * jax.nn.sigmoid(zh))).astype(o_ref.dtype)
        return carry

    lax.fori_loop(0, n_chunks, state_step, 0)

    @pl.when(tblk == pl.num_programs(2) - 1)
    def _():
        sfin_ref[0] = s_ref[...]


def gated_delta(proj, row0, b, t, s_conv, s_delta, conv_w, a_log, dt_bias, d_norm):
    c = CHUNK if t % CHUNK == 0 else t
    tb = TIME_BLOCK if t % TIME_BLOCK == 0 else t
    assert tb % c == 0 and c % INV_BASE == 0 and tb >= HIST_ROWS and row0 % tb == 0
    nt = t // tb
    rb = row0 // tb
    ngrp = H_B // DELTA_HEADS
    qcol = COL["qkv_b"][0] // DELTA_W
    hist = jnp.concatenate([jnp.zeros((b, HIST_ROWS - (CONV_W - 1), CONV_CH), F32), s_conv], axis=1)
    par = jnp.zeros((SUBLANE, LANE), F32)
    a_lane = COL["a_b"][0] - COL["k_i"][0]
    par = par.at[0, a_lane:a_lane + H_B].set(a_log).at[1, a_lane:a_lane + H_B].set(dt_bias)

    def xspec(part):
        return pl.BlockSpec((tb, DELTA_W), lambda bi, g, k: (rb + bi * nt + k, qcol + part * ngrp + g))

    def wspec(part):
        return pl.BlockSpec((CONV_W, DELTA_W), lambda bi, g, k: (0, part * ngrp + g))

    def hspec(part):
        return pl.BlockSpec((1, HIST_ROWS, DELTA_W), lambda bi, g, k: (bi, 0, part * ngrp + g))

    sspec = pl.BlockSpec((1, DELTA_HEADS, DK_B, DV_B), lambda bi, g, k: (bi, g, 0, 0))
    return pl.pallas_call(
        partial(_delta_kernel, tb=tb, c=c),
        grid=(b, ngrp, nt),
        in_specs=[xspec(0), xspec(1), xspec(2),
                  pl.BlockSpec((tb, DELTA_W), lambda bi, g, k: (rb + bi * nt + k, COL["z_b"][0] // DELTA_W + g)),
                  pl.BlockSpec((tb, LANE), lambda bi, g, k: (rb + bi * nt + k, COL["k_i"][0] // LANE)),
                  wspec(0), wspec(1), wspec(2), hspec(0), hspec(1), hspec(2),
                  pl.BlockSpec((SUBLANE, LANE), lambda bi, g, k: (0, 0)),
                  pl.BlockSpec((1, DV_B), lambda bi, g, k: (0, 0)),
                  sspec],
        out_specs=[pl.BlockSpec((tb, DELTA_W), lambda bi, g, k: (bi * nt + k, g)), sspec],
        out_shape=[jax.ShapeDtypeStruct((b * t, W_B), BF16),
                   jax.ShapeDtypeStruct((b, H_B, DK_B, DV_B), F32)],
        scratch_shapes=[pltpu.VMEM((tb + HIST_ROWS, DELTA_W), F32)] * 3
        + [pltpu.VMEM((tb, DELTA_W), F32)] * 3
        + [pltpu.VMEM((tb, LANE), F32)] * 2
        + [pltpu.VMEM((DELTA_HEADS, DK_B, DV_B), F32),
           pltpu.VMEM((tb, DELTA_W), F32),
           pltpu.VMEM((tb // c * DELTA_HEADS, c, c), F32),
           pltpu.VMEM((tb // c * DELTA_HEADS, SUBLANE, LANE), F32)],
        compiler_params=_cparams(("parallel", "parallel", "arbitrary")),
        name="gated_delta",
    )(proj, proj, proj, proj, proj, conv_w, conv_w, conv_w, hist, hist, hist, par, d_norm.reshape(1, DV_B), s_delta)


def t5_bucket(rel):
    half = NUM_BUCKETS // 2
    exact = half // 2
    n = jnp.abs(rel)
    large = exact + (jnp.log(jnp.maximum(n, 1).astype(jnp.float32) / exact)
                     / math.log(MAX_DISTANCE / exact) * (half - exact)).astype(jnp.int32)
    large = jnp.minimum(large, half - 1)
    return jnp.where(rel > 0, half, 0) + jnp.where(n < exact, n, large)


KEY_TILE = LANE
SCORE_TILE = 2 * KEY_TILE
N_BIAS_NEAR = 6
MASKED = -1e30
INT32_MIN = -2 ** 31
GROUP = H_C // H_KV


def _dot_nt(a, b):
    return lax.dot_general(a, b, (((1,), (1,)), ((), ())), preferred_element_type=F32)


def _dsa_kernel(qc_ref, qi_ref, small_ref, k_ref, v_ref, kidx_ref, bias_ref, o_ref,
                key_ref, qis_ref, wb_ref, qg_ref, acc_ref, m_ref, l_ref, *, tq, q0_tile, seq_len, topk):
    qt = q0_tile + pl.program_id(1)
    n_vis = qt + 1
    row = lax.broadcasted_iota(jnp.int32, (tq, SCORE_TILE), 0)
    lane = lax.broadcasted_iota(jnp.int32, (tq, SCORE_TILE), 1)
    q_chunk = (qt * KEY_TILE + row) // CHUNK
    pad_rows = KEY_TILE - tq

    qi = (qi_ref[...] * D_IDX ** -0.5).astype(BF16)
    w_lane = COL["w_i"][0] - COL["k_i"][0]
    w = small_ref[:, w_lane:w_lane + H_IDX] * H_IDX ** -0.5
    for i in range(H_IDX // 2):
        qis_ref[i * tq:(i + 1) * tq, :] = qi[:, i * LANE:(i + 1) * LANE]
        for j in range(2):
            wb_ref[i * tq:(i + 1) * tq, j * SCORE_TILE:(j + 1) * SCORE_TILE] = jnp.broadcast_to(
                w[:, 2 * i + j:2 * i + j + 1], (tq, SCORE_TILE))
    first_head = lax.broadcasted_iota(jnp.int32, (SCORE_TILE, LANE), 1) < D_IDX

    def score_tile(u, carry):
        start = pl.multiple_of(u * SCORE_TILE, SCORE_TILE)
        kd = kidx_ref[0, pl.ds(start, SCORE_TILE), :]
        zero = jnp.zeros_like(kd)
        kt2 = jnp.concatenate([jnp.where(first_head, kd, zero), jnp.where(first_head, zero, kd)], axis=0)
        s = _dot_nt(qis_ref[...], kt2)
        r = jnp.maximum(s, 0.0) * wb_ref[...]
        r = jnp.sum(r.reshape(H_IDX // 2, tq, 2 * SCORE_TILE), axis=0)
        score = r[:, :SCORE_TILE] + r[:, SCORE_TILE:] + 0.0
        bits = pltpu.bitcast(score, jnp.int32)
        key = jnp.where(bits < 0, bits ^ jnp.int32(0x7FFFFFFF), bits)
        kpos = start + lane
        visible = ((kpos // CHUNK) <= q_chunk) & (kpos < seq_len)
        key = jnp.where(visible, key, jnp.int32(INT32_MIN))
        if pad_rows:
            key = jnp.concatenate([key, jnp.full((pad_rows, SCORE_TILE), INT32_MIN, jnp.int32)], axis=0)
        for j in range(SCORE_TILE // KEY_TILE):
            key_ref[u * (SCORE_TILE // KEY_TILE) + j] = key[:, j * KEY_TILE:(j + 1) * KEY_TILE].T
        return carry

    lax.fori_loop(0, pl.cdiv(n_vis * KEY_TILE, SCORE_TILE), score_tile, 0)

    def count_keys(pred):
        per_step = SCORE_TILE // KEY_TILE

        def count_tiles(u, cnt):
            for j in range(per_step):
                t = u * per_step + j
                kpos = t * KEY_TILE + lax.broadcasted_iota(jnp.int32, (KEY_TILE, KEY_TILE), 0)
                cnt = cnt + jnp.where(pred(key_ref[t], kpos), 1.0, 0.0)
            return cnt

        cnt = lax.fori_loop(0, pl.cdiv(n_vis, per_step), count_tiles, jnp.zeros((KEY_TILE, KEY_TILE), F32))
        return jnp.sum(cnt, axis=0, keepdims=True)

    thr = jnp.full((1, KEY_TILE), INT32_MIN, jnp.int32)
    for bit in range(31, -1, -1):
        cand = thr + jnp.int32(-2 ** 31 if bit == 31 else 2 ** bit)
        thr = jnp.where(count_keys(lambda key, kpos, cand=cand: key >= cand) >= float(topk), cand, thr)
    thr = jnp.maximum(thr, jnp.int32(INT32_MIN + 1))

    n_above = count_keys(lambda key, kpos: key > thr)
    n_tied = count_keys(lambda key, kpos: key == thr)
    ties_wanted = float(topk) - n_above
    pos_bits = (key_ref.shape[0] * KEY_TILE - 1).bit_length()

    def find_last_tie():
        last = jnp.zeros((1, KEY_TILE), jnp.int32)
        for bit in range(pos_bits - 1, -1, -1):
            cand = last | jnp.int32(2 ** bit)
            before = count_keys(lambda key, kpos, cand=cand: (key == thr) & (kpos < cand))
            last = jnp.where(before < ties_wanted, cand, last)
        return last

    surplus = jnp.max(n_tied - ties_wanted) > 0.0
    last_tie = lax.cond(surplus, find_last_tie, lambda: jnp.full((1, KEY_TILE), 2 ** pos_bits - 1, jnp.int32))

    kvs = range(H_KV)
    for kv in kvs:
        for g in range(GROUP):
            c0 = (kv * GROUP + g) * DH_C
            qh = (qc_ref[:, c0:c0 + DH_C] * DH_C ** -0.5).astype(BF16)
            if pad_rows:
                qh = jnp.concatenate([qh, jnp.zeros((pad_rows, DH_C), BF16)], axis=0)
            qg_ref[kv, g * KEY_TILE:(g + 1) * KEY_TILE, :] = qh
    acc_ref[...] = jnp.zeros(acc_ref.shape, F32)
    m_ref[...] = jnp.full(m_ref.shape, MASKED, F32)
    l_ref[...] = jnp.zeros(l_ref.shape, F32)

    def attend_tile(t, carry, near):
        start = pl.multiple_of(t * KEY_TILE, KEY_TILE)
        key = key_ref[t]
        kpos = start + lax.broadcasted_iota(jnp.int32, (KEY_TILE, KEY_TILE), 0)
        sel = (key > thr) | ((key == thr) & (kpos <= last_tie))
        penalty = jnp.concatenate([jnp.where(sel, 0.0, MASKED)] * GROUP, axis=1)
        s = [_dot_nt(k_ref[0, pl.ds(start, KEY_TILE), kv * DH_C:(kv + 1) * DH_C], qg_ref[kv]) for kv in kvs]
        if near:
            sm = [s[kv] + bias_ref[qt - t, kv] + penalty for kv in kvs]
        else:
            sm = [s[kv] + penalty for kv in kvs]
        m_old = [m_ref[kv:kv + 1, :] for kv in kvs]
        m_new = [jnp.maximum(m_old[kv], jnp.max(sm[kv], axis=0, keepdims=True)) for kv in kvs]
        alpha = [jnp.exp(m_old[kv] - m_new[kv]) for kv in kvs]
        p = [jnp.exp(sm[kv] - m_new[kv]) for kv in kvs]
        pv = [lax.dot_general(v_ref[0, pl.ds(start, KEY_TILE), kv * DH_C:(kv + 1) * DH_C], p[kv].astype(BF16),
                              (((0,), (0,)), ((), ())), preferred_element_type=F32) for kv in kvs]
        for kv in kvs:
            l_ref[kv:kv + 1, :] = alpha[kv] * l_ref[kv:kv + 1, :] + jnp.sum(p[kv], axis=0, keepdims=True)
            acc_ref[kv] = alpha[kv] * acc_ref[kv] + pv[kv]
            m_ref[kv:kv + 1, :] = m_new[kv]
        return carry

    n_far = jnp.maximum(n_vis - N_BIAS_NEAR, 0)
    lax.fori_loop(0, n_far, partial(attend_tile, near=False), 0)
    lax.fori_loop(n_far, n_vis, partial(attend_tile, near=True), 0)
    for kv in kvs:
        out = (acc_ref[kv] / l_ref[kv:kv + 1, :]).T
        for g in range(GROUP):
            c0 = (kv * GROUP + g) * DH_C
            o_ref[:, c0:c0 + DH_C] = out[g * KEY_TILE:g * KEY_TILE + tq].astype(o_ref.dtype)


def dsa_bias_tiles(rel_bias):
    i = jnp.arange(KEY_TILE)
    d = jnp.arange(N_BIAS_NEAR + 1)
    rel = (i[None, :, None] - i[None, None, :]) - KEY_TILE * d[:, None, None]
    onehot = (t5_bucket(rel)[..., None] == jnp.arange(NUM_BUCKETS)).astype(F32)
    tiles = jnp.einsum('dkqn,nh->dkqh', onehot, rel_bias.astype(F32), precision=lax.Precision.HIGHEST)
    tiles = (tiles[:N_BIAS_NEAR] - tiles[N_BIAS_NEAR:]).reshape(N_BIAS_NEAR, KEY_TILE, KEY_TILE, H_KV, GROUP)
    return jnp.transpose(tiles, (0, 3, 1, 4, 2)).reshape(N_BIAS_NEAR, H_KV, KEY_TILE, GROUP * KEY_TILE)


def dsa_attention(proj, row0, b, t, k_all, v_all, kidx_all, bias_tiles, p_len, seq_len, topk):
    tq = KEY_TILE if t % KEY_TILE == 0 else t
    assert p_len % KEY_TILE == 0 and KEY_TILE % tq == 0 and (tq == KEY_TILE or t == tq) and row0 % tq == 0
    nb = t // tq
    lp = k_all.shape[1]
    assert lp % SCORE_TILE == 0 and lp >= p_len + nb * KEY_TILE
    assert 8 * (MAX_DISTANCE / 8) ** (7 / 8) < KEY_TILE * N_BIAS_NEAR - (KEY_TILE - 1)
    rb = row0 // tq
    kern = partial(_dsa_kernel, tq=tq, q0_tile=p_len // KEY_TILE, seq_len=seq_len, topk=topk)
    return pl.pallas_call(
        kern,
        grid=(b, nb),
        in_specs=[pl.BlockSpec((tq, W_C), lambda bi, j: (rb + bi * nb + j, COL["q_c"][0] // W_C)),
                  pl.BlockSpec((tq, H_IDX * D_IDX), lambda bi, j: (rb + bi * nb + j, COL["q_i"][0] // (H_IDX * D_IDX))),
                  pl.BlockSpec((tq, LANE), lambda bi, j: (rb + bi * nb + j, COL["k_i"][0] // LANE)),
                  pl.BlockSpec((1, lp, H_KV * DH_C), lambda bi, j: (bi, 0, 0)),
                  pl.BlockSpec((1, lp, H_KV * DH_C), lambda bi, j: (bi, 0, 0)),
                  pl.BlockSpec((1, lp, 2 * D_IDX), lambda bi, j: (bi, 0, 0)),
                  pl.BlockSpec((N_BIAS_NEAR, H_KV, KEY_TILE, GROUP * KEY_TILE), lambda bi, j: (0, 0, 0, 0))],
        out_specs=pl.BlockSpec((tq, W_C), lambda bi, j: (bi * nb + j, 0)),
        out_shape=jax.ShapeDtypeStruct((b * t, W_C), BF16),
        scratch_shapes=[pltpu.VMEM((lp // KEY_TILE, KEY_TILE, KEY_TILE), jnp.int32),
                        pltpu.VMEM((H_IDX // 2 * tq, 2 * D_IDX), BF16),
                        pltpu.VMEM((H_IDX // 2 * tq, 2 * SCORE_TILE), F32),
                        pltpu.VMEM((H_KV, GROUP * KEY_TILE, DH_C), BF16),
                        pltpu.VMEM((H_KV, DH_C, GROUP * KEY_TILE), F32),
                        pltpu.VMEM((SUBLANE, GROUP * KEY_TILE), F32),
                        pltpu.VMEM((SUBLANE, GROUP * KEY_TILE), F32)],
        compiler_params=_cparams(("parallel", "arbitrary")),
        name="dsa_attention",
    )(proj, proj, proj, k_all, v_all, kidx_all, bias_tiles)


def branches(proj_all, new_rows, row0, b, t, past_k, past_v, past_kidx, s_ret, s_delta, s_conv,
             ret_gn, conv_w, a_log, dt_bias, d_norm, bias_tiles):
    p_len = past_k.shape[1]
    seq_len = p_len + t
    topk = min(TOPK_MAX, seq_len // 4)
    lp = -(-seq_len // SCORE_TILE) * SCORE_TILE
    k_c, v_c, k_i, k_bf, v_bf, ki_bf = (x[row0:row0 + b * t].reshape(b, t, x.shape[-1]) for x in new_rows)

    o, w = COL["qkv_b"]
    qkv_tail = jnp.stack([proj_all[row0 + t - (CONV_W - 1) + j:row0 + b * t:t, o:o + w]
                          for j in range(CONV_W - 1)], axis=1)
    conv_new = jnp.concatenate([s_conv, qkv_tail], axis=1)[:, -(CONV_W - 1):]

    o_a, ret_new = retention(proj_all, row0, b, t, p_len, s_ret, ret_gn)
    o_b, delta_new = gated_delta(proj_all, row0, b, t, s_conv, s_delta, conv_w, a_log, dt_bias, d_norm)

    def all_keys(past, new, reps=1):
        if p_len == 0 and lp == seq_len:
            return new
        w = past.shape[-1] * reps
        past = jnp.tile(past.reshape(b, p_len, -1).astype(BF16), (1, 1, reps))
        return jnp.concatenate([past, new, jnp.zeros((b, lp - seq_len, w), BF16)], axis=1)

    o_c = dsa_attention(proj_all, row0, b, t, all_keys(past_k.reshape(b, p_len, KV_W), k_bf),
                        all_keys(past_v.reshape(b, p_len, KV_W), v_bf), all_keys(past_kidx, ki_bf, 2),
                        bias_tiles, p_len, seq_len, topk)
    caches = (k_c.reshape(b, t, H_KV, DH_C), v_c.reshape(b, t, H_KV, DH_C), k_i, ret_new, delta_new, conv_new)
    return jnp.stack([o_a, o_b, o_c]), caches


def _prep_w_in(w):
    parts = [w[:, MODEL_COL[n][0]:MODEL_COL[n][0] + MODEL_COL[n][1]] for n in _KERNEL_ORDER]
    parts.append(jnp.zeros((w.shape[0], D_IN_PAD - D_IN), w.dtype))
    return jnp.concatenate(parts, axis=1).astype(BF16)


def kernel(x_prompt, x_sample, cache_k, cache_v, cache_kidx, state_ret, state_delta, state_conv, norm_mix, w_in, ret_gn, conv_w, delta_a_log, delta_dt_bias, delta_norm, rel_bias, w_branch_a, w_branch_b, w_branch_c, w_out, norm_ffn, w_ffn_gate, w_ffn_up, w_ffn_down, norm_final):
    dt = x_prompt.dtype
    bp, tp, _ = x_prompt.shape
    bs, ts, _ = x_sample.shape
    depth = w_in.shape[0]
    n_p = bp * tp
    zk = jnp.zeros((bp, 0, H_KV, DH_C), dt)
    zkidx = jnp.zeros((bp, 0, D_IDX), dt)
    zret = jnp.zeros((bp, H_A, DK_A, DV_A), dt)
    zdelta = jnp.zeros((bp, H_B, DK_B, DV_B), dt)
    zconv = jnp.zeros((bp, CONV_W - 1, CONV_CH), dt)

    h = jnp.concatenate([x_prompt.reshape(n_p, D_MODEL), x_sample.reshape(bs * ts, D_MODEL)], axis=0)
    new_p, new_s = [], []
    bias_tiles = dsa_bias_tiles(rel_bias)
    for l in range(depth):
        u = rmsnorm(h, norm_mix[l], BF16)
        proj, *new_rows = matmul_in(u, _prep_w_in(w_in[l]))
        bw = (ret_gn[l], conv_w[l], delta_a_log[l], delta_dt_bias[l], delta_norm[l], bias_tiles)
        o_p, sp = branches(proj, new_rows, 0, bp, tp, zk, zk, zkidx, zret, zdelta, zconv, *bw)
        o_s, ss = branches(proj, new_rows, n_p, bs, ts, cache_k[l], cache_v[l], cache_kidx[l],
                           state_ret[l], state_delta[l], state_conv[l], *bw)
        new_p.append(sp)
        new_s.append(ss)
        o_abc = jnp.concatenate([o_p, o_s], axis=1)
        w_br = jnp.stack([w_branch_a[l], w_branch_b[l], w_branch_c[l]]).astype(BF16)
        merged = merge_branches(o_abc, w_br, proj)
        h = matmul_residual(merged, w_out[l].astype(BF16), h)
        u2 = rmsnorm(h, norm_ffn[l], BF16)
        act = ffn_gate_up(u2, w_ffn_gate[l].astype(BF16), w_ffn_up[l].astype(BF16))
        h = ffn_down(act, w_ffn_down[l].astype(BF16), h)

    y_prompt = rmsnorm(h, norm_final, F32, 0, n_p).reshape(bp, tp, D_MODEL)
    y_sample = rmsnorm(h, norm_final, F32, n_p, bs * ts).reshape(bs, ts, D_MODEL)
    k_p, v_p, kidx_p, ret_p, delta_p, conv_p = [jnp.stack([s[i] for s in new_p]) for i in range(6)]
    k_s, v_s, kidx_s, ret_s, delta_s, conv_s = [jnp.stack([s[i] for s in new_s]) for i in range(6)]
    return (y_prompt, y_sample, k_p, v_p, kidx_p, ret_p, delta_p, conv_p,
            k_s, v_s, kidx_s, ret_s, delta_s, conv_s)
```

```python
import math
from functools import partial

import jax
import jax.numpy as jnp
from jax import lax
from jax.experimental import pallas as pl
from jax.experimental.pallas import tpu as pltpu

F32 = jnp.float32
BF16 = jnp.bfloat16

D_MODEL = 4096
CHUNK = 64
EPS = 1e-6
H_A, DK_A, DV_A = 8, 128, 256
W_A = H_A * DV_A
ROPE_BASE = 10000.0
H_B, DK_B, DV_B = 16, 128, 128
W_B = H_B * DV_B
CONV_W = 4
CONV_CH = 2 * H_B * DK_B + H_B * DV_B
H_C, H_KV, DH_C = 16, 4, 128
W_C = H_C * DH_C
KV_W = H_KV * DH_C
H_IDX, D_IDX = 32, 64
TOPK_MAX = 256
NUM_BUCKETS = 32
MAX_DISTANCE = 1024

LANE = 128
SUBLANE = 8
VMEM_LIMIT_V7X = 56 * 1024 * 1024
TM = 1024
TN = 1024
TIME_BLOCK = 512

_MODEL_ORDER = [("q_a", H_A * DK_A), ("k_a", H_A * DK_A), ("v_a", W_A), ("g_a", W_A),
                ("qkv_b", CONV_CH), ("z_b", W_B), ("a_b", H_B), ("b_b", H_B),
                ("q_c", W_C), ("k_c", KV_W), ("v_c", KV_W), ("q_i", H_IDX * D_IDX), ("k_i", D_IDX), ("w_i", H_IDX),
                ("gate_a", D_MODEL), ("gate_b", D_MODEL), ("gate_c", D_MODEL)]
_KERNEL_ORDER = ["q_a", "k_a", "v_a", "g_a", "qkv_b", "z_b", "q_c", "q_i", "k_c", "v_c",
                 "gate_a", "gate_b", "gate_c", "k_i", "w_i", "a_b", "b_b"]


def _offsets(order, widths):
    table, off = {}, 0
    for name in order:
        table[name] = (off, widths[name])
        off += widths[name]
    return table, off


_WIDTH = dict(_MODEL_ORDER)
MODEL_COL, D_IN = _offsets([n for n, _ in _MODEL_ORDER], _WIDTH)
COL, _ = _offsets(_KERNEL_ORDER, _WIDTH)
D_IN_PAD = -(-D_IN // TN) * TN


def _cparams(sem):
    return pltpu.CompilerParams(dimension_semantics=sem, vmem_limit_bytes=VMEM_LIMIT_V7X)


def _rmsnorm_kernel(x_ref, g_ref, o_ref):
    x = x_ref[...]
    y = x * lax.rsqrt(jnp.mean(x * x, axis=-1, keepdims=True) + EPS)
    o_ref[...] = (y * g_ref[...]).astype(o_ref.dtype)


def rmsnorm(x, g, out_dtype, row0=0, rows=None, tr=512):
    d = x.shape[1]
    m = x.shape[0] if rows is None else rows
    assert row0 % tr == 0 and m % tr == 0
    rb = row0 // tr
    return pl.pallas_call(
        _rmsnorm_kernel,
        grid=(m // tr,),
        in_specs=[pl.BlockSpec((tr, d), lambda i: (rb + i, 0)), pl.BlockSpec((1, d), lambda i: (0, 0))],
        out_specs=pl.BlockSpec((tr, d), lambda i: (i, 0)),
        out_shape=jax.ShapeDtypeStruct((m, d), out_dtype),
        compiler_params=_cparams(("parallel",)),
        name="rmsnorm",
    )(x, g.reshape(1, d))


def _mm_in_kernel(a_ref, b_ref, o_ref, k_ref, v_ref, ki_ref, kb_ref, vb_ref, kib_ref, *, kv_block, small_block):
    j = pl.program_id(1)
    o_ref[...] = jnp.dot(a_ref[...], b_ref[...], preferred_element_type=F32)

    @pl.when(j == kv_block)
    def _():
        k, v = o_ref[:, :KV_W], o_ref[:, KV_W:2 * KV_W]
        k_ref[...] = k
        v_ref[...] = v
        kb_ref[...] = k.astype(BF16)
        vb_ref[...] = v.astype(BF16)

    @pl.when(j == small_block)
    def _():
        ki = o_ref[:, :D_IDX]
        ki_ref[...] = ki
        kib_ref[...] = jnp.concatenate([ki, ki], axis=1).astype(BF16)


def matmul_in(a, b, tn=TN):
    m, k = a.shape
    n = b.shape[1]
    assert COL["k_c"][0] % tn == 0 and COL["v_c"][0] == COL["k_c"][0] + KV_W and COL["k_i"][0] % tn == 0
    kern = partial(_mm_in_kernel, kv_block=COL["k_c"][0] // tn, small_block=COL["k_i"][0] // tn)

    def row_spec(w):
        return pl.BlockSpec((TM, w), lambda i, j: (i, 0))

    return pl.pallas_call(
        kern,
        grid=(m // TM, n // tn),
        in_specs=[pl.BlockSpec((TM, k), lambda i, j: (i, 0), pipeline_mode=pl.Buffered(1)),
                  pl.BlockSpec((k, tn), lambda i, j: (0, j))],
        out_specs=[pl.BlockSpec((TM, tn), lambda i, j: (i, j)), row_spec(KV_W), row_spec(KV_W), row_spec(D_IDX),
                   row_spec(KV_W), row_spec(KV_W), row_spec(2 * D_IDX)],
        out_shape=[jax.ShapeDtypeStruct((m, n), F32), jax.ShapeDtypeStruct((m, KV_W), F32),
                   jax.ShapeDtypeStruct((m, KV_W), F32), jax.ShapeDtypeStruct((m, D_IDX), F32),
                   jax.ShapeDtypeStruct((m, KV_W), BF16), jax.ShapeDtypeStruct((m, KV_W), BF16),
                   jax.ShapeDtypeStruct((m, 2 * D_IDX), BF16)],
        compiler_params=_cparams(("parallel", "arbitrary")),
        name="matmul_in",
    )(a, b)


N_BRANCH = 3


def _merge_kernel(*refs):
    o_refs, w_refs, g_refs = (refs[i * N_BRANCH:(i + 1) * N_BRANCH] for i in range(3))
    out_ref = refs[3 * N_BRANCH]
    acc = None
    for o_ref, w_ref, g_ref in zip(o_refs, w_refs, g_refs):
        y = jnp.dot(o_ref[0], w_ref[0], preferred_element_type=F32) * jax.nn.sigmoid(g_ref[...])
        acc = y if acc is None else acc + y
    out_ref[...] = acc.astype(out_ref.dtype)


def merge_branches(o_abc, w_br, proj, tn=TN // 2):
    _, m, k = o_abc.shape
    n = w_br.shape[2]
    gate0 = COL["gate_a"][0] // tn
    nj = n // tn

    def per_branch(make):
        return [make(r) for r in range(N_BRANCH)]

    return pl.pallas_call(
        _merge_kernel,
        grid=(m // TM, nj),
        in_specs=per_branch(lambda r: pl.BlockSpec((1, TM, k), lambda i, j: (r, i, 0), pipeline_mode=pl.Buffered(1)))
        + per_branch(lambda r: pl.BlockSpec((1, k, tn), lambda i, j: (r, 0, j)))
        + per_branch(lambda r: pl.BlockSpec((TM, tn), lambda i, j: (i, gate0 + r * nj + j))),
        out_specs=pl.BlockSpec((TM, tn), lambda i, j: (i, j)),
        out_shape=jax.ShapeDtypeStruct((m, n), BF16),
        compiler_params=_cparams(("parallel", "parallel")),
        name="merge_branches",
    )(*([o_abc] * N_BRANCH), *([w_br] * N_BRANCH), *([proj] * N_BRANCH))


def _mm_res_kernel(a_ref, b_ref, h_ref, o_ref):
    o_ref[...] = h_ref[...] + jnp.dot(a_ref[...], b_ref[...], preferred_element_type=F32)


def matmul_residual(a, b, h, tn=TN):
    m, k = a.shape
    n = b.shape[1]
    return pl.pallas_call(
        _mm_res_kernel,
        grid=(m // TM, n // tn),
        in_specs=[pl.BlockSpec((TM, k), lambda i, j: (i, 0)), pl.BlockSpec((k, tn), lambda i, j: (0, j)),
                  pl.BlockSpec((TM, tn), lambda i, j: (i, j))],
        out_specs=pl.BlockSpec((TM, tn), lambda i, j: (i, j)),
        out_shape=jax.ShapeDtypeStruct((m, n), F32),
        compiler_params=_cparams(("parallel", "parallel")),
        name="matmul_out",
    )(a, b, h)


def _gateup_kernel(u_ref, wg_ref, wu_ref, o_ref):
    u = u_ref[...]
    g = jnp.dot(u, wg_ref[...], preferred_element_type=F32)
    up = jnp.dot(u, wu_ref[...], preferred_element_type=F32)
    o_ref[...] = (jax.nn.silu(g) * up).astype(o_ref.dtype)


def ffn_gate_up(u, wg, wu, tn=512):
    m, k = u.shape
    n = wg.shape[1]
    return pl.pallas_call(
        _gateup_kernel,
        grid=(m // TM, pl.cdiv(n, tn)),
        in_specs=[pl.BlockSpec((TM, k), lambda i, j: (i, 0)),
                  pl.BlockSpec((k, tn), lambda i, j: (0, j)),
                  pl.BlockSpec((k, tn), lambda i, j: (0, j))],
        out_specs=pl.BlockSpec((TM, tn), lambda i, j: (i, j)),
        out_shape=jax.ShapeDtypeStruct((m, n), BF16),
        compiler_params=_cparams(("parallel", "parallel")),
        name="ffn_gate_up",
    )(u, wg, wu)


def ffn_down(a, b, h, tm=768, tn=512):
    m, k = a.shape
    n = b.shape[1]
    return pl.pallas_call(
        _mm_res_kernel,
        grid=(m // tm, n // tn),
        in_specs=[pl.BlockSpec((tm, k), lambda i, j: (i, 0), pipeline_mode=pl.Buffered(1)),
                  pl.BlockSpec((k, tn), lambda i, j: (0, j)),
                  pl.BlockSpec((tm, tn), lambda i, j: (i, j))],
        out_specs=pl.BlockSpec((tm, tn), lambda i, j: (i, j)),
        out_shape=jax.ShapeDtypeStruct((m, n), F32),
        compiler_params=_cparams(("parallel", "parallel")),
        name="ffn_down",
    )(a, b, h)


RET_HEADS = 8


def _retention_kernel(q_ref, k_ref, v_ref, g_ref, cos_ref, sin_ref, idec_ref, qdec_ref, kdec_ref, cdec_ref, gn_ref,
                      s0_ref, o_ref, sfin_ref, s_ref, *, tb, c):
    tblk = pl.program_id(2)
    heads = range(RET_HEADS)

    @pl.when(tblk == 0)
    def _():
        s_ref[...] = s0_ref[0]

    def rotate(x, cos, sin):
        return x * cos + pltpu.roll(x, DK_A // 2, 1) * sin

    def chunk_step(ci, carry):
        c0 = pl.multiple_of(ci * c, c)
        cos = cos_ref[pl.ds(c0, c), :]
        sin = sin_ref[pl.ds(c0, c), :]
        qr = [rotate(q_ref[pl.ds(c0, c), h * DK_A:(h + 1) * DK_A], cos, sin) for h in heads]
        kr = [rotate(k_ref[pl.ds(c0, c), h * DK_A:(h + 1) * DK_A], cos, sin) * DK_A ** -0.5 for h in heads]
        v = [v_ref[pl.ds(c0, c), h * DV_A:(h + 1) * DV_A].astype(BF16) for h in heads]
        s = [s_ref[h] for h in heads]
        att = [_dot_nt(qr[h].astype(BF16), kr[h].astype(BF16)) * idec_ref[h] for h in heads]
        o = [_bdot(att[h], v[h]) + _bdot(qr[h] * qdec_ref[h], s[h]) for h in heads]
        for h in heads:
            s_ref[h] = s[h] * cdec_ref[h, 0:1, :] + lax.dot_general(
                (kr[h] * kdec_ref[h]).astype(BF16), v[h], (((0,), (0,)), ((), ())), preferred_element_type=F32)
        for h in heads:
            mu = jnp.mean(o[h], axis=-1, keepdims=True)
            d = o[h] - mu
            var = jnp.mean(d * d, axis=-1, keepdims=True)
            cols = slice(h * DV_A, (h + 1) * DV_A)
            g = g_ref[pl.ds(c0, c), cols]
            o_ref[pl.ds(c0, c), cols] = (
                g * jax.nn.sigmoid(g) * (d * lax.rsqrt(var + EPS) * gn_ref[:, cols])).astype(o_ref.dtype)
        return carry

    lax.fori_loop(0, tb // c, chunk_step, 0)

    @pl.when(tblk == pl.num_programs(2) - 1)
    def _():
        sfin_ref[0] = s_ref[...]


def retention(proj, row0, b, t, p_len, s_ret, ret_gn):
    c = 2 * CHUNK if t % (2 * CHUNK) == 0 else t
    tb = TIME_BLOCK if t % TIME_BLOCK == 0 else t
    assert tb % c == 0 and row0 % tb == 0
    nt = t // tb
    rb = row0 // tb
    half = DK_A // 2
    inv = ROPE_BASE ** (-jnp.arange(half, dtype=F32) / half)
    ang = (p_len + jnp.arange(t)).astype(F32)[:, None] * inv[None, :]
    cos2 = jnp.concatenate([jnp.cos(ang), jnp.cos(ang)], axis=1)
    sin2 = jnp.concatenate([-jnp.sin(ang), jnp.sin(ang)], axis=1)
    log_gamma = jnp.log(1.0 - 2.0 ** (-5.0 - jnp.arange(H_A, dtype=F32)))
    i = jnp.arange(c, dtype=F32)
    rel = i[:, None] - i[None, :]
    idec = jnp.exp(jnp.where(rel[None] >= 0, rel[None] * log_gamma[:, None, None], -jnp.inf))
    qdec = jnp.broadcast_to(jnp.exp((i[None, :] + 1.0) * log_gamma[:, None])[:, :, None], (H_A, c, DK_A))
    kdec = jnp.broadcast_to(jnp.exp((c - 1.0 - i[None, :]) * log_gamma[:, None])[:, :, None], (H_A, c, DK_A))
    cdec = jnp.broadcast_to(jnp.exp(c * log_gamma)[:, None, None], (H_A, SUBLANE, DV_A))
    wk, wv = RET_HEADS * DK_A, RET_HEADS * DV_A
    qb0 = COL["q_a"][0] // wk
    kb0 = COL["k_a"][0] // wk
    vb0 = COL["v_a"][0] // wv
    gb0 = COL["g_a"][0] // wv
    sspec = pl.BlockSpec((1, RET_HEADS, DK_A, DV_A), lambda bi, h, k: (bi, h, 0, 0))
    return pl.pallas_call(
        partial(_retention_kernel, tb=tb, c=c),
        grid=(b, H_A // RET_HEADS, nt),
        in_specs=[pl.BlockSpec((tb, wk), lambda bi, h, k: (rb + bi * nt + k, qb0 + h)),
                  pl.BlockSpec((tb, wk), lambda bi, h, k: (rb + bi * nt + k, kb0 + h)),
                  pl.BlockSpec((tb, wv), lambda bi, h, k: (rb + bi * nt + k, vb0 + h)),
                  pl.BlockSpec((tb, wv), lambda bi, h, k: (rb + bi * nt + k, gb0 + h)),
                  pl.BlockSpec((tb, DK_A), lambda bi, h, k: (k, 0)),
                  pl.BlockSpec((tb, DK_A), lambda bi, h, k: (k, 0)),
                  pl.BlockSpec((RET_HEADS, c, c), lambda bi, h, k: (h, 0, 0)),
                  pl.BlockSpec((RET_HEADS, c, DK_A), lambda bi, h, k: (h, 0, 0)),
                  pl.BlockSpec((RET_HEADS, c, DK_A), lambda bi, h, k: (h, 0, 0)),
                  pl.BlockSpec((RET_HEADS, SUBLANE, DV_A), lambda bi, h, k: (h, 0, 0)),
                  pl.BlockSpec((1, wv), lambda bi, h, k: (0, h)),
                  sspec],
        out_specs=[pl.BlockSpec((tb, wv), lambda bi, h, k: (bi * nt + k, h)), sspec],
        out_shape=[jax.ShapeDtypeStruct((b * t, W_A), BF16),
                   jax.ShapeDtypeStruct((b, H_A, DK_A, DV_A), F32)],
        scratch_shapes=[pltpu.VMEM((RET_HEADS, DK_A, DV_A), F32)],
        compiler_params=_cparams(("parallel", "parallel", "arbitrary")),
        name="retention",
    )(proj, proj, proj, proj, cos2, sin2, idec, qdec, kdec, cdec, ret_gn.reshape(1, W_A), s_ret)


DELTA_HEADS = 8
DELTA_W = DELTA_HEADS * DK_B
HIST_ROWS = SUBLANE
INV_BASE = 16


def _bdot(a, b):
    return jnp.dot(a.astype(BF16), b.astype(BF16), preferred_element_type=F32)


def _hdot(a, b):
    return jnp.dot(a, b, preferred_element_type=F32, precision=lax.Precision.HIGHEST)


def _split_bf16(x):
    hi = x.astype(BF16)
    return hi, (x - hi.astype(F32)).astype(BF16)


def _dot3(a, b):
    a_hi, a_lo = _split_bf16(a)
    b_hi, b_lo = _split_bf16(b)
    dot = partial(jnp.dot, preferred_element_type=F32)
    return dot(a_hi, b_hi) + (dot(a_hi, b_lo) + dot(a_lo, b_hi))


def _unit_lower_inverse(mats, c):
    ii = lax.broadcasted_iota(jnp.int32, (c, c), 0)
    jj = lax.broadcasted_iota(jnp.int32, (c, c), 1)
    eye = jnp.where(ii == jj, 1.0, 0.0).astype(F32)
    ps = [jnp.where((ii // INV_BASE) == (jj // INV_BASE), a, 0.0) for a in mats]
    ts = [eye - p for p in ps]
    for _ in range(int(math.log2(INV_BASE)) - 1):
        ps = [_dot3(p, p) for p in ps]
        ts = [_dot3(t, eye + p) for t, p in zip(ts, ps)]
    size = INV_BASE
    while size < c:
        half_blocks = ((ii // (2 * size)) == (jj // (2 * size))) & ((ii // size) != (jj // size))
        tl = [_dot3(t, jnp.where(half_blocks, a, 0.0)) for t, a in zip(ts, mats)]
        ts = [t - _dot3(x, t) for t, x in zip(ts, tl)]
        size *= 2
    return ts


def _delta_kernel(xq_ref, xk_ref, xv_ref, z_ref, small_ref, cwq_ref, cwk_ref, cwv_ref, hq_ref, hk_ref, hv_ref,
                  par_ref, dn_ref, s0_ref, o_ref, sfin_ref,
                  bq_ref, bk_ref, bv_ref, qn_ref, kn_ref, vv_ref, g_ref, beta_ref, s_ref, w_ref, qk_ref, el_ref,
                  *, tb, c):
    hg = pl.program_id(1)
    tblk = pl.program_id(2)

    @pl.when(tblk == 0)
    def _():
        bq_ref[0:HIST_ROWS, :] = hq_ref[0]
        bk_ref[0:HIST_ROWS, :] = hk_ref[0]
        bv_ref[0:HIST_ROWS, :] = hv_ref[0]
        s_ref[...] = s0_ref[0]

    def conv_silu(x_ref, buf_ref, cw_ref):
        buf_ref[HIST_ROWS:HIST_ROWS + tb, :] = x_ref[...]
        y = jnp.zeros((tb, DELTA_W), F32)
        for j in range(CONV_W):
            lo = HIST_ROWS - (CONV_W - 1) + j
            y = y + buf_ref[lo:lo + tb, :] * cw_ref[j:j + 1, :]
        buf_ref[0:HIST_ROWS, :] = buf_ref[tb:tb + HIST_ROWS, :]
        return y * jax.nn.sigmoid(y)

    def l2n(x):
        parts = []
        for h in range(DELTA_HEADS):
            xh = x[:, h * DK_B:(h + 1) * DK_B]
            parts.append(xh * lax.rsqrt(jnp.sum(xh * xh, axis=-1, keepdims=True) + EPS))
        return jnp.concatenate(parts, axis=1)

    qn_ref[...] = l2n(conv_silu(xq_ref, bq_ref, cwq_ref)) * DK_B ** -0.5
    kn_ref[...] = l2n(conv_silu(xk_ref, bk_ref, cwk_ref))
    vv_ref[...] = conv_silu(xv_ref, bv_ref, cwv_ref)
    sm = small_ref[...]
    x = sm + par_ref[1:2, :]
    softplus = jnp.maximum(x, 0.0) + jnp.log1p(jnp.exp(-jnp.abs(x)))
    g_ref[...] = -jnp.exp(par_ref[0:1, :]) * softplus
    beta_ref[...] = jax.nn.sigmoid(sm)

    ii = lax.broadcasted_iota(jnp.int32, (c, c), 0)
    jj = lax.broadcasted_iota(jnp.int32, (c, c), 1)
    incl = ii >= jj
    strict = ii > jj
    diag = ii == jj
    tril_ones = jnp.where(incl, 1.0, 0.0).astype(F32)
    a_lane = COL["a_b"][0] - COL["k_i"][0]
    b_lane = COL["b_b"][0] - COL["k_i"][0]

    n_chunks = tb // c
    p_unroll = 2 if n_chunks % 2 == 0 else 1

    def prepare_chunks(step, carry):
        lane = lax.broadcasted_iota(jnp.int32, (c, LANE), 1)
        items = []
        for j in range(p_unroll):
            ci = step * p_unroll + j
            c0 = pl.multiple_of(ci * c, c)
            gcum_all = _hdot(tril_ones, g_ref[pl.ds(c0, c), :])
            beta_all = beta_ref[pl.ds(c0, c), :]
            for h in range(DELTA_HEADS):
                items.append((ci, c0, h, gcum_all, beta_all))
        cols = [slice(h * DK_B, (h + 1) * DK_B) for _, _, h, _, _ in items]
        gc = [jnp.sum(jnp.where(lane == a_lane + hg * DELTA_HEADS + h, ga, 0.0), axis=-1, keepdims=True)
              for _, _, h, ga, _ in items]
        bc = [jnp.sum(jnp.where(lane == b_lane + hg * DELTA_HEADS + h, ba, 0.0), axis=-1, keepdims=True)
              for _, _, h, _, ba in items]
        gr = [jnp.sum(jnp.where(diag, jnp.broadcast_to(g, (c, c)), 0.0), axis=0, keepdims=True) for g in gc]
        g_last = [jnp.sum(jnp.where(ii[:, 0:1] == c - 1, g, 0.0), axis=0, keepdims=True) for g in gc]
        gam = [jnp.exp(jnp.where(incl, g - r, -jnp.inf)) for g, r in zip(gc, gr)]
        eg = [jnp.exp(g) for g in gc]
        qh = [qn_ref[pl.ds(it[1], c), cs] for it, cs in zip(items, cols)]
        kh = [kn_ref[pl.ds(it[1], c), cs] for it, cs in zip(items, cols)]
        vh = [vv_ref[pl.ds(it[1], c), cs] for it, cs in zip(items, cols)]
        kb = [k * b for k, b in zip(kh, bc)]
        a = [jnp.where(strict, _dot_nt(x.astype(BF16), k.astype(BF16)) * gm, 0.0) for x, k, gm in zip(kb, kh, gam)]
        qk = [_dot_nt(q.astype(BF16), k.astype(BF16)) * gm for q, k, gm in zip(qh, kh, gam)]
        tinv = _unit_lower_inverse(a, c)
        u = [_bdot(t, v * b) for t, v, b in zip(tinv, vh, bc)]
        w = [_bdot(t, x * e) for t, x, e in zip(tinv, kb, eg)]
        for n, (ci, c0, h, _, _) in enumerate(items):
            rows = pl.ds(c0, c)
            vv_ref[rows, cols[n]] = u[n]
            w_ref[rows, cols[n]] = w[n]
            qn_ref[rows, cols[n]] = qh[n] * eg[n]
            kn_ref[rows, cols[n]] = kh[n] * jnp.exp(g_last[n] - gc[n])
            qk_ref[ci * DELTA_HEADS + h] = qk[n]
            el_ref[ci * DELTA_HEADS + h] = jnp.broadcast_to(jnp.exp(g_last[n]), (SUBLANE, LANE))
        return carry

    lax.fori_loop(0, n_chunks // p_unroll, prepare_chunks, 0)

    def state_step(ci, carry):
        rows = pl.ds(pl.multiple_of(ci * c, c), c)
        heads = range(DELTA_HEADS)
        cols = [slice(h * DK_B, (h + 1) * DK_B) for h in heads]
        s = [s_ref[h] for h in heads]
        v_new = [vv_ref[rows, cols[h]] - _bdot(w_ref[rows, cols[h]], s[h]) for h in heads]
        o = [_bdot(qn_ref[rows, cols[h]], s[h]) + _bdot(qk_ref[ci * DELTA_HEADS + h], v_new[h]) for h in heads]
        for h in heads:
            s_ref[h] = s[h] * el_ref[ci * DELTA_HEADS + h][0:1, :] + lax.dot_general(
                kn_ref[rows, cols[h]].astype(BF16), v_new[h].astype(BF16), (((0,), (0,)), ((), ())),
                preferred_element_type=F32)
        for h in heads:
            on = o[h] * lax.rsqrt(jnp.mean(o[h] * o[h], axis=-1, keepdims=True) + EPS) * dn_ref[...]
            zh = z_ref[rows, cols[h]]
            o_ref[rows, cols[h]] = (on * (zh * jax.nn.sigmoid(zh))).astype(o_ref.dtype)
        return carry

    lax.fori_loop(0, n_chunks, state_step, 0)

    @pl.when(tblk == pl.num_programs(2) - 1)
    def _():
        sfin_ref[0] = s_ref[...]


def gated_delta(proj, row0, b, t, s_conv, s_delta, conv_w, a_log, dt_bias, d_norm):
    c = CHUNK if t % CHUNK == 0 else t
    tb = TIME_BLOCK if t % TIME_BLOCK == 0 else t
    assert tb % c == 0 and c % INV_BASE == 0 and tb >= HIST_ROWS and row0 % tb == 0
    nt = t // tb
    rb = row0 // tb
    ngrp = H_B // DELTA_HEADS
    qcol = COL["qkv_b"][0] // DELTA_W
    hist = jnp.concatenate([jnp.zeros((b, HIST_ROWS - (CONV_W - 1), CONV_CH), F32), s_conv], axis=1)
    par = jnp.zeros((SUBLANE, LANE), F32)
    a_lane = COL["a_b"][0] - COL["k_i"][0]
    par = par.at[0, a_lane:a_lane + H_B].set(a_log).at[1, a_lane:a_lane + H_B].set(dt_bias)

    def xspec(part):
        return pl.BlockSpec((tb, DELTA_W), lambda bi, g, k: (rb + bi * nt + k, qcol + part * ngrp + g))

    def wspec(part):
        return pl.BlockSpec((CONV_W, DELTA_W), lambda bi, g, k: (0, part * ngrp + g))

    def hspec(part):
        return pl.BlockSpec((1, HIST_ROWS, DELTA_W), lambda bi, g, k: (bi, 0, part * ngrp + g))

    sspec = pl.BlockSpec((1, DELTA_HEADS, DK_B, DV_B), lambda bi, g, k: (bi, g, 0, 0))
    return pl.pallas_call(
        partial(_delta_kernel, tb=tb, c=c),
        grid=(b, ngrp, nt),
        in_specs=[xspec(0), xspec(1), xspec(2),
                  pl.BlockSpec((tb, DELTA_W), lambda bi, g, k: (rb + bi * nt + k, COL["z_b"][0] // DELTA_W + g)),
                  pl.BlockSpec((tb, LANE), lambda bi, g, k: (rb + bi * nt + k, COL["k_i"][0] // LANE)),
                  wspec(0), wspec(1), wspec(2), hspec(0), hspec(1), hspec(2),
                  pl.BlockSpec((SUBLANE, LANE), lambda bi, g, k: (0, 0)),
                  pl.BlockSpec((1, DV_B), lambda bi, g, k: (0, 0)),
                  sspec],
        out_specs=[pl.BlockSpec((tb, DELTA_W), lambda bi, g, k: (bi * nt + k, g)), sspec],
        out_shape=[jax.ShapeDtypeStruct((b * t, W_B), BF16),
                   jax.ShapeDtypeStruct((b, H_B, DK_B, DV_B), F32)],
        scratch_shapes=[pltpu.VMEM((tb + HIST_ROWS, DELTA_W), F32)] * 3
        + [pltpu.VMEM((tb, DELTA_W), F32)] * 3
        + [pltpu.VMEM((tb, LANE), F32)] * 2
        + [pltpu.VMEM((DELTA_HEADS, DK_B, DV_B), F32),
           pltpu.VMEM((tb, DELTA_W), F32),
           pltpu.VMEM((tb // c * DELTA_HEADS, c, c), F32),
           pltpu.VMEM((tb // c * DELTA_HEADS, SUBLANE, LANE), F32)],
        compiler_params=_cparams(("parallel", "parallel", "arbitrary")),
        name="gated_delta",
    )(proj, proj, proj, proj, proj, conv_w, conv_w, conv_w, hist, hist, hist, par, d_norm.reshape(1, DV_B), s_delta)


def t5_bucket(rel):
    half = NUM_BUCKETS // 2
    exact = half // 2
    n = jnp.abs(rel)
    large = exact + (jnp.log(jnp.maximum(n, 1).astype(jnp.float32) / exact)
                     / math.log(MAX_DISTANCE / exact) * (half - exact)).astype(jnp.int32)
    large = jnp.minimum(large, half - 1)
    return jnp.where(rel > 0, half, 0) + jnp.where(n < exact, n, large)


KEY_TILE = LANE
SCORE_TILE = 2 * KEY_TILE
N_BIAS_NEAR = 6
MASKED = -1e30
INT32_MIN = -2 ** 31
GROUP = H_C // H_KV


def _dot_nt(a, b):
    return lax.dot_general(a, b, (((1,), (1,)), ((), ())), preferred_element_type=F32)


def _dsa_kernel(qc_ref, qi_ref, small_ref, k_ref, v_ref, kidx_ref, bias_ref, o_ref,
                key_ref, qis_ref, wb_ref, qg_ref, acc_ref, m_ref, l_ref, *, tq, q0_tile, seq_len, topk):
    qt = q0_tile + pl.program_id(1)
    n_vis = qt + 1
    row = lax.broadcasted_iota(jnp.int32, (tq, SCORE_TILE), 0)
    lane = lax.broadcasted_iota(jnp.int32, (tq, SCORE_TILE), 1)
    q_chunk = (qt * KEY_TILE + row) // CHUNK
    pad_rows = KEY_TILE - tq

    qi = (qi_ref[...] * D_IDX ** -0.5).astype(BF16)
    w_lane = COL["w_i"][0] - COL["k_i"][0]
    w = small_ref[:, w_lane:w_lane + H_IDX] * H_IDX ** -0.5
    for i in range(H_IDX // 2):
        qis_ref[i * tq:(i + 1) * tq, :] = qi[:, i * LANE:(i + 1) * LANE]
        for j in range(2):
            wb_ref[i * tq:(i + 1) * tq, j * SCORE_TILE:(j + 1) * SCORE_TILE] = jnp.broadcast_to(
                w[:, 2 * i + j:2 * i + j + 1], (tq, SCORE_TILE))
    first_head = lax.broadcasted_iota(jnp.int32, (SCORE_TILE, LANE), 1) < D_IDX

    def score_tile(u, carry):
        start = pl.multiple_of(u * SCORE_TILE, SCORE_TILE)
        kd = kidx_ref[0, pl.ds(start, SCORE_TILE), :]
        zero = jnp.zeros_like(kd)
        kt2 = jnp.concatenate([jnp.where(first_head, kd, zero), jnp.where(first_head, zero, kd)], axis=0)
        s = _dot_nt(qis_ref[...], kt2)
        r = jnp.maximum(s, 0.0) * wb_ref[...]
        r = jnp.sum(r.reshape(H_IDX // 2, tq, 2 * SCORE_TILE), axis=0)
        score = r[:, :SCORE_TILE] + r[:, SCORE_TILE:] + 0.0
        bits = pltpu.bitcast(score, jnp.int32)
        key = jnp.where(bits < 0, bits ^ jnp.int32(0x7FFFFFFF), bits)
        kpos = start + lane
        visible = ((kpos // CHUNK) <= q_chunk) & (kpos < seq_len)
        key = jnp.where(visible, key, jnp.int32(INT32_MIN))
        if pad_rows:
            key = jnp.concatenate([key, jnp.full((pad_rows, SCORE_TILE), INT32_MIN, jnp.int32)], axis=0)
        for j in range(SCORE_TILE // KEY_TILE):
            key_ref[u * (SCORE_TILE // KEY_TILE) + j] = key[:, j * KEY_TILE:(j + 1) * KEY_TILE].T
        return carry

    lax.fori_loop(0, pl.cdiv(n_vis * KEY_TILE, SCORE_TILE), score_tile, 0)

    def count_keys(pred):
        per_step = SCORE_TILE // KEY_TILE

        def count_tiles(u, cnt):
            for j in range(per_step):
                t = u * per_step + j
                kpos = t * KEY_TILE + lax.broadcasted_iota(jnp.int32, (KEY_TILE, KEY_TILE), 0)
                cnt = cnt + jnp.where(pred(key_ref[t], kpos), 1.0, 0.0)
            return cnt

        cnt = lax.fori_loop(0, pl.cdiv(n_vis, per_step), count_tiles, jnp.zeros((KEY_TILE, KEY_TILE), F32))
        return jnp.sum(cnt, axis=0, keepdims=True)

    thr = jnp.full((1, KEY_TILE), INT32_MIN, jnp.int32)
    for bit in range(31, -1, -1):
        cand = thr + jnp.int32(-2 ** 31 if bit == 31 else 2 ** bit)
        thr = jnp.where(count_keys(lambda key, kpos, cand=cand: key >= cand) >= float(topk), cand, thr)
    thr = jnp.maximum(thr, jnp.int32(INT32_MIN + 1))

    n_above = count_keys(lambda key, kpos: key > thr)
    n_tied = count_keys(lambda key, kpos: key == thr)
    ties_wanted = float(topk) - n_above
    pos_bits = (key_ref.shape[0] * KEY_TILE - 1).bit_length()

    def find_last_tie():
        last = jnp.zeros((1, KEY_TILE), jnp.int32)
        for bit in range(pos_bits - 1, -1, -1):
            cand = last | jnp.int32(2 ** bit)
            before = count_keys(lambda key, kpos, cand=cand: (key == thr) & (kpos < cand))
            last = jnp.where(before < ties_wanted, cand, last)
        return last

    surplus = jnp.max(n_tied - ties_wanted) > 0.0
    last_tie = lax.cond(surplus, find_last_tie, lambda: jnp.full((1, KEY_TILE), 2 ** pos_bits - 1, jnp.int32))

    kvs = range(H_KV)
    for kv in kvs:
        for g in range(GROUP):
            c0 = (kv * GROUP + g) * DH_C
            qh = (qc_ref[:, c0:c0 + DH_C] * DH_C ** -0.5).astype(BF16)
            if pad_rows:
                qh = jnp.concatenate([qh, jnp.zeros((pad_rows, DH_C), BF16)], axis=0)
            qg_ref[kv, g * KEY_TILE:(g + 1) * KEY_TILE, :] = qh
    acc_ref[...] = jnp.zeros(acc_ref.shape, F32)
    m_ref[...] = jnp.full(m_ref.shape, MASKED, F32)
    l_ref[...] = jnp.zeros(l_ref.shape, F32)

    def attend_tile(t, carry, near):
        start = pl.multiple_of(t * KEY_TILE, KEY_TILE)
        key = key_ref[t]
        kpos = start + lax.broadcasted_iota(jnp.int32, (KEY_TILE, KEY_TILE), 0)
        sel = (key > thr) | ((key == thr) & (kpos <= last_tie))
        penalty = jnp.concatenate([jnp.where(sel, 0.0, MASKED)] * GROUP, axis=1)
        s = [_dot_nt(k_ref[0, pl.ds(start, KEY_TILE), kv * DH_C:(kv + 1) * DH_C], qg_ref[kv]) for kv in kvs]
        if near:
            sm = [s[kv] + bias_ref[qt - t, kv] + penalty for kv in kvs]
        else:
            sm = [s[kv] + penalty for kv in kvs]
        m_old = [m_ref[kv:kv + 1, :] for kv in kvs]
        m_new = [jnp.maximum(m_old[kv], jnp.max(sm[kv], axis=0, keepdims=True)) for kv in kvs]
        alpha = [jnp.exp(m_old[kv] - m_new[kv]) for kv in kvs]
        p = [jnp.exp(sm[kv] - m_new[kv]) for kv in kvs]
        pv = [lax.dot_general(v_ref[0, pl.ds(start, KEY_TILE), kv * DH_C:(kv + 1) * DH_C], p[kv].astype(BF16),
                              (((0,), (0,)), ((), ())), preferred_element_type=F32) for kv in kvs]
        for kv in kvs:
            l_ref[kv:kv + 1, :] = alpha[kv] * l_ref[kv:kv + 1, :] + jnp.sum(p[kv], axis=0, keepdims=True)
            acc_ref[kv] = alpha[kv] * acc_ref[kv] + pv[kv]
            m_ref[kv:kv + 1, :] = m_new[kv]
        return carry

    n_far = jnp.maximum(n_vis - N_BIAS_NEAR, 0)
    lax.fori_loop(0, n_far, partial(attend_tile, near=False), 0)
    lax.fori_loop(n_far, n_vis, partial(attend_tile, near=True), 0)
    for kv in kvs:
        out = (acc_ref[kv] / l_ref[kv:kv + 1, :]).T
        for g in range(GROUP):
            c0 = (kv * GROUP + g) * DH_C
            o_ref[:, c0:c0 + DH_C] = out[g * KEY_TILE:g * KEY_TILE + tq].astype(o_ref.dtype)


def dsa_bias_tiles(rel_bias):
    i = jnp.arange(KEY_TILE)
    d = jnp.arange(N_BIAS_NEAR + 1)
    rel = (i[None, :, None] - i[None, None, :]) - KEY_TILE * d[:, None, None]
    onehot = (t5_bucket(rel)[..., None] == jnp.arange(NUM_BUCKETS)).astype(F32)
    tiles = jnp.einsum('dkqn,nh->dkqh', onehot, rel_bias.astype(F32), precision=lax.Precision.HIGHEST)
    tiles = (tiles[:N_BIAS_NEAR] - tiles[N_BIAS_NEAR:]).reshape(N_BIAS_NEAR, KEY_TILE, KEY_TILE, H_KV, GROUP)
    return jnp.transpose(tiles, (0, 3, 1, 4, 2)).reshape(N_BIAS_NEAR, H_KV, KEY_TILE, GROUP * KEY_TILE)


def dsa_attention(proj, row0, b, t, k_all, v_all, kidx_all, bias_tiles, p_len, seq_len, topk):
    tq = KEY_TILE if t % KEY_TILE == 0 else t
    assert p_len % KEY_TILE == 0 and KEY_TILE % tq == 0 and (tq == KEY_TILE or t == tq) and row0 % tq == 0
    nb = t // tq
    lp = k_all.shape[1]
    assert lp % SCORE_TILE == 0 and lp >= p_len + nb * KEY_TILE
    assert 8 * (MAX_DISTANCE / 8) ** (7 / 8) < KEY_TILE * N_BIAS_NEAR - (KEY_TILE - 1)
    rb = row0 // tq
    kern = partial(_dsa_kernel, tq=tq, q0_tile=p_len // KEY_TILE, seq_len=seq_len, topk=topk)
    return pl.pallas_call(
        kern,
        grid=(b, nb),
        in_specs=[pl.BlockSpec((tq, W_C), lambda bi, j: (rb + bi * nb + j, COL["q_c"][0] // W_C)),
                  pl.BlockSpec((tq, H_IDX * D_IDX), lambda bi, j: (rb + bi * nb + j, COL["q_i"][0] // (H_IDX * D_IDX))),
                  pl.BlockSpec((tq, LANE), lambda bi, j: (rb + bi * nb + j, COL["k_i"][0] // LANE)),
                  pl.BlockSpec((1, lp, H_KV * DH_C), lambda bi, j: (bi, 0, 0)),
                  pl.BlockSpec((1, lp, H_KV * DH_C), lambda bi, j: (bi, 0, 0)),
                  pl.BlockSpec((1, lp, 2 * D_IDX), lambda bi, j: (bi, 0, 0)),
                  pl.BlockSpec((N_BIAS_NEAR, H_KV, KEY_TILE, GROUP * KEY_TILE), lambda bi, j: (0, 0, 0, 0))],
        out_specs=pl.BlockSpec((tq, W_C), lambda bi, j: (bi * nb + j, 0)),
        out_shape=jax.ShapeDtypeStruct((b * t, W_C), BF16),
        scratch_shapes=[pltpu.VMEM((lp // KEY_TILE, KEY_TILE, KEY_TILE), jnp.int32),
                        pltpu.VMEM((H_IDX // 2 * tq, 2 * D_IDX), BF16),
                        pltpu.VMEM((H_IDX // 2 * tq, 2 * SCORE_TILE), F32),
                        pltpu.VMEM((H_KV, GROUP * KEY_TILE, DH_C), BF16),
                        pltpu.VMEM((H_KV, DH_C, GROUP * KEY_TILE), F32),
                        pltpu.VMEM((SUBLANE, GROUP * KEY_TILE), F32),
                        pltpu.VMEM((SUBLANE, GROUP * KEY_TILE), F32)],
        compiler_params=_cparams(("parallel", "arbitrary")),
        name="dsa_attention",
    )(proj, proj, proj, k_all, v_all, kidx_all, bias_tiles)


def branches(proj_all, new_rows, row0, b, t, past_k, past_v, past_kidx, s_ret, s_delta, s_conv,
             ret_gn, conv_w, a_log, dt_bias, d_norm, bias_tiles):
    p_len = past_k.shape[1]
    seq_len = p_len + t
    topk = min(TOPK_MAX, seq_len // 4)
    lp = -(-seq_len // SCORE_TILE) * SCORE_TILE
    k_c, v_c, k_i, k_bf, v_bf, ki_bf = (x[row0:row0 + b * t].reshape(b, t, x.shape[-1]) for x in new_rows)

    o, w = COL["qkv_b"]
    qkv_tail = jnp.stack([proj_all[row0 + t - (CONV_W - 1) + j:row0 + b * t:t, o:o + w]
                          for j in range(CONV_W - 1)], axis=1)
    conv_new = jnp.concatenate([s_conv, qkv_tail], axis=1)[:, -(CONV_W - 1):]

    o_a, ret_new = retention(proj_all, row0, b, t, p_len, s_ret, ret_gn)
    o_b, delta_new = gated_delta(proj_all, row0, b, t, s_conv, s_delta, conv_w, a_log, dt_bias, d_norm)

    def all_keys(past, new, reps=1):
        if p_len == 0 and lp == seq_len:
            return new
        w = past.shape[-1] * reps
        past = jnp.tile(past.reshape(b, p_len, -1).astype(BF16), (1, 1, reps))
        return jnp.concatenate([past, new, jnp.zeros((b, lp - seq_len, w), BF16)], axis=1)

    o_c = dsa_attention(proj_all, row0, b, t, all_keys(past_k.reshape(b, p_len, KV_W), k_bf),
                        all_keys(past_v.reshape(b, p_len, KV_W), v_bf), all_keys(past_kidx, ki_bf, 2),
                        bias_tiles, p_len, seq_len, topk)
    caches = (k_c.reshape(b, t, H_KV, DH_C), v_c.reshape(b, t, H_KV, DH_C), k_i, ret_new, delta_new, conv_new)
    return jnp.stack([o_a, o_b, o_c]), caches


def _prep_w_in(w):
    parts = [w[:, MODEL_COL[n][0]:MODEL_COL[n][0] + MODEL_COL[n][1]] for n in _KERNEL_ORDER]
    parts.append(jnp.zeros((w.shape[0], D_IN_PAD - D_IN), w.dtype))
    return jnp.concatenate(parts, axis=1).astype(BF16)


def kernel(x_prompt, x_sample, cache_k, cache_v, cache_kidx, state_ret, state_delta, state_conv, norm_mix, w_in, ret_gn, conv_w, delta_a_log, delta_dt_bias, delta_norm, rel_bias, w_branch_a, w_branch_b, w_branch_c, w_out, norm_ffn, w_ffn_gate, w_ffn_up, w_ffn_down, norm_final):
    dt = x_prompt.dtype
    bp, tp, _ = x_prompt.shape
    bs, ts, _ = x_sample.shape
    depth = w_in.shape[0]
    n_p = bp * tp
    zk = jnp.zeros((bp, 0, H_KV, DH_C), dt)
    zkidx = jnp.zeros((bp, 0, D_IDX), dt)
    zret = jnp.zeros((bp, H_A, DK_A, DV_A), dt)
    zdelta = jnp.zeros((bp, H_B, DK_B, DV_B), dt)
    zconv = jnp.zeros((bp, CONV_W - 1, CONV_CH), dt)

    h = jnp.concatenate([x_prompt.reshape(n_p, D_MODEL), x_sample.reshape(bs * ts, D_MODEL)], axis=0)
    new_p, new_s = [], []
    bias_tiles = dsa_bias_tiles(rel_bias)
    for l in range(depth):
        u = rmsnorm(h, norm_mix[l], BF16)
        proj, *new_rows = matmul_in(u, _prep_w_in(w_in[l]))
        bw = (ret_gn[l], conv_w[l], delta_a_log[l], delta_dt_bias[l], delta_norm[l], bias_tiles)
        o_p, sp = branches(proj, new_rows, 0, bp, tp, zk, zk, zkidx, zret, zdelta, zconv, *bw)
        o_s, ss = branches(proj, new_rows, n_p, bs, ts, cache_k[l], cache_v[l], cache_kidx[l],
                           state_ret[l], state_delta[l], state_conv[l], *bw)
        new_p.append(sp)
        new_s.append(ss)
        o_abc = jnp.concatenate([o_p, o_s], axis=1)
        w_br = jnp.stack([w_branch_a[l], w_branch_b[l], w_branch_c[l]]).astype(BF16)
        merged = merge_branches(o_abc, w_br, proj)
        h = matmul_residual(merged, w_out[l].astype(BF16), h)
        u2 = rmsnorm(h, norm_ffn[l], BF16)
        act = ffn_gate_up(u2, w_ffn_gate[l].astype(BF16), w_ffn_up[l].astype(BF16))
        h = ffn_down(act, w_ffn_down[l].astype(BF16), h)

    y_prompt = rmsnorm(h, norm_final, F32, 0, n_p).reshape(bp, tp, D_MODEL)
    y_sample = rmsnorm(h, norm_final, F32, n_p, bs * ts).reshape(bs, ts, D_MODEL)
    k_p, v_p, kidx_p, ret_p, delta_p, conv_p = [jnp.stack([s[i] for s in new_p]) for i in range(6)]
    k_s, v_s, kidx_s, ret_s, delta_s, conv_s = [jnp.stack([s[i] for s in new_s]) for i in range(6)]
    return (y_prompt, y_sample, k_p, v_p, kidx_p, ret_p, delta_p, conv_p,
            k_s, v_s, kidx_s, ret_s, delta_s, conv_s)
```

```python
import math
from functools import partial

import jax
import jax.numpy as jnp
from jax import lax
from jax.experimental import pallas as pl
from jax.experimental.pallas import tpu as pltpu

F32 = jnp.float32
BF16 = jnp.bfloat16

D_MODEL = 4096
CHUNK = 64
EPS = 1e-6
H_A, DK_A, DV_A = 8, 128, 256
W_A = H_A * DV_A
ROPE_BASE = 10000.0
H_B, DK_B, DV_B = 16, 128, 128
W_B = H_B * DV_B
CONV_W = 4
CONV_CH = 2 * H_B * DK_B + H_B * DV_B
H_C, H_KV, DH_C = 16, 4, 128
W_C = H_C * DH_C
KV_W = H_KV * DH_C
H_IDX, D_IDX = 32, 64
TOPK_MAX = 256
NUM_BUCKETS = 32
MAX_DISTANCE = 1024

LANE = 128
SUBLANE = 8
VMEM_LIMIT_V7X = 56 * 1024 * 1024
TM = 1024
TN = 1024
TIME_BLOCK = 512

_MODEL_ORDER = [("q_a", H_A * DK_A), ("k_a", H_A * DK_A), ("v_a", W_A), ("g_a", W_A),
                ("qkv_b", CONV_CH), ("z_b", W_B), ("a_b", H_B), ("b_b", H_B),
                ("q_c", W_C), ("k_c", KV_W), ("v_c", KV_W), ("q_i", H_IDX * D_IDX), ("k_i", D_IDX), ("w_i", H_IDX),
                ("gate_a", D_MODEL), ("gate_b", D_MODEL), ("gate_c", D_MODEL)]
_KERNEL_ORDER = ["q_a", "k_a", "v_a", "g_a", "qkv_b", "z_b", "q_c", "q_i", "k_c", "v_c",
                 "gate_a", "gate_b", "gate_c", "k_i", "w_i", "a_b", "b_b"]


def _offsets(order, widths):
    table, off = {}, 0
    for name in order:
        table[name] = (off, widths[name])
        off += widths[name]
    return table, off


_WIDTH = dict(_MODEL_ORDER)
MODEL_COL, D_IN = _offsets([n for n, _ in _MODEL_ORDER], _WIDTH)
COL, _ = _offsets(_KERNEL_ORDER, _WIDTH)
D_IN_PAD = -(-D_IN // TN) * TN


def _cparams(sem):
    return pltpu.CompilerParams(dimension_semantics=sem, vmem_limit_bytes=VMEM_LIMIT_V7X)


def _rmsnorm_kernel(x_ref, g_ref, o_ref):
    x = x_ref[...]
    y = x * lax.rsqrt(jnp.mean(x * x, axis=-1, keepdims=True) + EPS)
    o_ref[...] = (y * g_ref[...]).astype(o_ref.dtype)


def rmsnorm(x, g, out_dtype, row0=0, rows=None, tr=512):
    d = x.shape[1]
    m = x.shape[0] if rows is None else rows
    assert row0 % tr == 0 and m % tr == 0
    rb = row0 // tr
    return pl.pallas_call(
        _rmsnorm_kernel,
        grid=(m // tr,),
        in_specs=[pl.BlockSpec((tr, d), lambda i: (rb + i, 0)), pl.BlockSpec((1, d), lambda i: (0, 0))],
        out_specs=pl.BlockSpec((tr, d), lambda i: (i, 0)),
        out_shape=jax.ShapeDtypeStruct((m, d), out_dtype),
        compiler_params=_cparams(("parallel",)),
        name="rmsnorm",
    )(x, g.reshape(1, d))


def _mm_in_kernel(a_ref, b_ref, o_ref, k_ref, v_ref, ki_ref, kb_ref, vb_ref, kib_ref, *, kv_block, small_block):
    j = pl.program_id(1)
    o_ref[...] = jnp.dot(a_ref[...], b_ref[...], preferred_element_type=F32)

    @pl.when(j == kv_block)
    def _():
        k, v = o_ref[:, :KV_W], o_ref[:, KV_W:2 * KV_W]
        k_ref[...] = k
        v_ref[...] = v
        kb_ref[...] = k.astype(BF16)
        vb_ref[...] = v.astype(BF16)

    @pl.when(j == small_block)
    def _():
        ki = o_ref[:, :D_IDX]
        ki_ref[...] = ki
        kib_ref[...] = jnp.concatenate([ki, ki], axis=1).astype(BF16)


def matmul_in(a, b, tn=TN):
    m, k = a.shape
    n = b.shape[1]
    assert COL["k_c"][0] % tn == 0 and COL["v_c"][0] == COL["k_c"][0] + KV_W and COL["k_i"][0] % tn == 0
    kern = partial(_mm_in_kernel, kv_block=COL["k_c"][0] // tn, small_block=COL["k_i"][0] // tn)

    def row_spec(w):
        return pl.BlockSpec((TM, w), lambda i, j: (i, 0))

    return pl.pallas_call(
        kern,
        grid=(m // TM, n // tn),
        in_specs=[pl.BlockSpec((TM, k), lambda i, j: (i, 0), pipeline_mode=pl.Buffered(1)),
                  pl.BlockSpec((k, tn), lambda i, j: (0, j))],
        out_specs=[pl.BlockSpec((TM, tn), lambda i, j: (i, j)), row_spec(KV_W), row_spec(KV_W), row_spec(D_IDX),
                   row_spec(KV_W), row_spec(KV_W), row_spec(2 * D_IDX)],
        out_shape=[jax.ShapeDtypeStruct((m, n), F32), jax.ShapeDtypeStruct((m, KV_W), F32),
                   jax.ShapeDtypeStruct((m, KV_W), F32), jax.ShapeDtypeStruct((m, D_IDX), F32),
                   jax.ShapeDtypeStruct((m, KV_W), BF16), jax.ShapeDtypeStruct((m, KV_W), BF16),
                   jax.ShapeDtypeStruct((m, 2 * D_IDX), BF16)],
        compiler_params=_cparams(("parallel", "arbitrary")),
        name="matmul_in",
    )(a, b)


N_BRANCH = 3


def _merge_kernel(*refs):
    o_refs, w_refs, g_refs = (refs[i * N_BRANCH:(i + 1) * N_BRANCH] for i in range(3))
    out_ref = refs[3 * N_BRANCH]
    acc = None
    for o_ref, w_ref, g_ref in zip(o_refs, w_refs, g_refs):
        y = jnp.dot(o_ref[0], w_ref[0], preferred_element_type=F32) * jax.nn.sigmoid(g_ref[...])
        acc = y if acc is None else acc + y
    out_ref[...] = acc.astype(out_ref.dtype)


def merge_branches(o_abc, w_br, proj, tn=TN // 2):
    _, m, k = o_abc.shape
    n = w_br.shape[2]
    gate0 = COL["gate_a"][0] // tn
    nj = n // tn

    def per_branch(make):
        return [make(r) for r in range(N_BRANCH)]

    return pl.pallas_call(
        _merge_kernel,
        grid=(m // TM, nj),
        in_specs=per_branch(lambda r: pl.BlockSpec((1, TM, k), lambda i, j: (r, i, 0), pipeline_mode=pl.Buffered(1)))
        + per_branch(lambda r: pl.BlockSpec((1, k, tn), lambda i, j: (r, 0, j)))
        + per_branch(lambda r: pl.BlockSpec((TM, tn), lambda i, j: (i, gate0 + r * nj + j))),
        out_specs=pl.BlockSpec((TM, tn), lambda i, j: (i, j)),
        out_shape=jax.ShapeDtypeStruct((m, n), BF16),
        compiler_params=_cparams(("parallel", "parallel")),
        name="merge_branches",
    )(*([o_abc] * N_BRANCH), *([w_br] * N_BRANCH), *([proj] * N_BRANCH))


def _mm_res_kernel(a_ref, b_ref, h_ref, o_ref):
    o_ref[...] = h_ref[...] + jnp.dot(a_ref[...], b_ref[...], preferred_element_type=F32)


def matmul_residual(a, b, h, tn=TN):
    m, k = a.shape
    n = b.shape[1]
    return pl.pallas_call(
        _mm_res_kernel,
        grid=(m // TM, n // tn),
        in_specs=[pl.BlockSpec((TM, k), lambda i, j: (i, 0)), pl.BlockSpec((k, tn), lambda i, j: (0, j)),
                  pl.BlockSpec((TM, tn), lambda i, j: (i, j))],
        out_specs=pl.BlockSpec((TM, tn), lambda i, j: (i, j)),
        out_shape=jax.ShapeDtypeStruct((m, n), F32),
        compiler_params=_cparams(("parallel", "parallel")),
        name="matmul_out",
    )(a, b, h)


def _gateup_kernel(u_ref, wg_ref, wu_ref, o_ref):
    u = u_ref[...]
    g = jnp.dot(u, wg_ref[...], preferred_element_type=F32)
    up = jnp.dot(u, wu_ref[...], preferred_element_type=F32)
    o_ref[...] = (jax.nn.silu(g) * up).astype(o_ref.dtype)


def ffn_gate_up(u, wg, wu, tn=512):
    m, k = u.shape
    n = wg.shape[1]
    return pl.pallas_call(
        _gateup_kernel,
        grid=(m // TM, pl.cdiv(n, tn)),
        in_specs=[pl.BlockSpec((TM, k), lambda i, j: (i, 0)),
                  pl.BlockSpec((k, tn), lambda i, j: (0, j)),
                  pl.BlockSpec((k, tn), lambda i, j: (0, j))],
        out_specs=pl.BlockSpec((TM, tn), lambda i, j: (i, j)),
        out_shape=jax.ShapeDtypeStruct((m, n), BF16),
        compiler_params=_cparams(("parallel", "parallel")),
        name="ffn_gate_up",
    )(u, wg, wu)


def ffn_down(a, b, h, tm=768, tn=512):
    m, k = a.shape
    n = b.shape[1]
    return pl.pallas_call(
        _mm_res_kernel,
        grid=(m // tm, n // tn),
        in_specs=[pl.BlockSpec((tm, k), lambda i, j: (i, 0), pipeline_mode=pl.Buffered(1)),
                  pl.BlockSpec((k, tn), lambda i, j: (0, j)),
                  pl.BlockSpec((tm, tn), lambda i, j: (i, j))],
        out_specs=pl.BlockSpec((tm, tn), lambda i, j: (i, j)),
        out_shape=jax.ShapeDtypeStruct((m, n), F32),
        compiler_params=_cparams(("parallel", "parallel")),
        name="ffn_down",
    )(a, b, h)


RET_HEADS = 4


def _retention_kernel(q_ref, k_ref, v_ref, g_ref, cos_ref, sin_ref, idec_ref, qdec_ref, kdec_ref, cdec_ref, gn_ref,
                      s0_ref, o_ref, sfin_ref, s_ref, *, tb, c):
    tblk = pl.program_id(2)
    heads = range(RET_HEADS)

    @pl.when(tblk == 0)
    def _():
        s_ref[...] = s0_ref[0]

    def rotate(x, cos, sin):
        return x * cos + pltpu.roll(x, DK_A // 2, 1) * sin

    def chunk_step(ci, carry):
        c0 = pl.multiple_of(ci * c, c)
        cos = cos_ref[pl.ds(c0, c), :]
        sin = sin_ref[pl.ds(c0, c), :]
        qr = [rotate(q_ref[pl.ds(c0, c), h * DK_A:(h + 1) * DK_A], cos, sin) for h in heads]
        kr = [rotate(k_ref[pl.ds(c0, c), h * DK_A:(h + 1) * DK_A], cos, sin) * DK_A ** -0.5 for h in heads]
        v = [v_ref[pl.ds(c0, c), h * DV_A:(h + 1) * DV_A].astype(BF16) for h in heads]
        s = [s_ref[h] for h in heads]
        att = [_dot_nt(qr[h].astype(BF16), kr[h].astype(BF16)) * idec_ref[h] for h in heads]
        o = [_bdot(att[h], v[h]) + _bdot(qr[h] * qdec_ref[h], s[h]) for h in heads]
        for h in heads:
            s_ref[h] = s[h] * cdec_ref[h, 0:1, :] + lax.dot_general(
                (kr[h] * kdec_ref[h]).astype(BF16), v[h], (((0,), (0,)), ((), ())), preferred_element_type=F32)
        for h in heads:
            mu = jnp.mean(o[h], axis=-1, keepdims=True)
            d = o[h] - mu
            var = jnp.mean(d * d, axis=-1, keepdims=True)
            cols = slice(h * DV_A, (h + 1) * DV_A)
            g = g_ref[pl.ds(c0, c), cols]
            o_ref[pl.ds(c0, c), cols] = (
                g * jax.nn.sigmoid(g) * (d * lax.rsqrt(var + EPS) * gn_ref[:, cols])).astype(o_ref.dtype)
        return carry

    lax.fori_loop(0, tb // c, chunk_step, 0)

    @pl.when(tblk == pl.num_programs(2) - 1)
    def _():
        sfin_ref[0] = s_ref[...]


def retention(proj, row0, b, t, p_len, s_ret, ret_gn):
    c = 2 * CHUNK if t % (2 * CHUNK) == 0 else t
    tb = TIME_BLOCK if t % TIME_BLOCK == 0 else t
    assert tb % c == 0 and row0 % tb == 0
    nt = t // tb
    rb = row0 // tb
    half = DK_A // 2
    inv = ROPE_BASE ** (-jnp.arange(half, dtype=F32) / half)
    ang = (p_len + jnp.arange(t)).astype(F32)[:, None] * inv[None, :]
    cos2 = jnp.concatenate([jnp.cos(ang), jnp.cos(ang)], axis=1)
    sin2 = jnp.concatenate([-jnp.sin(ang), jnp.sin(ang)], axis=1)
    log_gamma = jnp.log(1.0 - 2.0 ** (-5.0 - jnp.arange(H_A, dtype=F32)))
    i = jnp.arange(c, dtype=F32)
    rel = i[:, None] - i[None, :]
    idec = jnp.exp(jnp.where(rel[None] >= 0, rel[None] * log_gamma[:, None, None], -jnp.inf))
    qdec = jnp.broadcast_to(jnp.exp((i[None, :] + 1.0) * log_gamma[:, None])[:, :, None], (H_A, c, DK_A))
    kdec = jnp.broadcast_to(jnp.exp((c - 1.0 - i[None, :]) * log_gamma[:, None])[:, :, None], (H_A, c, DK_A))
    cdec = jnp.broadcast_to(jnp.exp(c * log_gamma)[:, None, None], (H_A, SUBLANE, DV_A))
    wk, wv = RET_HEADS * DK_A, RET_HEADS * DV_A
    qb0 = COL["q_a"][0] // wk
    kb0 = COL["k_a"][0] // wk
    vb0 = COL["v_a"][0] // wv
    gb0 = COL["g_a"][0] // wv
    sspec = pl.BlockSpec((1, RET_HEADS, DK_A, DV_A), lambda bi, h, k: (bi, h, 0, 0))
    return pl.pallas_call(
        partial(_retention_kernel, tb=tb, c=c),
        grid=(b, H_A // RET_HEADS, nt),
        in_specs=[pl.BlockSpec((tb, wk), lambda bi, h, k: (rb + bi * nt + k, qb0 + h)),
                  pl.BlockSpec((tb, wk), lambda bi, h, k: (rb + bi * nt + k, kb0 + h)),
                  pl.BlockSpec((tb, wv), lambda bi, h, k: (rb + bi * nt + k, vb0 + h)),
                  pl.BlockSpec((tb, wv), lambda bi, h, k: (rb + bi * nt + k, gb0 + h)),
                  pl.BlockSpec((tb, DK_A), lambda bi, h, k: (k, 0)),
                  pl.BlockSpec((tb, DK_A), lambda bi, h, k: (k, 0)),
                  pl.BlockSpec((RET_HEADS, c, c), lambda bi, h, k: (h, 0, 0)),
                  pl.BlockSpec((RET_HEADS, c, DK_A), lambda bi, h, k: (h, 0, 0)),
                  pl.BlockSpec((RET_HEADS, c, DK_A), lambda bi, h, k: (h, 0, 0)),
                  pl.BlockSpec((RET_HEADS, SUBLANE, DV_A), lambda bi, h, k: (h, 0, 0)),
                  pl.BlockSpec((1, wv), lambda bi, h, k: (0, h)),
                  sspec],
        out_specs=[pl.BlockSpec((tb, wv), lambda bi, h, k: (bi * nt + k, h)), sspec],
        out_shape=[jax.ShapeDtypeStruct((b * t, W_A), BF16),
                   jax.ShapeDtypeStruct((b, H_A, DK_A, DV_A), F32)],
        scratch_shapes=[pltpu.VMEM((RET_HEADS, DK_A, DV_A), F32)],
        compiler_params=_cparams(("parallel", "parallel", "arbitrary")),
        name="retention",
    )(proj, proj, proj, proj, cos2, sin2, idec, qdec, kdec, cdec, ret_gn.reshape(1, W_A), s_ret)


DELTA_HEADS = 8
DELTA_W = DELTA_HEADS * DK_B
HIST_ROWS = SUBLANE
INV_BASE = 16


def _bdot(a, b):
    return jnp.dot(a.astype(BF16), b.astype(BF16), preferred_element_type=F32)


def _hdot(a, b):
    return jnp.dot(a, b, preferred_element_type=F32, precision=lax.Precision.HIGHEST)


def _split_bf16(x):
    hi = x.astype(BF16)
    return hi, (x - hi.astype(F32)).astype(BF16)


def _dot3(a, b):
    a_hi, a_lo = _split_bf16(a)
    b_hi, b_lo = _split_bf16(b)
    dot = partial(jnp.dot, preferred_element_type=F32)
    return dot(a_hi, b_hi) + (dot(a_hi, b_lo) + dot(a_lo, b_hi))


def _unit_lower_inverse(mats, c):
    ii = lax.broadcasted_iota(jnp.int32, (c, c), 0)
    jj = lax.broadcasted_iota(jnp.int32, (c, c), 1)
    eye = jnp.where(ii == jj, 1.0, 0.0).astype(F32)
    ps = [jnp.where((ii // INV_BASE) == (jj // INV_BASE), a, 0.0) for a in mats]
    ts = [eye - p for p in ps]
    for _ in range(int(math.log2(INV_BASE)) - 1):
        ps = [_dot3(p, p) for p in ps]
        ts = [_dot3(t, eye + p) for t, p in zip(ts, ps)]
    size = INV_BASE
    while size < c:
        half_blocks = ((ii // (2 * size)) == (jj // (2 * size))) & ((ii // size) != (jj // size))
        tl = [_dot3(t, jnp.where(half_blocks, a, 0.0)) for t, a in zip(ts, mats)]
        ts = [t - _dot3(x, t) for t, x in zip(ts, tl)]
        size *= 2
    return ts


def _delta_kernel(xq_ref, xk_ref, xv_ref, z_ref, small_ref, cwq_ref, cwk_ref, cwv_ref, hq_ref, hk_ref, hv_ref,
                  par_ref, dn_ref, s0_ref, o_ref, sfin_ref,
                  bq_ref, bk_ref, bv_ref, qn_ref, kn_ref, vv_ref, g_ref, beta_ref, s_ref, w_ref, qk_ref, el_ref,
                  *, tb, c):
    hg = pl.program_id(1)
    tblk = pl.program_id(2)

    @pl.when(tblk == 0)
    def _():
        bq_ref[0:HIST_ROWS, :] = hq_ref[0]
        bk_ref[0:HIST_ROWS, :] = hk_ref[0]
        bv_ref[0:HIST_ROWS, :] = hv_ref[0]
        s_ref[...] = s0_ref[0]

    def conv_silu(x_ref, buf_ref, cw_ref):
        buf_ref[HIST_ROWS:HIST_ROWS + tb, :] = x_ref[...]
        y = jnp.zeros((tb, DELTA_W), F32)
        for j in range(CONV_W):
            lo = HIST_ROWS - (CONV_W - 1) + j
            y = y + buf_ref[lo:lo + tb, :] * cw_ref[j:j + 1, :]
        buf_ref[0:HIST_ROWS, :] = buf_ref[tb:tb + HIST_ROWS, :]
        return y * jax.nn.sigmoid(y)

    def l2n(x):
        parts = []
        for h in range(DELTA_HEADS):
            xh = x[:, h * DK_B:(h + 1) * DK_B]
            parts.append(xh * lax.rsqrt(jnp.sum(xh * xh, axis=-1, keepdims=True) + EPS))
        return jnp.concatenate(parts, axis=1)

    qn_ref[...] = l2n(conv_silu(xq_ref, bq_ref, cwq_ref)) * DK_B ** -0.5
    kn_ref[...] = l2n(conv_silu(xk_ref, bk_ref, cwk_ref))
    vv_ref[...] = conv_silu(xv_ref, bv_ref, cwv_ref)
    sm = small_ref[...]
    x = sm + par_ref[1:2, :]
    softplus = jnp.maximum(x, 0.0) + jnp.log1p(jnp.exp(-jnp.abs(x)))
    g_ref[...] = -jnp.exp(par_ref[0:1, :]) * softplus
    beta_ref[...] = jax.nn.sigmoid(sm)

    ii = lax.broadcasted_iota(jnp.int32, (c, c), 0)
    jj = lax.broadcasted_iota(jnp.int32, (c, c), 1)
    incl = ii >= jj
    strict = ii > jj
    diag = ii == jj
    tril_ones = jnp.where(incl, 1.0, 0.0).astype(F32)
    a_lane = COL["a_b"][0] - COL["k_i"][0]
    b_lane = COL["b_b"][0] - COL["k_i"][0]

    n_chunks = tb // c
    p_unroll = 2 if n_chunks % 2 == 0 else 1

    def prepare_chunks(step, carry):
        lane = lax.broadcasted_iota(jnp.int32, (c, LANE), 1)
        items = []
        for j in range(p_unroll):
            ci = step * p_unroll + j
            c0 = pl.multiple_of(ci * c, c)
            gcum_all = _hdot(tril_ones, g_ref[pl.ds(c0, c), :])
            beta_all = beta_ref[pl.ds(c0, c), :]
            for h in range(DELTA_HEADS):
                items.append((ci, c0, h, gcum_all, beta_all))
        cols = [slice(h * DK_B, (h + 1) * DK_B) for _, _, h, _, _ in items]
        gc = [jnp.sum(jnp.where(lane == a_lane + hg * DELTA_HEADS + h, ga, 0.0), axis=-1, keepdims=True)
              for _, _, h, ga, _ in items]
        bc = [jnp.sum(jnp.where(lane == b_lane + hg * DELTA_HEADS + h, ba, 0.0), axis=-1, keepdims=True)
              for _, _, h, _, ba in items]
        gr = [jnp.sum(jnp.where(diag, jnp.broadcast_to(g, (c, c)), 0.0), axis=0, keepdims=True) for g in gc]
        g_last = [jnp.sum(jnp.where(ii[:, 0:1] == c - 1, g, 0.0), axis=0, keepdims=True) for g in gc]
        gam = [jnp.exp(jnp.where(incl, g - r, -jnp.inf)) for g, r in zip(gc, gr)]
        eg = [jnp.exp(g) for g in gc]
        qh = [qn_ref[pl.ds(it[1], c), cs] for it, cs in zip(items, cols)]
        kh = [kn_ref[pl.ds(it[1], c), cs] for it, cs in zip(items, cols)]
        vh = [vv_ref[pl.ds(it[1], c), cs] for it, cs in zip(items, cols)]
        kb = [k * b for k, b in zip(kh, bc)]
        a = [jnp.where(strict, _dot_nt(x.astype(BF16), k.astype(BF16)) * gm, 0.0) for x, k, gm in zip(kb, kh, gam)]
        qk = [_dot_nt(q.astype(BF16), k.astype(BF16)) * gm for q, k, gm in zip(qh, kh, gam)]
        tinv = _unit_lower_inverse(a, c)
        u = [_bdot(t, v * b) for t, v, b in zip(tinv, vh, bc)]
        w = [_bdot(t, x * e) for t, x, e in zip(tinv, kb, eg)]
        for n, (ci, c0, h, _, _) in enumerate(items):
            rows = pl.ds(c0, c)
            vv_ref[rows, cols[n]] = u[n]
            w_ref[rows, cols[n]] = w[n]
            qn_ref[rows, cols[n]] = qh[n] * eg[n]
            kn_ref[rows, cols[n]] = kh[n] * jnp.exp(g_last[n] - gc[n])
            qk_ref[ci * DELTA_HEADS + h] = qk[n]
            el_ref[ci * DELTA_HEADS + h] = jnp.broadcast_to(jnp.exp(g_last[n]), (SUBLANE, LANE))
        return carry

    lax.fori_loop(0, n_chunks // p_unroll, prepare_chunks, 0)

    def state_step(ci, carry):
        rows = pl.ds(pl.multiple_of(ci * c, c), c)
        heads = range(DELTA_HEADS)
        cols = [slice(h * DK_B, (h + 1) * DK_B) for h in heads]
        s = [s_ref[h] for h in heads]
        v_new = [vv_ref[rows, cols[h]] - _bdot(w_ref[rows, cols[h]], s[h]) for h in heads]
        o = [_bdot(qn_ref[rows, cols[h]], s[h]) + _bdot(qk_ref[ci * DELTA_HEADS + h], v_new[h]) for h in heads]
        for h in heads:
            s_ref[h] = s[h] * el_ref[ci * DELTA_HEADS + h][0:1, :] + lax.dot_general(
                kn_ref[rows, cols[h]].astype(BF16), v_new[h].astype(BF16), (((0,), (0,)), ((), ())),
                preferred_element_type=F32)
        for h in heads:
            on = o[h] * lax.rsqrt(jnp.mean(o[h] * o[h], axis=-1, keepdims=True) + EPS) * dn_ref[...]
            zh = z_ref[rows, cols[h]]
            o_ref[rows, cols[h]] = (on * (zh * jax.nn.sigmoid(zh))).astype(o_ref.dtype)
        return carry

    lax.fori_loop(0, n_chunks, state_step, 0)

    @pl.when(tblk == pl.num_programs(2) - 1)
    def _():
        sfin_ref[0] = s_ref[...]


def gated_delta(proj, row0, b, t, s_conv, s_delta, conv_w, a_log, dt_bias, d_norm):
    c = CHUNK if t % CHUNK == 0 else t
    tb = TIME_BLOCK if t % TIME_BLOCK == 0 else t
    assert tb % c == 0 and c % INV_BASE == 0 and tb >= HIST_ROWS and row0 % tb == 0
    nt = t // tb
    rb = row0 // tb
    ngrp = H_B // DELTA_HEADS
    qcol = COL["qkv_b"][0] // DELTA_W
    hist = jnp.concatenate([jnp.zeros((b, HIST_ROWS - (CONV_W - 1), CONV_CH), F32), s_conv], axis=1)
    par = jnp.zeros((SUBLANE, LANE), F32)
    a_lane = COL["a_b"][0] - COL["k_i"][0]
    par = par.at[0, a_lane:a_lane + H_B].set(a_log).at[1, a_lane:a_lane + H_B].set(dt_bias)

    def xspec(part):
        return pl.BlockSpec((tb, DELTA_W), lambda bi, g, k: (rb + bi * nt + k, qcol + part * ngrp + g))

    def wspec(part):
        return pl.BlockSpec((CONV_W, DELTA_W), lambda bi, g, k: (0, part * ngrp + g))

    def hspec(part):
        return pl.BlockSpec((1, HIST_ROWS, DELTA_W), lambda bi, g, k: (bi, 0, part * ngrp + g))

    sspec = pl.BlockSpec((1, DELTA_HEADS, DK_B, DV_B), lambda bi, g, k: (bi, g, 0, 0))
    return pl.pallas_call(
        partial(_delta_kernel, tb=tb, c=c),
        grid=(b, ngrp, nt),
        in_specs=[xspec(0), xspec(1), xspec(2),
                  pl.BlockSpec((tb, DELTA_W), lambda bi, g, k: (rb + bi * nt + k, COL["z_b"][0] // DELTA_W + g)),
                  pl.BlockSpec((tb, LANE), lambda bi, g, k: (rb + bi * nt + k, COL["k_i"][0] // LANE)),
                  wspec(0), wspec(1), wspec(2), hspec(0), hspec(1), hspec(2),
                  pl.BlockSpec((SUBLANE, LANE), lambda bi, g, k: (0, 0)),
                  pl.BlockSpec((1, DV_B), lambda bi, g, k: (0, 0)),
                  sspec],
        out_specs=[pl.BlockSpec((tb, DELTA_W), lambda bi, g, k: (bi * nt + k, g)), sspec],
        out_shape=[jax.ShapeDtypeStruct((b * t, W_B), BF16),
                   jax.ShapeDtypeStruct((b, H_B, DK_B, DV_B), F32)],
        scratch_shapes=[pltpu.VMEM((tb + HIST_ROWS, DELTA_W), F32)] * 3
        + [pltpu.VMEM((tb, DELTA_W), F32)] * 3
        + [pltpu.VMEM((tb, LANE), F32)] * 2
        + [pltpu.VMEM((DELTA_HEADS, DK_B, DV_B), F32),
           pltpu.VMEM((tb, DELTA_W), F32),
           pltpu.VMEM((tb // c * DELTA_HEADS, c, c), F32),
           pltpu.VMEM((tb // c * DELTA_HEADS, SUBLANE, LANE), F32)],
        compiler_params=_cparams(("parallel", "parallel", "arbitrary")),
        name="gated_delta",
    )(proj, proj, proj, proj, proj, conv_w, conv_w, conv_w, hist, hist, hist, par, d_norm.reshape(1, DV_B), s_delta)


def t5_bucket(rel):
    half = NUM_BUCKETS // 2
    exact = half // 2
    n = jnp.abs(rel)
    large = exact + (jnp.log(jnp.maximum(n, 1).astype(jnp.float32) / exact)
                     / math.log(MAX_DISTANCE / exact) * (half - exact)).astype(jnp.int32)
    large = jnp.minimum(large, half - 1)
    return jnp.where(rel > 0, half, 0) + jnp.where(n < exact, n, large)


KEY_TILE = LANE
SCORE_TILE = 2 * KEY_TILE
FAR_TILES = 2
N_BIAS_NEAR = 6
MASKED = -1e30
INT32_MIN = -2 ** 31
GROUP = H_C // H_KV


def _dot_nt(a, b):
    return lax.dot_general(a, b, (((1,), (1,)), ((), ())), preferred_element_type=F32)


def _dsa_kernel(qc_ref, qi_ref, small_ref, k_ref, v_ref, kidx_ref, bias_ref, o_ref,
                key_ref, qis_ref, wb_ref, qg_ref, acc_ref, m_ref, l_ref, *, tq, q0_tile, seq_len, topk):
    qt = q0_tile + pl.program_id(1)
    n_vis = qt + 1
    row = lax.broadcasted_iota(jnp.int32, (tq, SCORE_TILE), 0)
    lane = lax.broadcasted_iota(jnp.int32, (tq, SCORE_TILE), 1)
    q_chunk = (qt * KEY_TILE + row) // CHUNK
    pad_rows = KEY_TILE - tq

    qi = (qi_ref[...] * D_IDX ** -0.5).astype(BF16)
    w_lane = COL["w_i"][0] - COL["k_i"][0]
    w = small_ref[:, w_lane:w_lane + H_IDX] * H_IDX ** -0.5
    for i in range(H_IDX // 2):
        qis_ref[i * tq:(i + 1) * tq, :] = qi[:, i * LANE:(i + 1) * LANE]
        for j in range(2):
            wb_ref[i * tq:(i + 1) * tq, j * SCORE_TILE:(j + 1) * SCORE_TILE] = jnp.broadcast_to(
                w[:, 2 * i + j:2 * i + j + 1], (tq, SCORE_TILE))
    first_head = lax.broadcasted_iota(jnp.int32, (SCORE_TILE, LANE), 1) < D_IDX

    def score_tile(u, carry):
        start = pl.multiple_of(u * SCORE_TILE, SCORE_TILE)
        kd = kidx_ref[0, pl.ds(start, SCORE_TILE), :]
        zero = jnp.zeros_like(kd)
        kt2 = jnp.concatenate([jnp.where(first_head, kd, zero), jnp.where(first_head, zero, kd)], axis=0)
        s = _dot_nt(qis_ref[...], kt2)
        r = jnp.maximum(s, 0.0) * wb_ref[...]
        r = jnp.sum(r.reshape(H_IDX // 2, tq, 2 * SCORE_TILE), axis=0)
        score = r[:, :SCORE_TILE] + r[:, SCORE_TILE:] + 0.0
        bits = pltpu.bitcast(score, jnp.int32)
        key = jnp.where(bits < 0, bits ^ jnp.int32(0x7FFFFFFF), bits)
        kpos = start + lane
        visible = ((kpos // CHUNK) <= q_chunk) & (kpos < seq_len)
        key = jnp.where(visible, key, jnp.int32(INT32_MIN))
        if pad_rows:
            key = jnp.concatenate([key, jnp.full((pad_rows, SCORE_TILE), INT32_MIN, jnp.int32)], axis=0)
        for j in range(SCORE_TILE // KEY_TILE):
            key_ref[u * (SCORE_TILE // KEY_TILE) + j] = key[:, j * KEY_TILE:(j + 1) * KEY_TILE].T
        return carry

    lax.fori_loop(0, pl.cdiv(n_vis * KEY_TILE, SCORE_TILE), score_tile, 0)

    def count_keys(pred):
        per_step = SCORE_TILE // KEY_TILE

        def count_tiles(u, cnt):
            for j in range(per_step):
                t = u * per_step + j
                kpos = t * KEY_TILE + lax.broadcasted_iota(jnp.int32, (KEY_TILE, KEY_TILE), 0)
                cnt = cnt + jnp.where(pred(key_ref[t], kpos), 1.0, 0.0)
            return cnt

        cnt = lax.fori_loop(0, pl.cdiv(n_vis, per_step), count_tiles, jnp.zeros((KEY_TILE, KEY_TILE), F32))
        return jnp.sum(cnt, axis=0, keepdims=True)

    thr = jnp.full((1, KEY_TILE), INT32_MIN, jnp.int32)
    for bit in range(31, -1, -1):
        cand = thr + jnp.int32(-2 ** 31 if bit == 31 else 2 ** bit)
        thr = jnp.where(count_keys(lambda key, kpos, cand=cand: key >= cand) >= float(topk), cand, thr)
    thr = jnp.maximum(thr, jnp.int32(INT32_MIN + 1))

    n_above = count_keys(lambda key, kpos: key > thr)
    n_tied = count_keys(lambda key, kpos: key == thr)
    ties_wanted = float(topk) - n_above
    pos_bits = (key_ref.shape[0] * KEY_TILE - 1).bit_length()

    def find_last_tie():
        last = jnp.zeros((1, KEY_TILE), jnp.int32)
        for bit in range(pos_bits - 1, -1, -1):
            cand = last | jnp.int32(2 ** bit)
            before = count_keys(lambda key, kpos, cand=cand: (key == thr) & (kpos < cand))
            last = jnp.where(before < ties_wanted, cand, last)
        return last

    surplus = jnp.max(n_tied - ties_wanted) > 0.0
    last_tie = lax.cond(surplus, find_last_tie, lambda: jnp.full((1, KEY_TILE), 2 ** pos_bits - 1, jnp.int32))

    kvs = range(H_KV)
    for kv in kvs:
        for g in range(GROUP):
            c0 = (kv * GROUP + g) * DH_C
            qh = (qc_ref[:, c0:c0 + DH_C] * DH_C ** -0.5).astype(BF16)
            if pad_rows:
                qh = jnp.concatenate([qh, jnp.zeros((pad_rows, DH_C), BF16)], axis=0)
            qg_ref[kv, g * KEY_TILE:(g + 1) * KEY_TILE, :] = qh
    acc_ref[...] = jnp.zeros(acc_ref.shape, F32)
    m_ref[...] = jnp.full(m_ref.shape, MASKED, F32)
    l_ref[...] = jnp.zeros(l_ref.shape, F32)

    def attend_tile(t, carry, near, ntile=1):
        rows = ntile * KEY_TILE
        start = pl.multiple_of(t * KEY_TILE, KEY_TILE)
        key = jnp.concatenate([key_ref[t + j] for j in range(ntile)], axis=0) if ntile > 1 else key_ref[t]
        kpos = start + lax.broadcasted_iota(jnp.int32, (rows, KEY_TILE), 0)
        sel = (key > thr) | ((key == thr) & (kpos <= last_tie))
        penalty = jnp.concatenate([jnp.where(sel, 0.0, MASKED)] * GROUP, axis=1)
        s = [_dot_nt(k_ref[0, pl.ds(start, rows), kv * DH_C:(kv + 1) * DH_C], qg_ref[kv]) for kv in kvs]
        if near:
            sm = [s[kv] + bias_ref[qt - t, kv] + penalty for kv in kvs]
        else:
            sm = [s[kv] + penalty for kv in kvs]
        m_old = [m_ref[kv:kv + 1, :] for kv in kvs]
        m_new = [jnp.maximum(m_old[kv], jnp.max(sm[kv], axis=0, keepdims=True)) for kv in kvs]
        alpha = [jnp.exp(m_old[kv] - m_new[kv]) for kv in kvs]
        p = [jnp.exp(sm[kv] - m_new[kv]) for kv in kvs]
        pv = [lax.dot_general(v_ref[0, pl.ds(start, rows), kv * DH_C:(kv + 1) * DH_C], p[kv].astype(BF16),
                              (((0,), (0,)), ((), ())), preferred_element_type=F32) for kv in kvs]
        for kv in kvs:
            l_ref[kv:kv + 1, :] = alpha[kv] * l_ref[kv:kv + 1, :] + jnp.sum(p[kv], axis=0, keepdims=True)
            acc_ref[kv] = alpha[kv] * acc_ref[kv] + pv[kv]
            m_ref[kv:kv + 1, :] = m_new[kv]
        return carry

    n_far = jnp.maximum(n_vis - N_BIAS_NEAR, 0)
    n_pair = n_far // FAR_TILES
    lax.fori_loop(0, n_pair, lambda u, c: attend_tile(u * FAR_TILES, c, near=False, ntile=FAR_TILES), 0)
    lax.fori_loop(n_pair * FAR_TILES, n_far, partial(attend_tile, near=False), 0)
    lax.fori_loop(n_far, n_vis, partial(attend_tile, near=True), 0)
    for kv in kvs:
        out = (acc_ref[kv] / l_ref[kv:kv + 1, :]).T
        for g in range(GROUP):
            c0 = (kv * GROUP + g) * DH_C
            o_ref[:, c0:c0 + DH_C] = out[g * KEY_TILE:g * KEY_TILE + tq].astype(o_ref.dtype)


def dsa_bias_tiles(rel_bias):
    i = jnp.arange(KEY_TILE)
    d = jnp.arange(N_BIAS_NEAR + 1)
    rel = (i[None, :, None] - i[None, None, :]) - KEY_TILE * d[:, None, None]
    onehot = (t5_bucket(rel)[..., None] == jnp.arange(NUM_BUCKETS)).astype(F32)
    tiles = jnp.einsum('dkqn,nh->dkqh', onehot, rel_bias.astype(F32), precision=lax.Precision.HIGHEST)
    tiles = (tiles[:N_BIAS_NEAR] - tiles[N_BIAS_NEAR:]).reshape(N_BIAS_NEAR, KEY_TILE, KEY_TILE, H_KV, GROUP)
    return jnp.transpose(tiles, (0, 3, 1, 4, 2)).reshape(N_BIAS_NEAR, H_KV, KEY_TILE, GROUP * KEY_TILE)


def dsa_attention(proj, row0, b, t, k_all, v_all, kidx_all, bias_tiles, p_len, seq_len, topk):
    tq = KEY_TILE if t % KEY_TILE == 0 else t
    assert p_len % KEY_TILE == 0 and KEY_TILE % tq == 0 and (tq == KEY_TILE or t == tq) and row0 % tq == 0
    nb = t // tq
    lp = k_all.shape[1]
    assert lp % SCORE_TILE == 0 and lp >= p_len + nb * KEY_TILE
    assert 8 * (MAX_DISTANCE / 8) ** (7 / 8) < KEY_TILE * N_BIAS_NEAR - (KEY_TILE - 1)
    rb = row0 // tq
    kern = partial(_dsa_kernel, tq=tq, q0_tile=p_len // KEY_TILE, seq_len=seq_len, topk=topk)
    return pl.pallas_call(
        kern,
        grid=(b, nb),
        in_specs=[pl.BlockSpec((tq, W_C), lambda bi, j: (rb + bi * nb + j, COL["q_c"][0] // W_C)),
                  pl.BlockSpec((tq, H_IDX * D_IDX), lambda bi, j: (rb + bi * nb + j, COL["q_i"][0] // (H_IDX * D_IDX))),
                  pl.BlockSpec((tq, LANE), lambda bi, j: (rb + bi * nb + j, COL["k_i"][0] // LANE)),
                  pl.BlockSpec((1, lp, H_KV * DH_C), lambda bi, j: (bi, 0, 0)),
                  pl.BlockSpec((1, lp, H_KV * DH_C), lambda bi, j: (bi, 0, 0)),
                  pl.BlockSpec((1, lp, 2 * D_IDX), lambda bi, j: (bi, 0, 0)),
                  pl.BlockSpec((N_BIAS_NEAR, H_KV, KEY_TILE, GROUP * KEY_TILE), lambda bi, j: (0, 0, 0, 0))],
        out_specs=pl.BlockSpec((tq, W_C), lambda bi, j: (bi * nb + j, 0)),
        out_shape=jax.ShapeDtypeStruct((b * t, W_C), BF16),
        scratch_shapes=[pltpu.VMEM((lp // KEY_TILE, KEY_TILE, KEY_TILE), jnp.int32),
                        pltpu.VMEM((H_IDX // 2 * tq, 2 * D_IDX), BF16),
                        pltpu.VMEM((H_IDX // 2 * tq, 2 * SCORE_TILE), F32),
                        pltpu.VMEM((H_KV, GROUP * KEY_TILE, DH_C), BF16),
                        pltpu.VMEM((H_KV, DH_C, GROUP * KEY_TILE), F32),
                        pltpu.VMEM((SUBLANE, GROUP * KEY_TILE), F32),
                        pltpu.VMEM((SUBLANE, GROUP * KEY_TILE), F32)],
        compiler_params=_cparams(("parallel", "arbitrary")),
        name="dsa_attention",
    )(proj, proj, proj, k_all, v_all, kidx_all, bias_tiles)


def branches(proj_all, new_rows, row0, b, t, past_k, past_v, past_kidx, s_ret, s_delta, s_conv,
             ret_gn, conv_w, a_log, dt_bias, d_norm, bias_tiles):
    p_len = past_k.shape[1]
    seq_len = p_len + t
    topk = min(TOPK_MAX, seq_len // 4)
    lp = -(-seq_len // SCORE_TILE) * SCORE_TILE
    k_c, v_c, k_i, k_bf, v_bf, ki_bf = (x[row0:row0 + b * t].reshape(b, t, x.shape[-1]) for x in new_rows)

    o, w = COL["qkv_b"]
    qkv_tail = jnp.stack([proj_all[row0 + t - (CONV_W - 1) + j:row0 + b * t:t, o:o + w]
                          for j in range(CONV_W - 1)], axis=1)
    conv_new = jnp.concatenate([s_conv, qkv_tail], axis=1)[:, -(CONV_W - 1):]

    o_a, ret_new = retention(proj_all, row0, b, t, p_len, s_ret, ret_gn)
    o_b, delta_new = gated_delta(proj_all, row0, b, t, s_conv, s_delta, conv_w, a_log, dt_bias, d_norm)

    def all_keys(past, new, reps=1):
        if p_len == 0 and lp == seq_len:
            return new
        w = past.shape[-1] * reps
        past = jnp.tile(past.reshape(b, p_len, -1).astype(BF16), (1, 1, reps))
        return jnp.concatenate([past, new, jnp.zeros((b, lp - seq_len, w), BF16)], axis=1)

    o_c = dsa_attention(proj_all, row0, b, t, all_keys(past_k.reshape(b, p_len, KV_W), k_bf),
                        all_keys(past_v.reshape(b, p_len, KV_W), v_bf), all_keys(past_kidx, ki_bf, 2),
                        bias_tiles, p_len, seq_len, topk)
    caches = (k_c.reshape(b, t, H_KV, DH_C), v_c.reshape(b, t, H_KV, DH_C), k_i, ret_new, delta_new, conv_new)
    return jnp.stack([o_a, o_b, o_c]), caches


def _prep_w_in(w):
    parts = [w[:, MODEL_COL[n][0]:MODEL_COL[n][0] + MODEL_COL[n][1]] for n in _KERNEL_ORDER]
    parts.append(jnp.zeros((w.shape[0], D_IN_PAD - D_IN), w.dtype))
    return jnp.concatenate(parts, axis=1).astype(BF16)


def kernel(x_prompt, x_sample, cache_k, cache_v, cache_kidx, state_ret, state_delta, state_conv, norm_mix, w_in, ret_gn, conv_w, delta_a_log, delta_dt_bias, delta_norm, rel_bias, w_branch_a, w_branch_b, w_branch_c, w_out, norm_ffn, w_ffn_gate, w_ffn_up, w_ffn_down, norm_final):
    dt = x_prompt.dtype
    bp, tp, _ = x_prompt.shape
    bs, ts, _ = x_sample.shape
    depth = w_in.shape[0]
    n_p = bp * tp
    zk = jnp.zeros((bp, 0, H_KV, DH_C), dt)
    zkidx = jnp.zeros((bp, 0, D_IDX), dt)
    zret = jnp.zeros((bp, H_A, DK_A, DV_A), dt)
    zdelta = jnp.zeros((bp, H_B, DK_B, DV_B), dt)
    zconv = jnp.zeros((bp, CONV_W - 1, CONV_CH), dt)

    h = jnp.concatenate([x_prompt.reshape(n_p, D_MODEL), x_sample.reshape(bs * ts, D_MODEL)], axis=0)
    new_p, new_s = [], []
    bias_tiles = dsa_bias_tiles(rel_bias)
    for l in range(depth):
        u = rmsnorm(h, norm_mix[l], BF16)
        proj, *new_rows = matmul_in(u, _prep_w_in(w_in[l]))
        bw = (ret_gn[l], conv_w[l], delta_a_log[l], delta_dt_bias[l], delta_norm[l], bias_tiles)
        o_p, sp = branches(proj, new_rows, 0, bp, tp, zk, zk, zkidx, zret, zdelta, zconv, *bw)
        o_s, ss = branches(proj, new_rows, n_p, bs, ts, cache_k[l], cache_v[l], cache_kidx[l],
                           state_ret[l], state_delta[l], state_conv[l], *bw)
        new_p.append(sp)
        new_s.append(ss)
        o_abc = jnp.concatenate([o_p, o_s], axis=1)
        w_br = jnp.stack([w_branch_a[l], w_branch_b[l], w_branch_c[l]]).astype(BF16)
        merged = merge_branches(o_abc, w_br, proj)
        h = matmul_residual(merged, w_out[l].astype(BF16), h)
        u2 = rmsnorm(h, norm_ffn[l], BF16)
        act = ffn_gate_up(u2, w_ffn_gate[l].astype(BF16), w_ffn_up[l].astype(BF16))
        h = ffn_down(act, w_ffn_down[l].astype(BF16), h)

    y_prompt = rmsnorm(h, norm_final, F32, 0, n_p).reshape(bp, tp, D_MODEL)
    y_sample = rmsnorm(h, norm_final, F32, n_p, bs * ts).reshape(bs, ts, D_MODEL)
    k_p, v_p, kidx_p, ret_p, delta_p, conv_p = [jnp.stack([s[i] for s in new_p]) for i in range(6)]
    k_s, v_s, kidx_s, ret_s, delta_s, conv_s = [jnp.stack([s[i] for s in new_s]) for i in range(6)]
    return (y_prompt, y_sample, k_p, v_p, kidx_p, ret_p, delta_p, conv_p,
            k_s, v_s, kidx_s, ret_s, delta_s, conv_s)
```
